```python
import jax, jax.numpy as jnp
from jax import lax
import numpy as np

D_MODEL = 2048
BATCH = 4
SEQ = 4096
DEPTH = 4

N_MEM = 256
N_MIXERS = 3
HEAD_DIM = 128
BLOCK = 128
NORM_EPS = 1e-6
ROPE_THETA = 500000.0
ROPE_FRACTION = 4
SB_HEADS = D_MODEL // HEAD_DIM
DIL_PATTERNS = ((128, 1), (512, 4), (2048, 16))
DIL_HEADS_PER_GROUP = D_MODEL // (4 * HEAD_DIM)
DIL_HEADS = DIL_HEADS_PER_GROUP * len(DIL_PATTERNS)
SWA_HEAD_DIM = 64
SWA_Q_HEADS = D_MODEL // SWA_HEAD_DIM
SWA_KV_HEADS = SWA_Q_HEADS // 8
SWA_GROUP = SWA_Q_HEADS // SWA_KV_HEADS
SWA_WINDOW = 128
XA_HEADS = 4
XA_HEAD_DIM = 128
D_FF = 5632
N_LAYERS_A = (DEPTH + 2) // 3
N_LAYERS_B = (DEPTH + 1) // 3
N_LAYERS_C = DEPTH // 3

kernel_name = "hybrid_sb_dilated_swa_macaron"

F32 = jnp.float32


def rmsnorm(x, g):
    xf = x.astype(F32)
    y = xf * lax.rsqrt(jnp.mean(xf * xf, axis=-1, keepdims=True) + NORM_EPS)
    return (y * g.astype(F32)).astype(x.dtype)


def swiglu(x, w_gate_up, w_down):
    gate, up = jnp.split(x @ w_gate_up, 2, axis=-1)
    return (jax.nn.silu(gate) * up) @ w_down


def partial_rotary(x, positions):
    d = x.shape[-1]
    rot = d // ROPE_FRACTION
    half = rot // 2
    inv_freq = jnp.power(F32(ROPE_THETA), -jnp.arange(half, dtype=F32) * 2.0 / rot)
    ang = positions.astype(F32)[..., None] * inv_freq
    cos = jnp.cos(ang)[:, :, None, :]
    sin = jnp.sin(ang)[:, :, None, :]
    xf = x.astype(F32)
    x1, x2, rest = xf[..., :half], xf[..., half:rot], xf[..., rot:]
    out = jnp.concatenate([x1 * cos - x2 * sin, x2 * cos + x1 * sin, rest], axis=-1)
    return out.astype(x.dtype)


def banded_window_attention(q, k, v, max_dist, sinks=None):
    n, L, hk, g, d = q.shape
    nb = L // BLOCK
    qb = q.reshape(n, nb, BLOCK, hk, g, d)

    def with_prev(t):
        t = t.reshape(n, nb, BLOCK, hk, d)
        prev = jnp.pad(t, ((0, 0), (1, 0), (0, 0), (0, 0), (0, 0)))[:, :-1]
        return jnp.concatenate([prev, t], axis=2)

    kk, vv = with_prev(k), with_prev(v)
    scores = jnp.einsum('bnqhgd,bnkhd->bnhgqk', qb, kk, preferred_element_type=F32) * (d ** -0.5)
    dist = (BLOCK + jnp.arange(BLOCK))[:, None] - jnp.arange(2 * BLOCK)[None, :]
    in_band = (dist >= 0) & (dist <= max_dist)
    has_prev = (jnp.arange(nb)[:, None] > 0) | (jnp.arange(2 * BLOCK)[None, :] >= BLOCK)
    mask = in_band[None] & has_prev[:, None, :]
    scores = jnp.where(mask[None, :, None, None], scores, -jnp.inf)
    lse = jax.nn.logsumexp(scores, axis=-1)
    if sinks is not None:
        lse = jnp.logaddexp(lse, sinks.astype(F32)[None, None, :, :, None])
    p = jnp.exp(scores - lse[..., None])
    o = jnp.einsum('bnhgqk,bnkhd->bnqhgd', p, vv.astype(F32)).astype(q.dtype)
    return o.reshape(n, L, hk, g, d), lse.transpose(0, 1, 4, 2, 3).reshape(n, L, hk, g)


def stick_breaking_attention(q, k, v):
    b, s, h, d = q.shape
    nb = s // BLOCK
    qb = q.reshape(b, nb, BLOCK, h, d).transpose(1, 0, 2, 3, 4)
    kpos = jnp.arange(s)
    vf = v.astype(F32)

    def block_fn(args):
        q_blk, blk = args
        z = jnp.einsum('bqhd,bkhd->bhqk', q_blk, k, preferred_element_type=F32) * (d ** -0.5)
        qpos = blk * BLOCK + jnp.arange(BLOCK)
        causal = kpos[None, :] < qpos[:, None]
        log_keep = jnp.where(causal, -jax.nn.softplus(z), 0.0)
        tail = lax.cumsum(log_keep, axis=3, reverse=True) - log_keep
        a = jnp.where(causal, jnp.exp(jax.nn.log_sigmoid(z) + tail), 0.0)
        return jnp.einsum('bhqk,bkhd->bqhd', a, vf).astype(q.dtype)

    out = lax.map(block_fn, (qb, jnp.arange(nb, dtype=jnp.int32)))
    return out.transpose(1, 0, 2, 3, 4).reshape(b, s, h, d)


def stick_breaking_mixer(h, w_qkv, w_o):
    b, s, _ = h.shape
    qkv = (h @ w_qkv).reshape(b, s, 3, SB_HEADS, HEAD_DIM)
    o = stick_breaking_attention(qkv[:, :, 0], qkv[:, :, 1], qkv[:, :, 2])
    return o.reshape(b, s, SB_HEADS * HEAD_DIM) @ w_o


def dilated_group_attention(q, k, v, window, dil):
    b, s, h, d = q.shape
    L = s // dil
    Lp = -(-L // BLOCK) * BLOCK

    def to_classes(t):
        t = t.reshape(b, L, dil, h, d).transpose(0, 2, 1, 3, 4).reshape(b * dil, L, h, d)
        return jnp.pad(t, ((0, 0), (0, Lp - L), (0, 0), (0, 0)))

    qc, kc, vc = to_classes(q), to_classes(k), to_classes(v)
    o, lse = banded_window_attention(qc[:, :, :, None], kc, vc, window // dil)
    o = o[:, :L, :, 0].reshape(b, dil, L, h, d).transpose(0, 2, 1, 3, 4).reshape(b, s, h, d)
    lse = lse[:, :L, :, 0].reshape(b, dil, L, h).transpose(0, 2, 1, 3).reshape(b, s, h)
    return o, lse


def dilated_mixer(h, positions, w_qkv, w_o):
    b, s, _ = h.shape
    hg = DIL_HEADS_PER_GROUP
    qkv = (h @ w_qkv).reshape(b, s, 3, DIL_HEADS, HEAD_DIM)
    q = partial_rotary(qkv[:, :, 0], positions)
    k = partial_rotary(qkv[:, :, 1], positions)
    v = qkv[:, :, 2]
    outs, lses = [], []
    for g, (window, dil) in enumerate(DIL_PATTERNS):
        sl = slice(g * hg, (g + 1) * hg)
        o, lse = dilated_group_attention(q[:, :, sl], k[:, :, sl], v[:, :, sl], window, dil)
        outs.append(o)
        lses.append(lse)
    alpha = jax.nn.softmax(jnp.stack(lses, axis=0), axis=0)
    out = jnp.concatenate(
        [(o.astype(F32) * a[..., None]).astype(h.dtype) for o, a in zip(outs, alpha)], axis=2)
    return out.reshape(b, s, DIL_HEADS * HEAD_DIM) @ w_o


def swa_sink_mixer(h, positions, w_qkv, b_qkv, sinks, w_o, b_o):
    b, s, _ = h.shape
    nq = SWA_Q_HEADS * SWA_HEAD_DIM
    nk = SWA_KV_HEADS * SWA_HEAD_DIM
    qkv = h @ w_qkv + b_qkv
    q = partial_rotary(qkv[..., :nq].reshape(b, s, SWA_Q_HEADS, SWA_HEAD_DIM), positions)
    q = q.reshape(b, s, SWA_KV_HEADS, SWA_GROUP, SWA_HEAD_DIM)
    k = partial_rotary(qkv[..., nq:nq + nk].reshape(b, s, SWA_KV_HEADS, SWA_HEAD_DIM), positions)
    v = qkv[..., nq + nk:].reshape(b, s, SWA_KV_HEADS, SWA_HEAD_DIM)
    o, _ = banded_window_attention(q, k, v, SWA_WINDOW - 1, sinks.reshape(SWA_KV_HEADS, SWA_GROUP))
    return o.reshape(b, s, nq) @ w_o + b_o


def memory_cross_attention(h, mem_h, w_q, w_kv, w_o):
    b, s, _ = h.shape
    m = mem_h.shape[1]
    q = (h @ w_q).reshape(b, s, XA_HEADS, XA_HEAD_DIM)
    kv = (mem_h @ w_kv).reshape(b, m, 2, XA_HEADS, XA_HEAD_DIM)
    scores = jnp.einsum('bqhd,bkhd->bhqk', q, kv[:, :, 0], preferred_element_type=F32) * (XA_HEAD_DIM ** -0.5)
    p = jax.nn.softmax(scores, axis=-1)
    o = jnp.einsum('bhqk,bkhd->bqhd', p, kv[:, :, 1].astype(F32)).astype(h.dtype)
    return o.reshape(b, s, XA_HEADS * XA_HEAD_DIM) @ w_o


def setup_inputs(seed: int = 0) -> dict:
    key = jax.random.key(seed)
    ks = iter(jax.random.split(key, 32))
    D, F = D_MODEL, D_FF

    def normal(shape, scale):
        return jax.random.normal(next(ks), shape, F32) * scale

    def dense(shape, fan_in):
        return normal(shape, fan_in ** -0.5)

    def gain(shape):
        return 1.0 + normal(shape, 0.02)

    sb_w = SB_HEADS * HEAD_DIM
    dil_w = DIL_HEADS * HEAD_DIM
    swa_q = SWA_Q_HEADS * SWA_HEAD_DIM
    swa_qkv = swa_q + 2 * SWA_KV_HEADS * SWA_HEAD_DIM
    xa_w = XA_HEADS * XA_HEAD_DIM
    return {
        "x": normal((BATCH, SEQ, D), 1.0),
        "mem": normal((BATCH, N_MEM, D), 1.0),
        "positions": (jax.random.randint(next(ks), (BATCH, 1), 0, 1024, dtype=jnp.int32)
                      + jnp.arange(SEQ, dtype=jnp.int32)[None, :]),
        "ffn1_norm": gain((DEPTH, D)),
        "ffn1_w_gate_up": dense((DEPTH, D, 2 * F), D),
        "ffn1_w_down": dense((DEPTH, F, D), F),
        "mix_norm": gain((DEPTH, D)),
        "sb_w_qkv": dense((N_LAYERS_A, D, 3 * sb_w), D),
        "sb_w_o": dense((N_LAYERS_A, sb_w, D), sb_w),
        "dil_w_qkv": dense((N_LAYERS_B, D, 3 * dil_w), D),
        "dil_w_o": dense((N_LAYERS_B, dil_w, D), dil_w),
        "swa_w_qkv": dense((N_LAYERS_C, D, swa_qkv), D),
        "swa_b_qkv": normal((N_LAYERS_C, swa_qkv), 0.02),
        "swa_sinks": normal((N_LAYERS_C, SWA_Q_HEADS), 1.0),
        "swa_w_o": dense((N_LAYERS_C, swa_q, D), swa_q),
        "swa_b_o": normal((N_LAYERS_C, D), 0.02),
        "xattn_norm": gain((DEPTH, D)),
        "mem_norm": gain((DEPTH, D)),
        "xattn_w_q": dense((DEPTH, D, xa_w), D),
        "xattn_w_kv": dense((DEPTH, D, 2 * xa_w), D),
        "xattn_w_o": dense((DEPTH, xa_w, D), xa_w),
        "ffn2_norm": gain((DEPTH, D)),
        "ffn2_w_gate_up": dense((DEPTH, D, 2 * F), D),
        "ffn2_w_down": dense((DEPTH, F, D), F),
        "final_norm": gain((D,)),
    }


def reference(x, mem, positions, ffn1_norm, ffn1_w_gate_up, ffn1_w_down, mix_norm,
              sb_w_qkv, sb_w_o, dil_w_qkv, dil_w_o,
              swa_w_qkv, swa_b_qkv, swa_sinks, swa_w_o, swa_b_o,
              xattn_norm, mem_norm, xattn_w_q, xattn_w_kv, xattn_w_o,
              ffn2_norm, ffn2_w_gate_up, ffn2_w_down, final_norm):
    for i in range(DEPTH):
        x = x + 0.5 * swiglu(rmsnorm(x, ffn1_norm[i]), ffn1_w_gate_up[i], ffn1_w_down[i])
        h = rmsnorm(x, mix_norm[i])
        kind, j = i % N_MIXERS, i // N_MIXERS
        if kind == 0:
            y = stick_breaking_mixer(h, sb_w_qkv[j], sb_w_o[j])
        elif kind == 1:
            y = dilated_mixer(h, positions, dil_w_qkv[j], dil_w_o[j])
        else:
            y = swa_sink_mixer(h, positions, swa_w_qkv[j], swa_b_qkv[j], swa_sinks[j],
                               swa_w_o[j], swa_b_o[j])
        x = x + y
        x = x + memory_cross_attention(rmsnorm(x, xattn_norm[i]), rmsnorm(mem, mem_norm[i]),
                                       xattn_w_q[i], xattn_w_kv[i], xattn_w_o[i])
        x = x + 0.5 * swiglu(rmsnorm(x, ffn2_norm[i]), ffn2_w_gate_up[i], ffn2_w_down[i])
    return rmsnorm(x, final_norm)
```

```python
import functools

import jax
import jax.numpy as jnp
from jax import lax
from jax.experimental import pallas as pl
from jax.experimental.pallas import tpu as pltpu

F32 = jnp.float32
BF16 = jnp.bfloat16

NORM_EPS = 1e-6
ROPE_THETA = 500000.0
ROPE_FRACTION = 4
HEAD_DIM = 128
BLOCK = 128
DIL_PATTERNS = ((128, 1), (512, 4), (2048, 16))
SWA_HEAD_DIM = 64
SWA_GROUP = 8
SWA_WINDOW = 128
XA_HEADS = 4
XA_HEAD_DIM = 128

LANES = 128
V7X_VMEM_BYTES = 64 * 2**20
VMEM_LIMIT_BYTES = (V7X_VMEM_BYTES * 3) // 4


def _params(*semantics):
    return pltpu.CompilerParams(dimension_semantics=semantics,
                                vmem_limit_bytes=VMEM_LIMIT_BYTES)


def _tile(n, cap, mult=LANES):
    if n <= cap:
        return n
    t = (cap // mult) * mult
    while t > mult and n % t:
        t -= mult
    assert n % t == 0, (n, cap, mult)
    return t


def _rms(x, g):
    ms = jnp.mean(x * x, axis=-1, keepdims=True)
    return x * lax.rsqrt(ms + NORM_EPS) * g


def _norm_matmul_kernel(x_ref, g_ref, w_ref, b_ref, o_ref, xn_ref):
    @pl.when(pl.program_id(1) == 0)
    def _():
        xn_ref[...] = _rms(x_ref[...], g_ref[...]).astype(BF16)

    acc = jnp.dot(xn_ref[...], w_ref[...], preferred_element_type=F32)
    o_ref[...] = (acc + b_ref[...]).astype(o_ref.dtype)


def _norm_matmul(x, g, w, bias, out_dtype, tm_cap=1024, tn_cap=1024):
    m, d = x.shape
    n = w.shape[1]
    tm = _tile(m, tm_cap, 8)
    tn = _tile(n, tn_cap)
    if bias is None:
        bias = jnp.zeros((n,), F32)
    return pl.pallas_call(
        _norm_matmul_kernel,
        out_shape=jax.ShapeDtypeStruct((m, n), out_dtype),
        grid=(m // tm, n // tn),
        in_specs=[
            pl.BlockSpec((tm, d), lambda i, j: (i, 0)),
            pl.BlockSpec((1, d), lambda i, j: (0, 0)),
            pl.BlockSpec((d, tn), lambda i, j: (0, j)),
            pl.BlockSpec((1, tn), lambda i, j: (0, j)),
        ],
        out_specs=pl.BlockSpec((tm, tn), lambda i, j: (i, j)),
        scratch_shapes=[pltpu.VMEM((tm, d), BF16)],
        compiler_params=_params("parallel", "arbitrary"),
        name="norm_matmul",
    )(x, g.reshape(1, d), w, bias.reshape(1, n))


def _ffn_kernel(x_ref, g_ref, wg_ref, wu_ref, wd_ref, gf_ref, o_ref, xn_ref, acc_ref,
                *, final_norm):
    j = pl.program_id(1)

    @pl.when(j == 0)
    def _():
        xn_ref[...] = _rms(x_ref[...], g_ref[...]).astype(BF16)
        acc_ref[...] = jnp.zeros_like(acc_ref)

    xn = xn_ref[...]
    gate = jnp.dot(xn, wg_ref[...], preferred_element_type=F32)
    up = jnp.dot(xn, wu_ref[...], preferred_element_type=F32)
    h = gate * jax.nn.sigmoid(gate) * up
    acc_ref[...] += jnp.dot(h.astype(BF16), wd_ref[...], preferred_element_type=F32)

    @pl.when(j == pl.num_programs(1) - 1)
    def _():
        y = x_ref[...] + 0.5 * acc_ref[...]
        if final_norm:
            y = _rms(y, gf_ref[...])
        o_ref[...] = y


def _ffn(x, g, w_gu, w_d, final_g=None, tm_cap=512, tf_cap=512):
    m, d = x.shape
    f = w_d.shape[0]
    tm = _tile(m, tm_cap, 8)
    tf = _tile(f, tf_cap)
    nf = f // tf
    gf = jnp.ones((d,), F32) if final_g is None else final_g
    return pl.pallas_call(
        functools.partial(_ffn_kernel, final_norm=final_g is not None),
        out_shape=jax.ShapeDtypeStruct((m, d), F32),
        grid=(m // tm, nf),
        in_specs=[
            pl.BlockSpec((tm, d), lambda i, j: (i, 0)),
            pl.BlockSpec((1, d), lambda i, j: (0, 0)),
            pl.BlockSpec((d, tf), lambda i, j: (0, j)),
            pl.BlockSpec((d, tf), lambda i, j: (0, j + nf)),
            pl.BlockSpec((tf, d), lambda i, j: (j, 0)),
            pl.BlockSpec((1, d), lambda i, j: (0, 0)),
        ],
        out_specs=pl.BlockSpec((tm, d), lambda i, j: (i, 0)),
        scratch_shapes=[pltpu.VMEM((tm, d), BF16), pltpu.VMEM((tm, d), F32)],
        compiler_params=_params("parallel", "arbitrary"),
        name="ffn",
    )(x, g.reshape(1, d), w_gu, w_gu, w_d, gf.reshape(1, d))


def _proj_residual_kernel(a_ref, w_ref, x_ref, b_ref, o_ref):
    acc = jnp.dot(a_ref[...], w_ref[...], preferred_element_type=F32)
    o_ref[...] = x_ref[...] + acc + b_ref[...]


def _proj_residual(a, w, x, bias, tm_cap=1024, tn_cap=1024):
    m, k = a.shape
    d = w.shape[1]
    tm = _tile(m, tm_cap, 8)
    tn = _tile(d, tn_cap)
    if bias is None:
        bias = jnp.zeros((d,), F32)
    return pl.pallas_call(
        _proj_residual_kernel,
        out_shape=jax.ShapeDtypeStruct((m, d), F32),
        grid=(m // tm, d // tn),
        in_specs=[
            pl.BlockSpec((tm, k), lambda i, j: (i, 0)),
            pl.BlockSpec((k, tn), lambda i, j: (0, j)),
            pl.BlockSpec((tm, tn), lambda i, j: (i, j)),
            pl.BlockSpec((1, tn), lambda i, j: (0, j)),
        ],
        out_specs=pl.BlockSpec((tm, tn), lambda i, j: (i, j)),
        compiler_params=_params("parallel", "arbitrary"),
        name="proj_residual",
    )(a, w, x, bias.reshape(1, d))


def _sb_kernel(q_ref, k_ref, v_ref, tri_ref, o_ref, acc_ref, car_ref, *, tq, tk, scale):
    row0 = pl.program_id(2) * tq
    acc_ref[...] = jnp.zeros_like(acc_ref)
    car_ref[...] = jnp.zeros_like(car_ref)
    tri = tri_ref[...]

    def tile(r0, ks, masked):
        nr = tq - r0
        q = q_ref[r0:tq, :]
        k = k_ref[pl.ds(ks, tk), :]
        v = v_ref[pl.ds(ks, tk), :]
        z = lax.dot_general(q, k, (((1,), (1,)), ((), ())),
                            preferred_element_type=F32) * scale
        lk = -(jnp.maximum(z, 0.0) + jnp.log(1.0 + jnp.exp(-jnp.abs(z))))
        if masked:
            qpos = row0 + r0 + lax.broadcasted_iota(jnp.int32, (nr, tk), 0)
            kpos = ks + lax.broadcasted_iota(jnp.int32, (nr, tk), 1)
            causal = kpos < qpos
            lk = jnp.where(causal, lk, 0.0)
        hi = lk.astype(BF16)
        lo = (lk - hi.astype(F32)).astype(BF16)
        ssum = (jnp.dot(hi, tri, preferred_element_type=F32)
                + jnp.dot(lo, tri, preferred_element_type=F32))
        car = car_ref[r0:tq, :]
        a = jnp.exp(z + ssum + jnp.concatenate([car] * (tk // LANES), axis=1))
        if masked:
            a = jnp.where(causal, a, 0.0)
        acc_ref[r0:tq, :] += jnp.dot(a.astype(BF16), v, preferred_element_type=F32)
        car_ref[r0:tq, :] = car + jnp.broadcast_to(ssum[:, 0:1], (nr, LANES))

    for d in reversed(range(tq // tk)):
        tile(d * tk, pl.multiple_of(row0 + d * tk, tk), True)

    nfull = row0 // tk

    def body(t, c):
        tile(0, pl.multiple_of((nfull - 1 - t) * tk, tk), False)
        return c

    lax.fori_loop(0, nfull, body, 0)
    o_ref[...] = acc_ref[...].astype(o_ref.dtype)


def _sb_attention(qkv, heads, tq_cap=512, tk_cap=256):
    b, s, _ = qkv.shape
    tq = _tile(s, tq_cap)
    tk = _tile(tq, tk_cap)
    tri = jnp.tril(jnp.ones((tk, tk), F32)).astype(BF16)
    kern = functools.partial(_sb_kernel, tq=tq, tk=tk, scale=HEAD_DIM ** -0.5)
    return pl.pallas_call(
        kern,
        out_shape=jax.ShapeDtypeStruct((b, s, heads * HEAD_DIM), BF16),
        grid=(b, heads, s // tq),
        in_specs=[
            pl.BlockSpec((None, tq, HEAD_DIM), lambda bi, h, i: (bi, i, h)),
            pl.BlockSpec((None, s, HEAD_DIM), lambda bi, h, i: (bi, 0, heads + h)),
            pl.BlockSpec((None, s, HEAD_DIM), lambda bi, h, i: (bi, 0, 2 * heads + h)),
            pl.BlockSpec((tk, tk), lambda bi, h, i: (0, 0)),
        ],
        out_specs=pl.BlockSpec((None, tq, HEAD_DIM), lambda bi, h, i: (bi, i, h)),
        scratch_shapes=[pltpu.VMEM((tq, HEAD_DIM), F32), pltpu.VMEM((tq, LANES), F32)],
        compiler_params=_params("parallel", "parallel", "arbitrary"),
        name="sb_attention",
    )(qkv, qkv, qkv, tri)


def _rope_table_kernel(pos_ref, freq_ref, sign_ref, cos_ref, sin_ref):
    ang = pos_ref[...].astype(F32) * freq_ref[...]
    cos_ref[...] = jnp.cos(ang)
    sin_ref[...] = jnp.sin(ang) * sign_ref[...]


def _rope_tables(positions, head_dim):
    m = positions.shape[0]
    rot = head_dim // ROPE_FRACTION
    half = rot // 2
    inv_freq = jnp.power(F32(ROPE_THETA), -jnp.arange(half, dtype=F32) * 2.0 / rot)
    lane = jnp.arange(LANES) % head_dim
    freq = jnp.where(lane < rot, inv_freq[lane % half], 0.0).astype(F32)
    sign = jnp.where(lane < half, -1.0, jnp.where(lane < rot, 1.0, 0.0)).astype(F32)
    tm = _tile(m, 2048, 8)
    out = jax.ShapeDtypeStruct((m, LANES), F32)
    return pl.pallas_call(
        _rope_table_kernel,
        out_shape=(out, out),
        grid=(m // tm,),
        in_specs=[
            pl.BlockSpec((tm, 1), lambda i: (i, 0)),
            pl.BlockSpec((1, LANES), lambda i: (0, 0)),
            pl.BlockSpec((1, LANES), lambda i: (0, 0)),
        ],
        out_specs=(pl.BlockSpec((tm, LANES), lambda i: (i, 0)),
                   pl.BlockSpec((tm, LANES), lambda i: (i, 0))),
        compiler_params=_params("parallel"),
        name="rope_tables",
    )(positions, freq.reshape(1, LANES), sign.reshape(1, LANES))


def _rope(x, cos, sin, head_dim):
    half = head_dim // ROPE_FRACTION // 2
    lane = lax.broadcasted_iota(jnp.int32, x.shape, 1) & (head_dim - 1)
    ahead = pltpu.roll(x, LANES - half, 1)
    behind = pltpu.roll(x, half, 1)
    return x * cos + jnp.where(lane < half, ahead, behind) * sin


def _band_kernel(q_ref, k_ref, v_ref, cos_ref, sin_ref, o_ref, lse_ref, qr_ref, kr_ref,
                 *, length, max_dist, scale):
    cos = cos_ref[...]
    sin = sin_ref[...]
    qr_ref[...] = _rope(q_ref[...].astype(F32), cos, sin, HEAD_DIM).astype(BF16)
    kr_ref[...] = _rope(k_ref[...].astype(F32), cos, sin, HEAD_DIM).astype(BF16)

    def block(q0, k0, nk):
        q = qr_ref[pl.ds(q0, BLOCK), :]
        k = kr_ref[pl.ds(k0, nk), :]
        v = v_ref[pl.ds(k0, nk), :]
        s = lax.dot_general(q, k, (((1,), (1,)), ((), ())),
                            preferred_element_type=F32) * scale
        dist = ((nk - BLOCK) + lax.broadcasted_iota(jnp.int32, (BLOCK, nk), 0)
                - lax.broadcasted_iota(jnp.int32, (BLOCK, nk), 1))
        s = jnp.where((dist >= 0) & (dist <= max_dist), s, -jnp.inf)
        m = jnp.max(s, axis=-1, keepdims=True)
        p = jnp.exp(s - m)
        l = jnp.sum(p, axis=-1, keepdims=True)
        o = jnp.dot(p.astype(BF16), v, preferred_element_type=F32) / l
        o_ref[pl.ds(q0, BLOCK), :] = o
        lse_ref[pl.ds(q0, BLOCK), :] = jnp.broadcast_to(m + jnp.log(l), (BLOCK, LANES))

    block(0, 0, BLOCK)

    def body(a, c):
        block(pl.multiple_of(a * BLOCK, BLOCK), pl.multiple_of((a - 1) * BLOCK, BLOCK),
              2 * BLOCK)
        return c

    lax.fori_loop(1, length // BLOCK, body, 0)


def _band_attention(qkv, cos, sin, group, hpg, n_heads, window, dil):
    b, s, n = qkv.shape
    length = s // dil
    assert length % BLOCK == 0 and window % dil == 0 and window // dil <= BLOCK
    nblk = n // LANES
    qkv_c = qkv.reshape(b, length, dil * n)
    cos_c = cos.reshape(b, length, dil * LANES)
    sin_c = sin.reshape(b, length, dil * LANES)
    h0 = group * hpg

    def col(which):
        return lambda bi, p, h: (bi, 0, p * nblk + which * n_heads + h0 + h)

    tab = pl.BlockSpec((None, length, LANES), lambda bi, p, h: (bi, 0, p))
    out = pl.BlockSpec((None, length, LANES), lambda bi, p, h: (bi, 0, p * hpg + h))
    shape = jax.ShapeDtypeStruct((b, length, dil * hpg * LANES), F32)
    kern = functools.partial(_band_kernel, length=length, max_dist=window // dil,
                             scale=HEAD_DIM ** -0.5)
    o, lse = pl.pallas_call(
        kern,
        out_shape=(shape, shape),
        grid=(b, dil, hpg),
        in_specs=[
            pl.BlockSpec((None, length, LANES), col(0)),
            pl.BlockSpec((None, length, LANES), col(1)),
            pl.BlockSpec((None, length, LANES), col(2)),
            tab, tab,
        ],
        out_specs=(out, out),
        scratch_shapes=[pltpu.VMEM((length, LANES), BF16), pltpu.VMEM((length, LANES), BF16)],
        compiler_params=_params("parallel", "parallel", "parallel"),
        name=f"band_attention_d{dil}",
    )(qkv_c, qkv_c, qkv_c, cos_c, sin_c)
    return (o.reshape(b * s, hpg * LANES), lse.reshape(b * s, hpg * LANES))


def _dil_out_kernel(*refs, groups):
    o_refs = refs[:groups]
    l_refs = refs[groups:2 * groups]
    w_ref, x_ref, out_ref, a_ref = refs[2 * groups:]

    @pl.when(pl.program_id(1) == 0)
    def _():
        lses = [r[...] for r in l_refs]
        mx = functools.reduce(jnp.maximum, lses)
        es = [jnp.exp(l - mx) for l in lses]
        inv = 1.0 / functools.reduce(lambda u, v: u + v, es)
        w = o_refs[0].shape[1]
        for g in range(groups):
            a_ref[:, g * w:(g + 1) * w] = (o_refs[g][...] * (es[g] * inv)).astype(BF16)

    out_ref[...] = x_ref[...] + jnp.dot(a_ref[...], w_ref[...], preferred_element_type=F32)


def _dil_out(outs, lses, w_o, x, tm_cap=512, tn_cap=1024):
    m, d = x.shape
    groups = len(outs)
    gw = outs[0].shape[1]
    k = groups * gw
    tm = _tile(m, tm_cap, 8)
    tn = _tile(d, tn_cap)
    part = pl.BlockSpec((tm, gw), lambda i, j: (i, 0))
    return pl.pallas_call(
        functools.partial(_dil_out_kernel, groups=groups),
        out_shape=jax.ShapeDtypeStruct((m, d), F32),
        grid=(m // tm, d // tn),
        in_specs=[part] * (2 * groups) + [
            pl.BlockSpec((k, tn), lambda i, j: (0, j)),
            pl.BlockSpec((tm, tn), lambda i, j: (i, j)),
        ],
        out_specs=pl.BlockSpec((tm, tn), lambda i, j: (i, j)),
        scratch_shapes=[pltpu.VMEM((tm, k), BF16)],
        compiler_params=_params("parallel", "arbitrary"),
        name="dil_out",
    )(*outs, *lses, w_o, x)


def _swa_kernel(sink_ref, q_ref, kc_ref, kp_ref, vc_ref, vp_ref, cc_ref, cp_ref, sc_ref, sp_ref,
                o_ref, qr_ref, kr_ref, vr_ref, *, tq, scale):
    pair = pl.program_id(1)
    has_prev = pl.program_id(2) > 0
    d = SWA_HEAD_DIM
    cos = cc_ref[...]
    sin = sc_ref[...]
    kr_ref[0:BLOCK, :] = _rope(kp_ref[...].astype(F32), cp_ref[...], sp_ref[...], d).astype(BF16)
    kr_ref[BLOCK:, :] = _rope(kc_ref[...].astype(F32), cos, sin, d).astype(BF16)
    vr_ref[0:BLOCK, :] = vp_ref[...]
    vr_ref[BLOCK:, :] = vc_ref[...]
    for c in range(q_ref.shape[1] // LANES):
        sl = slice(c * LANES, (c + 1) * LANES)
        qr_ref[:, sl] = _rope(q_ref[:, sl].astype(F32), cos, sin, d).astype(BF16)

    heads_per_kv = SWA_GROUP
    for kv in range(LANES // d):
        for g in range(heads_per_kv):
            h = kv * heads_per_kv + g
            sink = sink_ref[pair * (2 * heads_per_kv) + h]
            for r in range(tq // BLOCK):
                q = qr_ref[r * BLOCK:(r + 1) * BLOCK, h * d:(h + 1) * d]
                k = kr_ref[r * BLOCK:(r + 2) * BLOCK, kv * d:(kv + 1) * d]
                v = vr_ref[r * BLOCK:(r + 2) * BLOCK, kv * d:(kv + 1) * d]
                s = lax.dot_general(q, k, (((1,), (1,)), ((), ())),
                                    preferred_element_type=F32) * scale
                qi = lax.broadcasted_iota(jnp.int32, (BLOCK, 2 * BLOCK), 0)
                kj = lax.broadcasted_iota(jnp.int32, (BLOCK, 2 * BLOCK), 1)
                dist = BLOCK + qi - kj
                mask = (dist >= 0) & (dist <= SWA_WINDOW - 1)
                if r == 0:
                    mask = mask & ((kj >= BLOCK) | has_prev)
                s = jnp.where(mask, s, -jnp.inf)
                m = jnp.maximum(jnp.max(s, axis=-1, keepdims=True), sink)
                p = jnp.exp(s - m)
                l = jnp.sum(p, axis=-1, keepdims=True) + jnp.exp(sink - m)
                o = jnp.dot(p.astype(BF16), v, preferred_element_type=F32) / l
                o_ref[r * BLOCK:(r + 1) * BLOCK, h * d:(h + 1) * d] = o.astype(o_ref.dtype)


def _swa_attention(qkv, cos, sin, sinks, q_heads, tq_cap=512):
    b, s, _ = qkv.shape
    d = SWA_HEAD_DIM
    kv_heads = q_heads // SWA_GROUP
    per_blk = LANES // d
    assert kv_heads % per_blk == 0
    nq = q_heads * d
    qw = per_blk * SWA_GROUP * d
    kcol = nq // LANES
    vcol = (nq + kv_heads * d) // LANES
    tq = _tile(s, tq_cap)
    rb = tq // BLOCK

    def cur(col0):
        return pl.BlockSpec((None, tq, LANES), lambda bi, p, i: (bi, i, col0 + p))

    def prev(col0):
        return pl.BlockSpec((None, BLOCK, LANES),
                            lambda bi, p, i: (bi, jnp.maximum(i * rb - 1, 0), col0 + p))

    tab_c = pl.BlockSpec((None, tq, LANES), lambda bi, p, i: (bi, i, 0))
    tab_p = pl.BlockSpec((None, BLOCK, LANES), lambda bi, p, i: (bi, jnp.maximum(i * rb - 1, 0), 0))
    kern = functools.partial(_swa_kernel, tq=tq, scale=d ** -0.5)
    return pl.pallas_call(
        kern,
        out_shape=jax.ShapeDtypeStruct((b, s, nq), BF16),
        grid=(b, kv_heads // per_blk, s // tq),
        in_specs=[
            pl.BlockSpec(memory_space=pltpu.SMEM),
            pl.BlockSpec((None, tq, qw), lambda bi, p, i: (bi, i, p)),
            cur(kcol), prev(kcol), cur(vcol), prev(vcol),
            tab_c, tab_p, tab_c, tab_p,
        ],
        out_specs=pl.BlockSpec((None, tq, qw), lambda bi, p, i: (bi, i, p)),
        scratch_shapes=[pltpu.VMEM((tq, qw), BF16),
                        pltpu.VMEM((tq + BLOCK, LANES), BF16),
                        pltpu.VMEM((tq + BLOCK, LANES), BF16)],
        compiler_params=_params("parallel", "parallel", "parallel"),
        name="swa_attention",
    )(sinks, qkv, qkv, qkv, qkv, qkv, cos, cos, sin, sin)


def _xattn_kernel(x_ref, g_ref, wq_ref, kv_ref, wo_ref, o_ref, a_ref, *, scale):
    x = x_ref[...]
    q = jnp.dot(_rms(x, g_ref[...]).astype(BF16), wq_ref[...],
                preferred_element_type=F32).astype(BF16)
    width = XA_HEADS * XA_HEAD_DIM
    for h in range(XA_HEADS):
        sl = slice(h * XA_HEAD_DIM, (h + 1) * XA_HEAD_DIM)
        k = kv_ref[:, sl]
        v = kv_ref[:, width + h * XA_HEAD_DIM:width + (h + 1) * XA_HEAD_DIM]
        s = lax.dot_general(q[:, sl], k, (((1,), (1,)), ((), ())),
                            preferred_element_type=F32) * scale
        m = jnp.max(s, axis=-1, keepdims=True)
        p = jnp.exp(s - m)
        l = jnp.sum(p, axis=-1, keepdims=True)
        a_ref[:, sl] = (jnp.dot(p.astype(BF16), v, preferred_element_type=F32) / l).astype(BF16)
    o_ref[...] = x + jnp.dot(a_ref[...], wo_ref[...], preferred_element_type=F32)


def _xattn(x, g, w_q, kv, w_o, seq, tm_cap=512):
    m, d = x.shape
    n_mem = kv.shape[1]
    width = XA_HEADS * XA_HEAD_DIM
    tm = _tile(seq, tm_cap, 8)
    per_batch = seq // tm
    return pl.pallas_call(
        functools.partial(_xattn_kernel, scale=XA_HEAD_DIM ** -0.5),
        out_shape=jax.ShapeDtypeStruct((m, d), F32),
        grid=(m // tm,),
        in_specs=[
            pl.BlockSpec((tm, d), lambda i: (i, 0)),
            pl.BlockSpec((1, d), lambda i: (0, 0)),
            pl.BlockSpec((d, width), lambda i: (0, 0)),
            pl.BlockSpec((None, n_mem, 2 * width), lambda i: (i // per_batch, 0, 0)),
            pl.BlockSpec((width, d), lambda i: (0, 0)),
        ],
        out_specs=pl.BlockSpec((tm, d), lambda i: (i, 0)),
        scratch_shapes=[pltpu.VMEM((tm, width), BF16)],
        compiler_params=_params("parallel"),
        name="xattn",
    )(x, g.reshape(1, d), w_q, kv, w_o)


def kernel(x, mem, positions, ffn1_norm, ffn1_w_gate_up, ffn1_w_down, mix_norm, sb_w_qkv, sb_w_o, dil_w_qkv, dil_w_o, swa_w_qkv, swa_b_qkv, swa_sinks, swa_w_o, swa_b_o, xattn_norm, mem_norm, xattn_w_q, xattn_w_kv, xattn_w_o, ffn2_norm, ffn2_w_gate_up, ffn2_w_down, final_norm):
    b, s, d = x.shape
    m = b * s
    depth = ffn1_norm.shape[0]
    n_mem = mem.shape[1]
    sb_heads = d // HEAD_DIM
    dil_hpg = d // (4 * HEAD_DIM)
    dil_heads = dil_hpg * len(DIL_PATTERNS)
    swa_q_heads = d // SWA_HEAD_DIM

    xs = x.reshape(m, d)
    mem2 = mem.reshape(b * n_mem, d)
    pos = positions.reshape(m, 1)
    if depth > 1:
        dil_cos, dil_sin = (t.reshape(b, s, LANES) for t in _rope_tables(pos, HEAD_DIM))
    if depth > 2:
        swa_cos, swa_sin = (t.reshape(b, s, LANES) for t in _rope_tables(pos, SWA_HEAD_DIM))

    for i in range(depth):
        xs = _ffn(xs, ffn1_norm[i], ffn1_w_gate_up[i].astype(BF16), ffn1_w_down[i].astype(BF16))

        kind, j = i % 3, i // 3
        if kind == 0:
            qkv = _norm_matmul(xs, mix_norm[i], sb_w_qkv[j].astype(BF16), None, BF16)
            o = _sb_attention(qkv.reshape(b, s, -1), sb_heads)
            xs = _proj_residual(o.reshape(m, -1), sb_w_o[j].astype(BF16), xs, None)
        elif kind == 1:
            qkv = _norm_matmul(xs, mix_norm[i], dil_w_qkv[j].astype(BF16), None, BF16)
            qkv = qkv.reshape(b, s, -1)
            outs, lses = [], []
            for g, (window, dil) in enumerate(DIL_PATTERNS):
                o, lse = _band_attention(qkv, dil_cos, dil_sin, g, dil_hpg, dil_heads, window, dil)
                outs.append(o)
                lses.append(lse)
            xs = _dil_out(outs, lses, dil_w_o[j].astype(BF16), xs)
        else:
            qkv = _norm_matmul(xs, mix_norm[i], swa_w_qkv[j].astype(BF16), swa_b_qkv[j], BF16)
            o = _swa_attention(qkv.reshape(b, s, -1), swa_cos, swa_sin, swa_sinks[j], swa_q_heads)
            xs = _proj_residual(o.reshape(m, -1), swa_w_o[j].astype(BF16), xs, swa_b_o[j])

        kv = _norm_matmul(mem2, mem_norm[i], xattn_w_kv[i].astype(BF16), None, BF16)
        xs = _xattn(xs, xattn_norm[i], xattn_w_q[i].astype(BF16), kv.reshape(b, n_mem, -1),
                    xattn_w_o[i].astype(BF16), s)

        xs = _ffn(xs, ffn2_norm[i], ffn2_w_gate_up[i].astype(BF16), ffn2_w_down[i].astype(BF16),
                  final_g=final_norm if i == depth - 1 else None)

    return xs.reshape(b, s, d)
```

```python
import functools

import jax
import jax.numpy as jnp
from jax import lax
from jax.experimental import pallas as pl
from jax.experimental.pallas import tpu as pltpu

F32 = jnp.float32
BF16 = jnp.bfloat16

NORM_EPS = 1e-6
ROPE_THETA = 500000.0
ROPE_FRACTION = 4
LOG2E = 1.4426950408889634
HEAD_DIM = 128
BLOCK = 128
DIL_PATTERNS = ((128, 1), (512, 4), (2048, 16))
SWA_HEAD_DIM = 64
SWA_GROUP = 8
SWA_WINDOW = 128
XA_HEADS = 4
XA_HEAD_DIM = 128

LANES = 128
V7X_VMEM_BYTES = 64 * 2**20
VMEM_LIMIT_BYTES = (V7X_VMEM_BYTES * 3) // 4


def _params(*semantics):
    return pltpu.CompilerParams(dimension_semantics=semantics,
                                vmem_limit_bytes=VMEM_LIMIT_BYTES)


def _tile(n, cap, mult=LANES):
    if n <= cap:
        return n
    t = (cap // mult) * mult
    while t > mult and n % t:
        t -= mult
    assert n % t == 0, (n, cap, mult)
    return t


def _rms(x, g):
    ms = jnp.mean(x * x, axis=-1, keepdims=True)
    return x * lax.rsqrt(ms + NORM_EPS) * g


def _norm_matmul_kernel(x_ref, g_ref, w_ref, b_ref, o_ref, xn_ref):
    @pl.when(pl.program_id(1) == 0)
    def _():
        xn_ref[...] = _rms(x_ref[...], g_ref[...]).astype(BF16)

    acc = jnp.dot(xn_ref[...], w_ref[...], preferred_element_type=F32)
    o_ref[...] = (acc + b_ref[...]).astype(o_ref.dtype)


def _norm_matmul(x, g, w, bias, out_dtype, tm_cap=1024, tn_cap=1024):
    m, d = x.shape
    n = w.shape[1]
    tm = _tile(m, tm_cap, 8)
    tn = _tile(n, tn_cap)
    if bias is None:
        bias = jnp.zeros((n,), F32)
    return pl.pallas_call(
        _norm_matmul_kernel,
        out_shape=jax.ShapeDtypeStruct((m, n), out_dtype),
        grid=(m // tm, n // tn),
        in_specs=[
            pl.BlockSpec((tm, d), lambda i, j: (i, 0)),
            pl.BlockSpec((1, d), lambda i, j: (0, 0)),
            pl.BlockSpec((d, tn), lambda i, j: (0, j)),
            pl.BlockSpec((1, tn), lambda i, j: (0, j)),
        ],
        out_specs=pl.BlockSpec((tm, tn), lambda i, j: (i, j)),
        scratch_shapes=[pltpu.VMEM((tm, d), BF16)],
        compiler_params=_params("parallel", "arbitrary"),
        name="norm_matmul",
    )(x, g.reshape(1, d), w, bias.reshape(1, n))


def _ffn_kernel(x_ref, g_ref, wg_ref, wu_ref, wd_ref, gf_ref, o_ref, xn_ref, acc_ref,
                *, final_norm):
    j = pl.program_id(1)

    @pl.when(j == 0)
    def _():
        xn_ref[...] = _rms(x_ref[...], g_ref[...]).astype(BF16)
        acc_ref[...] = jnp.zeros_like(acc_ref)

    xn = xn_ref[...]
    gate = jnp.dot(xn, wg_ref[...], preferred_element_type=F32)
    up = jnp.dot(xn, wu_ref[...], preferred_element_type=F32)
    h = gate * jax.nn.sigmoid(gate) * up
    acc_ref[...] += jnp.dot(h.astype(BF16), wd_ref[...], preferred_element_type=F32)

    @pl.when(j == pl.num_programs(1) - 1)
    def _():
        y = x_ref[...] + 0.5 * acc_ref[...]
        if final_norm:
            y = _rms(y, gf_ref[...])
        o_ref[...] = y


def _ffn(x, g, w_gu, w_d, final_g=None, tm_cap=512, tf_cap=512):
    m, d = x.shape
    f = w_d.shape[0]
    tm = _tile(m, tm_cap, 8)
    tf = _tile(f, tf_cap)
    nf = f // tf
    gf = jnp.ones((d,), F32) if final_g is None else final_g
    return pl.pallas_call(
        functools.partial(_ffn_kernel, final_norm=final_g is not None),
        out_shape=jax.ShapeDtypeStruct((m, d), F32),
        grid=(m // tm, nf),
        in_specs=[
            pl.BlockSpec((tm, d), lambda i, j: (i, 0)),
            pl.BlockSpec((1, d), lambda i, j: (0, 0)),
            pl.BlockSpec((d, tf), lambda i, j: (0, j)),
            pl.BlockSpec((d, tf), lambda i, j: (0, j + nf)),
            pl.BlockSpec((tf, d), lambda i, j: (j, 0)),
            pl.BlockSpec((1, d), lambda i, j: (0, 0)),
        ],
        out_specs=pl.BlockSpec((tm, d), lambda i, j: (i, 0)),
        scratch_shapes=[pltpu.VMEM((tm, d), BF16), pltpu.VMEM((tm, d), F32)],
        compiler_params=_params("parallel", "arbitrary"),
        name="ffn",
    )(x, g.reshape(1, d), w_gu, w_gu, w_d, gf.reshape(1, d))


def _proj_residual_kernel(a_ref, w_ref, x_ref, b_ref, o_ref):
    acc = jnp.dot(a_ref[...], w_ref[...], preferred_element_type=F32)
    o_ref[...] = x_ref[...] + acc + b_ref[...]


def _proj_residual(a, w, x, bias, tm_cap=1024, tn_cap=1024):
    m, k = a.shape
    d = w.shape[1]
    tm = _tile(m, tm_cap, 8)
    tn = _tile(d, tn_cap)
    if bias is None:
        bias = jnp.zeros((d,), F32)
    return pl.pallas_call(
        _proj_residual_kernel,
        out_shape=jax.ShapeDtypeStruct((m, d), F32),
        grid=(m // tm, d // tn),
        in_specs=[
            pl.BlockSpec((tm, k), lambda i, j: (i, 0)),
            pl.BlockSpec((k, tn), lambda i, j: (0, j)),
            pl.BlockSpec((tm, tn), lambda i, j: (i, j)),
            pl.BlockSpec((1, tn), lambda i, j: (0, j)),
        ],
        out_specs=pl.BlockSpec((tm, tn), lambda i, j: (i, j)),
        compiler_params=_params("parallel", "arbitrary"),
        name="proj_residual",
    )(a, w, x, bias.reshape(1, d))


def _sb_kernel(q_ref, k_ref, v_ref, tri_ref, o_ref, acc_ref, car_ref, *, tq, tk, scale):
    row0 = pl.program_id(2) * tq
    acc_ref[...] = jnp.zeros_like(acc_ref)
    car_ref[...] = jnp.zeros_like(car_ref)
    tri = tri_ref[...]
    nsub = tq // tk
    nheads = q_ref.shape[1] // HEAD_DIM

    def head_tile(hs, ks, masked):
        k = k_ref[pl.ds(ks, tq), hs]
        v = v_ref[pl.ds(ks, tq), hs]
        z2 = lax.dot_general(q_ref[:, hs], k, (((1,), (1,)), ((), ())),
                             preferred_element_type=F32) * (scale * LOG2E)
        neg_abs = lax.bitcast_convert_type(
            lax.bitcast_convert_type(z2, jnp.uint32) | jnp.uint32(0x80000000), F32)
        nl = jnp.maximum(z2, 0.0) + jnp.log2(1.0 + jnp.exp2(neg_abs))
        if masked:
            causal = (lax.broadcasted_iota(jnp.int32, (tq, tq), 1)
                      < lax.broadcasted_iota(jnp.int32, (tq, tq), 0))
            nl = jnp.where(causal, nl, 0.0)
        nlb = nl.astype(BF16)
        car = car_ref[:, hs]
        parts = [None] * nsub
        for j in reversed(range(nsub)):
            sl = slice(j * tk, (j + 1) * tk)
            ssum = jnp.dot(nlb[:, sl], tri, preferred_element_type=F32)
            parts[j] = jnp.exp2(z2[:, sl] - ssum - jnp.concatenate([car] * (tk // LANES), axis=1))
            car = car + jnp.broadcast_to(ssum[:, 0:1], (tq, LANES))
        a = jnp.concatenate(parts, axis=1)
        if masked:
            a = jnp.where(causal, a, 0.0)
        acc_ref[:, hs] += jnp.dot(a.astype(BF16), v, preferred_element_type=F32)
        car_ref[:, hs] = car

    def tile(ks, masked):
        for h in range(nheads):
            head_tile(slice(h * HEAD_DIM, (h + 1) * HEAD_DIM), ks, masked)

    tile(pl.multiple_of(row0, tq), True)

    def body(t, c):
        tile(pl.multiple_of(row0 - (t + 1) * tq, tq), False)
        return c

    lax.fori_loop(0, pl.program_id(2), body, 0)
    o_ref[...] = acc_ref[...].astype(o_ref.dtype)


def _sb_attention(qkv, heads, tq_cap=512, tk_cap=256, heads_per_step=2):
    b, s, _ = qkv.shape
    tq = _tile(s, tq_cap)
    tk = _tile(tq, tk_cap)
    hp = heads_per_step if heads % heads_per_step == 0 else 1
    hw = hp * HEAD_DIM
    hb = heads // hp
    tri = jnp.tril(jnp.ones((tk, tk), F32)).astype(BF16)
    kern = functools.partial(_sb_kernel, tq=tq, tk=tk, scale=HEAD_DIM ** -0.5)
    return pl.pallas_call(
        kern,
        out_shape=jax.ShapeDtypeStruct((b, s, heads * HEAD_DIM), BF16),
        grid=(b, hb, s // tq),
        in_specs=[
            pl.BlockSpec((None, tq, hw), lambda bi, h, i: (bi, i, h)),
            pl.BlockSpec((None, s, hw), lambda bi, h, i: (bi, 0, hb + h)),
            pl.BlockSpec((None, s, hw), lambda bi, h, i: (bi, 0, 2 * hb + h)),
            pl.BlockSpec((tk, tk), lambda bi, h, i: (0, 0)),
        ],
        out_specs=pl.BlockSpec((None, tq, hw), lambda bi, h, i: (bi, i, h)),
        scratch_shapes=[pltpu.VMEM((tq, hw), F32), pltpu.VMEM((tq, hw), F32)],
        compiler_params=_params("parallel", "parallel", "arbitrary"),
        name="sb_attention",
    )(qkv, qkv, qkv, tri)


def _rope_table_kernel(pos_ref, freq_ref, sign_ref, cos_ref, sin_ref):
    ang = pos_ref[...].astype(F32) * freq_ref[...]
    cos_ref[...] = jnp.cos(ang)
    sin_ref[...] = jnp.sin(ang) * sign_ref[...]


def _rope_tables(positions, head_dim):
    m = positions.shape[0]
    rot = head_dim // ROPE_FRACTION
    half = rot // 2
    inv_freq = jnp.power(F32(ROPE_THETA), -jnp.arange(half, dtype=F32) * 2.0 / rot)
    lane = jnp.arange(LANES) % head_dim
    freq = jnp.where(lane < rot, inv_freq[lane % half], 0.0).astype(F32)
    sign = jnp.where(lane < half, -1.0, jnp.where(lane < rot, 1.0, 0.0)).astype(F32)
    tm = _tile(m, 2048, 8)
    out = jax.ShapeDtypeStruct((m, LANES), F32)
    return pl.pallas_call(
        _rope_table_kernel,
        out_shape=(out, out),
        grid=(m // tm,),
        in_specs=[
            pl.BlockSpec((tm, 1), lambda i: (i, 0)),
            pl.BlockSpec((1, LANES), lambda i: (0, 0)),
            pl.BlockSpec((1, LANES), lambda i: (0, 0)),
        ],
        out_specs=(pl.BlockSpec((tm, LANES), lambda i: (i, 0)),
                   pl.BlockSpec((tm, LANES), lambda i: (i, 0))),
        compiler_params=_params("parallel"),
        name="rope_tables",
    )(positions, freq.reshape(1, LANES), sign.reshape(1, LANES))


def _rope(x, cos, sin, head_dim):
    half = head_dim // ROPE_FRACTION // 2
    lane = lax.broadcasted_iota(jnp.int32, x.shape, 1) & (head_dim - 1)
    ahead = pltpu.roll(x, LANES - half, 1)
    behind = pltpu.roll(x, half, 1)
    return x * cos + jnp.where(lane < half, ahead, behind) * sin


def _band_kernel(q_ref, k_ref, v_ref, cos_ref, sin_ref, o_ref, lse_ref, qr_ref, kr_ref,
                 *, length, max_dist, scale):
    cos = cos_ref[...]
    sin = sin_ref[...]
    qr_ref[...] = _rope(q_ref[...].astype(F32), cos, sin, HEAD_DIM).astype(BF16)
    kr_ref[...] = _rope(k_ref[...].astype(F32), cos, sin, HEAD_DIM).astype(BF16)

    def block(q0, k0, nk):
        q = qr_ref[pl.ds(q0, BLOCK), :]
        k = kr_ref[pl.ds(k0, nk), :]
        v = v_ref[pl.ds(k0, nk), :]
        s = lax.dot_general(q, k, (((1,), (1,)), ((), ())),
                            preferred_element_type=F32) * scale
        dist = ((nk - BLOCK) + lax.broadcasted_iota(jnp.int32, (BLOCK, nk), 0)
                - lax.broadcasted_iota(jnp.int32, (BLOCK, nk), 1))
        s = jnp.where((dist >= 0) & (dist <= max_dist), s, -jnp.inf)
        m = jnp.max(s, axis=-1, keepdims=True)
        p = jnp.exp(s - m)
        l = jnp.sum(p, axis=-1, keepdims=True)
        o = jnp.dot(p.astype(BF16), v, preferred_element_type=F32) / l
        o_ref[pl.ds(q0, BLOCK), :] = o
        lse_ref[pl.ds(q0, BLOCK), :] = jnp.broadcast_to(m + jnp.log(l), (BLOCK, LANES))

    block(0, 0, BLOCK)

    def body(a, c):
        block(pl.multiple_of(a * BLOCK, BLOCK), pl.multiple_of((a - 1) * BLOCK, BLOCK),
              2 * BLOCK)
        return c

    lax.fori_loop(1, length // BLOCK, body, 0)


def _band_attention(qkv, cos, sin, group, hpg, n_heads, window, dil):
    b, s, n = qkv.shape
    length = s // dil
    assert length % BLOCK == 0 and window % dil == 0 and window // dil <= BLOCK
    nblk = n // LANES
    qkv_c = qkv.reshape(b, length, dil * n)
    cos_c = cos.reshape(b, length, dil * LANES)
    sin_c = sin.reshape(b, length, dil * LANES)
    h0 = group * hpg

    def col(which):
        return lambda bi, p, h: (bi, 0, p * nblk + which * n_heads + h0 + h)

    tab = pl.BlockSpec((None, length, LANES), lambda bi, p, h: (bi, 0, p))
    out = pl.BlockSpec((None, length, LANES), lambda bi, p, h: (bi, 0, p * hpg + h))
    shape = jax.ShapeDtypeStruct((b, length, dil * hpg * LANES), F32)
    kern = functools.partial(_band_kernel, length=length, max_dist=window // dil,
                             scale=HEAD_DIM ** -0.5)
    o, lse = pl.pallas_call(
        kern,
        out_shape=(shape, shape),
        grid=(b, dil, hpg),
        in_specs=[
            pl.BlockSpec((None, length, LANES), col(0)),
            pl.BlockSpec((None, length, LANES), col(1)),
            pl.BlockSpec((None, length, LANES), col(2)),
            tab, tab,
        ],
        out_specs=(out, out),
        scratch_shapes=[pltpu.VMEM((length, LANES), BF16), pltpu.VMEM((length, LANES), BF16)],
        compiler_params=_params("parallel", "parallel", "parallel"),
        name=f"band_attention_d{dil}",
    )(qkv_c, qkv_c, qkv_c, cos_c, sin_c)
    return (o.reshape(b * s, hpg * LANES), lse.reshape(b * s, hpg * LANES))


def _dil_out_kernel(*refs, groups):
    o_refs = refs[:groups]
    l_refs = refs[groups:2 * groups]
    w_ref, x_ref, out_ref, a_ref = refs[2 * groups:]

    @pl.when(pl.program_id(1) == 0)
    def _():
        lses = [r[...] for r in l_refs]
        mx = functools.reduce(jnp.maximum, lses)
        es = [jnp.exp(l - mx) for l in lses]
        inv = 1.0 / functools.reduce(lambda u, v: u + v, es)
        w = o_refs[0].shape[1]
        for g in range(groups):
            a_ref[:, g * w:(g + 1) * w] = (o_refs[g][...] * (es[g] * inv)).astype(BF16)

    out_ref[...] = x_ref[...] + jnp.dot(a_ref[...], w_ref[...], preferred_element_type=F32)


def _dil_out(outs, lses, w_o, x, tm_cap=512, tn_cap=1024):
    m, d = x.shape
    groups = len(outs)
    gw = outs[0].shape[1]
    k = groups * gw
    tm = _tile(m, tm_cap, 8)
    tn = _tile(d, tn_cap)
    part = pl.BlockSpec((tm, gw), lambda i, j: (i, 0))
    return pl.pallas_call(
        functools.partial(_dil_out_kernel, groups=groups),
        out_shape=jax.ShapeDtypeStruct((m, d), F32),
        grid=(m // tm, d // tn),
        in_specs=[part] * (2 * groups) + [
            pl.BlockSpec((k, tn), lambda i, j: (0, j)),
            pl.BlockSpec((tm, tn), lambda i, j: (i, j)),
        ],
        out_specs=pl.BlockSpec((tm, tn), lambda i, j: (i, j)),
        scratch_shapes=[pltpu.VMEM((tm, k), BF16)],
        compiler_params=_params("parallel", "arbitrary"),
        name="dil_out",
    )(*outs, *lses, w_o, x)


def _swa_kernel(sink_ref, q_ref, kc_ref, kp_ref, vc_ref, vp_ref, cc_ref, cp_ref, sc_ref, sp_ref,
                o_ref, qr_ref, kr_ref, vr_ref, *, tq, scale):
    pair = pl.program_id(1)
    has_prev = pl.program_id(2) > 0
    d = SWA_HEAD_DIM
    cos = cc_ref[...]
    sin = sc_ref[...]
    kx = jnp.concatenate([_rope(kp_ref[...].astype(F32), cp_ref[...], sp_ref[...], d),
                          _rope(kc_ref[...].astype(F32), cos, sin, d)], axis=0)
    vx = jnp.concatenate([vp_ref[...], vc_ref[...]], axis=0).astype(F32)
    low = lax.broadcasted_iota(jnp.int32, kx.shape, 1) < d
    for src, dst in ((kx, kr_ref), (vx, vr_ref)):
        swapped = pltpu.roll(src, d, 1)
        dst[0] = jnp.where(low, src, 0.0).astype(BF16)
        dst[1] = jnp.where(low, 0.0, swapped).astype(BF16)
        dst[2] = jnp.where(low, swapped, 0.0).astype(BF16)
        dst[3] = jnp.where(low, 0.0, src).astype(BF16)
    nblk = q_ref.shape[1] // LANES
    for c in range(nblk):
        sl = slice(c * LANES, (c + 1) * LANES)
        qr_ref[:, sl] = _rope(q_ref[:, sl].astype(F32), cos, sin, d).astype(BF16)

    qi = lax.broadcasted_iota(jnp.int32, (BLOCK, 2 * BLOCK), 0)
    kj = lax.broadcasted_iota(jnp.int32, (BLOCK, 2 * BLOCK), 1)
    dist = BLOCK + qi - kj
    band = (dist >= 0) & (dist <= SWA_WINDOW - 1)
    band0 = band & ((kj >= BLOCK) | has_prev)
    per_kv = nblk // (LANES // d)
    for r in range(tq // BLOCK):
        rows = slice(r * BLOCK, (r + 1) * BLOCK)
        mask = jnp.concatenate([band0 if r == 0 else band] * per_kv, axis=0)
        for kv in range(LANES // d):
            blocks = range(kv * per_kv, (kv + 1) * per_kv)
            q = jnp.concatenate([qr_ref[rows, c * LANES:(c + 1) * LANES] for c in blocks], axis=0)
            out = None
            for half in range(2):
                k = kr_ref[2 * kv + half, r * BLOCK:(r + 2) * BLOCK, :]
                v = vr_ref[2 * kv + half, r * BLOCK:(r + 2) * BLOCK, :]
                sink = jnp.concatenate(
                    [jnp.full((BLOCK, 1), sink_ref[(pair * nblk + c) * 2 + half], F32)
                     for c in blocks], axis=0)
                s = lax.dot_general(q, k, (((1,), (1,)), ((), ())),
                                    preferred_element_type=F32) * scale
                s = jnp.where(mask, s, -jnp.inf)
                m = jnp.maximum(jnp.max(s, axis=-1, keepdims=True), sink)
                p = jnp.exp(s - m)
                l = jnp.sum(p, axis=-1, keepdims=True) + jnp.exp(sink - m)
                o = jnp.dot(p.astype(BF16), v, preferred_element_type=F32) / l
                out = o if out is None else out + o
            for n, c in enumerate(blocks):
                o_ref[rows, c * LANES:(c + 1) * LANES] = (
                    out[n * BLOCK:(n + 1) * BLOCK].astype(o_ref.dtype))


def _swa_attention(qkv, cos, sin, sinks, q_heads, tq_cap=512):
    b, s, _ = qkv.shape
    d = SWA_HEAD_DIM
    kv_heads = q_heads // SWA_GROUP
    per_blk = LANES // d
    assert kv_heads % per_blk == 0
    nq = q_heads * d
    qw = per_blk * SWA_GROUP * d
    kcol = nq // LANES
    vcol = (nq + kv_heads * d) // LANES
    tq = _tile(s, tq_cap)
    rb = tq // BLOCK

    def cur(col0):
        return pl.BlockSpec((None, tq, LANES), lambda bi, p, i: (bi, i, col0 + p))

    def prev(col0):
        return pl.BlockSpec((None, BLOCK, LANES),
                            lambda bi, p, i: (bi, jnp.maximum(i * rb - 1, 0), col0 + p))

    tab_c = pl.BlockSpec((None, tq, LANES), lambda bi, p, i: (bi, i, 0))
    tab_p = pl.BlockSpec((None, BLOCK, LANES), lambda bi, p, i: (bi, jnp.maximum(i * rb - 1, 0), 0))
    kern = functools.partial(_swa_kernel, tq=tq, scale=d ** -0.5)
    return pl.pallas_call(
        kern,
        out_shape=jax.ShapeDtypeStruct((b, s, nq), BF16),
        grid=(b, kv_heads // per_blk, s // tq),
        in_specs=[
            pl.BlockSpec(memory_space=pltpu.SMEM),
            pl.BlockSpec((None, tq, qw), lambda bi, p, i: (bi, i, p)),
            cur(kcol), prev(kcol), cur(vcol), prev(vcol),
            tab_c, tab_p, tab_c, tab_p,
        ],
        out_specs=pl.BlockSpec((None, tq, qw), lambda bi, p, i: (bi, i, p)),
        scratch_shapes=[pltpu.VMEM((tq, qw), BF16),
                        pltpu.VMEM((2 * per_blk, tq + BLOCK, LANES), BF16),
                        pltpu.VMEM((2 * per_blk, tq + BLOCK, LANES), BF16)],
        compiler_params=_params("parallel", "parallel", "parallel"),
        name="swa_attention",
    )(sinks, qkv, qkv, qkv, qkv, qkv, cos, cos, sin, sin)


def _xattn_kernel(x_ref, g_ref, wq_ref, kv_ref, wo_ref, o_ref, a_ref, *, scale):
    x = x_ref[...]
    q = jnp.dot(_rms(x, g_ref[...]).astype(BF16), wq_ref[...],
                preferred_element_type=F32).astype(BF16)
    width = XA_HEADS * XA_HEAD_DIM
    for h in range(XA_HEADS):
        sl = slice(h * XA_HEAD_DIM, (h + 1) * XA_HEAD_DIM)
        k = kv_ref[:, sl]
        v = kv_ref[:, width + h * XA_HEAD_DIM:width + (h + 1) * XA_HEAD_DIM]
        s = lax.dot_general(q[:, sl], k, (((1,), (1,)), ((), ())),
                            preferred_element_type=F32) * scale
        m = jnp.max(s, axis=-1, keepdims=True)
        p = jnp.exp(s - m)
        l = jnp.sum(p, axis=-1, keepdims=True)
        a_ref[:, sl] = (jnp.dot(p.astype(BF16), v, preferred_element_type=F32) / l).astype(BF16)
    o_ref[...] = x + jnp.dot(a_ref[...], wo_ref[...], preferred_element_type=F32)


def _xattn(x, g, w_q, kv, w_o, seq, tm_cap=512):
    m, d = x.shape
    n_mem = kv.shape[1]
    width = XA_HEADS * XA_HEAD_DIM
    tm = _tile(seq, tm_cap, 8)
    per_batch = seq // tm
    return pl.pallas_call(
        functools.partial(_xattn_kernel, scale=XA_HEAD_DIM ** -0.5),
        out_shape=jax.ShapeDtypeStruct((m, d), F32),
        grid=(m // tm,),
        in_specs=[
            pl.BlockSpec((tm, d), lambda i: (i, 0)),
            pl.BlockSpec((1, d), lambda i: (0, 0)),
            pl.BlockSpec((d, width), lambda i: (0, 0)),
            pl.BlockSpec((None, n_mem, 2 * width), lambda i: (i // per_batch, 0, 0)),
            pl.BlockSpec((width, d), lambda i: (0, 0)),
        ],
        out_specs=pl.BlockSpec((tm, d), lambda i: (i, 0)),
        scratch_shapes=[pltpu.VMEM((tm, width), BF16)],
        compiler_params=_params("parallel"),
        name="xattn",
    )(x, g.reshape(1, d), w_q, kv, w_o)


def kernel(x, mem, positions, ffn1_norm, ffn1_w_gate_up, ffn1_w_down, mix_norm, sb_w_qkv, sb_w_o, dil_w_qkv, dil_w_o, swa_w_qkv, swa_b_qkv, swa_sinks, swa_w_o, swa_b_o, xattn_norm, mem_norm, xattn_w_q, xattn_w_kv, xattn_w_o, ffn2_norm, ffn2_w_gate_up, ffn2_w_down, final_norm):
    b, s, d = x.shape
    m = b * s
    depth = ffn1_norm.shape[0]
    n_mem = mem.shape[1]
    sb_heads = d // HEAD_DIM
    dil_hpg = d // (4 * HEAD_DIM)
    dil_heads = dil_hpg * len(DIL_PATTERNS)
    swa_q_heads = d // SWA_HEAD_DIM

    xs = x.reshape(m, d)
    mem2 = mem.reshape(b * n_mem, d)
    pos = positions.reshape(m, 1)
    if depth > 1:
        dil_cos, dil_sin = (t.reshape(b, s, LANES) for t in _rope_tables(pos, HEAD_DIM))
    if depth > 2:
        swa_cos, swa_sin = (t.reshape(b, s, LANES) for t in _rope_tables(pos, SWA_HEAD_DIM))

    for i in range(depth):
        xs = _ffn(xs, ffn1_norm[i], ffn1_w_gate_up[i].astype(BF16), ffn1_w_down[i].astype(BF16))

        kind, j = i % 3, i // 3
        if kind == 0:
            qkv = _norm_matmul(xs, mix_norm[i], sb_w_qkv[j].astype(BF16), None, BF16)
            o = _sb_attention(qkv.reshape(b, s, -1), sb_heads)
            xs = _proj_residual(o.reshape(m, -1), sb_w_o[j].astype(BF16), xs, None)
        elif kind == 1:
            qkv = _norm_matmul(xs, mix_norm[i], dil_w_qkv[j].astype(BF16), None, BF16)
            qkv = qkv.reshape(b, s, -1)
            outs, lses = [], []
            for g, (window, dil) in enumerate(DIL_PATTERNS):
                o, lse = _band_attention(qkv, dil_cos, dil_sin, g, dil_hpg, dil_heads, window, dil)
                outs.append(o)
                lses.append(lse)
            xs = _dil_out(outs, lses, dil_w_o[j].astype(BF16), xs)
        else:
            qkv = _norm_matmul(xs, mix_norm[i], swa_w_qkv[j].astype(BF16), swa_b_qkv[j], BF16)
            o = _swa_attention(qkv.reshape(b, s, -1), swa_cos, swa_sin, swa_sinks[j], swa_q_heads)
            xs = _proj_residual(o.reshape(m, -1), swa_w_o[j].astype(BF16), xs, swa_b_o[j])

        kv = _norm_matmul(mem2, mem_norm[i], xattn_w_kv[i].astype(BF16), None, BF16)
        xs = _xattn(xs, xattn_norm[i], xattn_w_q[i].astype(BF16), kv.reshape(b, n_mem, -1),
                    xattn_w_o[i].astype(BF16), s)

        xs = _ffn(xs, ffn2_norm[i], ffn2_w_gate_up[i].astype(BF16), ffn2_w_down[i].astype(BF16),
                  final_g=final_norm if i == depth - 1 else None)

    return xs.reshape(b, s, d)
```

```python
import functools

import jax
import jax.numpy as jnp
from jax import lax
from jax.experimental import pallas as pl
from jax.experimental.pallas import tpu as pltpu

F32 = jnp.float32
BF16 = jnp.bfloat16

NORM_EPS = 1e-6
ROPE_THETA = 500000.0
ROPE_FRACTION = 4
LOG2E = 1.4426950408889634
HEAD_DIM = 128
BLOCK = 128
DIL_PATTERNS = ((128, 1), (512, 4), (2048, 16))
SWA_HEAD_DIM = 64
SWA_GROUP = 8
SWA_WINDOW = 128
XA_HEADS = 4
XA_HEAD_DIM = 128

LANES = 128
V7X_VMEM_BYTES = 64 * 2**20
VMEM_LIMIT_BYTES = (V7X_VMEM_BYTES * 3) // 4


def _params(*semantics):
    return pltpu.CompilerParams(dimension_semantics=semantics,
                                vmem_limit_bytes=VMEM_LIMIT_BYTES)


def _tile(n, cap, mult=LANES):
    if n <= cap:
        return n
    t = (cap // mult) * mult
    while t > mult and n % t:
        t -= mult
    assert n % t == 0, (n, cap, mult)
    return t


def _rms(x, g):
    ms = jnp.mean(x * x, axis=-1, keepdims=True)
    return x * lax.rsqrt(ms + NORM_EPS) * g


def _norm_matmul_kernel(x_ref, g_ref, w_ref, b_ref, o_ref, xn_ref):
    @pl.when(pl.program_id(1) == 0)
    def _():
        xn_ref[...] = _rms(x_ref[...], g_ref[...]).astype(BF16)

    acc = jnp.dot(xn_ref[...], w_ref[...], preferred_element_type=F32)
    o_ref[...] = (acc + b_ref[...]).astype(o_ref.dtype)


def _norm_matmul(x, g, w, layer, bias, out_dtype, tm_cap=1024, tn_cap=1024):
    m, d = x.shape
    n = w.shape[2]
    tm = _tile(m, tm_cap, 8)
    tn = _tile(n, tn_cap)
    if bias is None:
        bias = jnp.zeros((n,), F32)
    return pl.pallas_call(
        _norm_matmul_kernel,
        out_shape=jax.ShapeDtypeStruct((m, n), out_dtype),
        grid=(m // tm, n // tn),
        in_specs=[
            pl.BlockSpec((tm, d), lambda i, j: (i, 0)),
            pl.BlockSpec((1, d), lambda i, j: (0, 0)),
            pl.BlockSpec((None, d, tn), lambda i, j: (layer, 0, j)),
            pl.BlockSpec((1, tn), lambda i, j: (0, j)),
        ],
        out_specs=pl.BlockSpec((tm, tn), lambda i, j: (i, j)),
        scratch_shapes=[pltpu.VMEM((tm, d), BF16)],
        compiler_params=_params("parallel", "arbitrary"),
        name="norm_matmul",
    )(x, g.reshape(1, d), w, bias.reshape(1, n))


def _ffn_kernel(x_ref, g_ref, wg_ref, wu_ref, wd_ref, gf_ref, o_ref, xn_ref, acc_ref,
                *, final_norm):
    j = pl.program_id(1)

    @pl.when(j == 0)
    def _():
        xn_ref[...] = _rms(x_ref[...], g_ref[...]).astype(BF16)
        acc_ref[...] = jnp.zeros_like(acc_ref)

    xn = xn_ref[...]
    gate = jnp.dot(xn, wg_ref[...], preferred_element_type=F32)
    up = jnp.dot(xn, wu_ref[...], preferred_element_type=F32)
    h = gate * jax.nn.sigmoid(gate) * up
    acc_ref[...] += jnp.dot(h.astype(BF16), wd_ref[...], preferred_element_type=F32)

    @pl.when(j == pl.num_programs(1) - 1)
    def _():
        y = x_ref[...] + 0.5 * acc_ref[...]
        if final_norm:
            y = _rms(y, gf_ref[...])
        o_ref[...] = y


def _ffn(x, g, w_gu, w_d, layer, final_g=None, tm_cap=512, tf_cap=512):
    m, d = x.shape
    f = w_d.shape[1]
    tm = _tile(m, tm_cap, 8)
    tf = _tile(f, tf_cap)
    nf = f // tf
    gf = jnp.ones((d,), F32) if final_g is None else final_g
    return pl.pallas_call(
        functools.partial(_ffn_kernel, final_norm=final_g is not None),
        out_shape=jax.ShapeDtypeStruct((m, d), F32),
        grid=(m // tm, nf),
        in_specs=[
            pl.BlockSpec((tm, d), lambda i, j: (i, 0)),
            pl.BlockSpec((1, d), lambda i, j: (0, 0)),
            pl.BlockSpec((None, d, tf), lambda i, j: (layer, 0, j)),
            pl.BlockSpec((None, d, tf), lambda i, j: (layer, 0, j + nf)),
            pl.BlockSpec((None, tf, d), lambda i, j: (layer, j, 0)),
            pl.BlockSpec((1, d), lambda i, j: (0, 0)),
        ],
        out_specs=pl.BlockSpec((tm, d), lambda i, j: (i, 0)),
        scratch_shapes=[pltpu.VMEM((tm, d), BF16), pltpu.VMEM((tm, d), F32)],
        compiler_params=_params("parallel", "arbitrary"),
        name="ffn",
    )(x, g.reshape(1, d), w_gu, w_gu, w_d, gf.reshape(1, d))


def _proj_residual_kernel(a_ref, w_ref, x_ref, b_ref, o_ref):
    acc = jnp.dot(a_ref[...], w_ref[...], preferred_element_type=F32)
    o_ref[...] = x_ref[...] + acc + b_ref[...]


def _proj_residual(a, w, layer, x, bias, tm_cap=1024, tn_cap=1024):
    m, k = a.shape
    d = w.shape[2]
    tm = _tile(m, tm_cap, 8)
    tn = _tile(d, tn_cap)
    if bias is None:
        bias = jnp.zeros((d,), F32)
    return pl.pallas_call(
        _proj_residual_kernel,
        out_shape=jax.ShapeDtypeStruct((m, d), F32),
        grid=(m // tm, d // tn),
        in_specs=[
            pl.BlockSpec((tm, k), lambda i, j: (i, 0)),
            pl.BlockSpec((None, k, tn), lambda i, j: (layer, 0, j)),
            pl.BlockSpec((tm, tn), lambda i, j: (i, j)),
            pl.BlockSpec((1, tn), lambda i, j: (0, j)),
        ],
        out_specs=pl.BlockSpec((tm, tn), lambda i, j: (i, j)),
        compiler_params=_params("parallel", "arbitrary"),
        name="proj_residual",
    )(a, w, x, bias.reshape(1, d))


def _sb_kernel(q_ref, k_ref, v_ref, tri_ref, o_ref, acc_ref, car_ref, *, tq, tk, scale):
    row0 = pl.program_id(2) * tq
    acc_ref[...] = jnp.zeros_like(acc_ref)
    car_ref[...] = jnp.zeros_like(car_ref)
    tri = tri_ref[...]
    nsub = tq // tk
    nheads = q_ref.shape[1] // HEAD_DIM

    def head_tile(hs, ks, masked):
        k = k_ref[pl.ds(ks, tq), hs]
        v = v_ref[pl.ds(ks, tq), hs]
        z2 = lax.dot_general(q_ref[:, hs], k, (((1,), (1,)), ((), ())),
                             preferred_element_type=F32) * (scale * LOG2E)
        neg_abs = lax.bitcast_convert_type(
            lax.bitcast_convert_type(z2, jnp.uint32) | jnp.uint32(0x80000000), F32)
        nl = jnp.maximum(z2, 0.0) + jnp.log2(1.0 + jnp.exp2(neg_abs))
        if masked:
            causal = (lax.broadcasted_iota(jnp.int32, (tq, tq), 1)
                      < lax.broadcasted_iota(jnp.int32, (tq, tq), 0))
            nl = jnp.where(causal, nl, 0.0)
        nlb = nl.astype(BF16)
        car = car_ref[:, hs]
        parts = [None] * nsub
        for j in reversed(range(nsub)):
            sl = slice(j * tk, (j + 1) * tk)
            ssum = jnp.dot(nlb[:, sl], tri, preferred_element_type=F32)
            parts[j] = jnp.exp2(z2[:, sl] - ssum - jnp.concatenate([car] * (tk // LANES), axis=1))
            car = car + jnp.broadcast_to(ssum[:, 0:1], (tq, LANES))
        a = jnp.concatenate(parts, axis=1)
        if masked:
            a = jnp.where(causal, a, 0.0)
        acc_ref[:, hs] += jnp.dot(a.astype(BF16), v, preferred_element_type=F32)
        car_ref[:, hs] = car

    def tile(ks, masked):
        for h in range(nheads):
            head_tile(slice(h * HEAD_DIM, (h + 1) * HEAD_DIM), ks, masked)

    tile(pl.multiple_of(row0, tq), True)

    def body(t, c):
        tile(pl.multiple_of(row0 - (t + 1) * tq, tq), False)
        return c

    lax.fori_loop(0, pl.program_id(2), body, 0)
    o_ref[...] = acc_ref[...].astype(o_ref.dtype)


def _sb_attention(qkv, heads, tq_cap=512, tk_cap=256, heads_per_step=2):
    b, s, _ = qkv.shape
    tq = _tile(s, tq_cap)
    tk = _tile(tq, tk_cap)
    hp = heads_per_step if heads % heads_per_step == 0 else 1
    hw = hp * HEAD_DIM
    hb = heads // hp
    tri = jnp.tril(jnp.ones((tk, tk), F32)).astype(BF16)
    kern = functools.partial(_sb_kernel, tq=tq, tk=tk, scale=HEAD_DIM ** -0.5)
    return pl.pallas_call(
        kern,
        out_shape=jax.ShapeDtypeStruct((b, s, heads * HEAD_DIM), BF16),
        grid=(b, hb, s // tq),
        in_specs=[
            pl.BlockSpec((None, tq, hw), lambda bi, h, i: (bi, i, h)),
            pl.BlockSpec((None, s, hw), lambda bi, h, i: (bi, 0, hb + h)),
            pl.BlockSpec((None, s, hw), lambda bi, h, i: (bi, 0, 2 * hb + h)),
            pl.BlockSpec((tk, tk), lambda bi, h, i: (0, 0)),
        ],
        out_specs=pl.BlockSpec((None, tq, hw), lambda bi, h, i: (bi, i, h)),
        scratch_shapes=[pltpu.VMEM((tq, hw), F32), pltpu.VMEM((tq, hw), F32)],
        compiler_params=_params("parallel", "parallel", "arbitrary"),
        name="sb_attention",
    )(qkv, qkv, qkv, tri)


def _rope_table_kernel(pos_ref, freq_ref, sign_ref, cos_ref, sin_ref):
    ang = pos_ref[...].astype(F32) * freq_ref[...]
    cos_ref[...] = jnp.cos(ang)
    sin_ref[...] = jnp.sin(ang) * sign_ref[...]


def _rope_tables(positions, head_dim):
    m = positions.shape[0]
    rot = head_dim // ROPE_FRACTION
    half = rot // 2
    inv_freq = jnp.power(F32(ROPE_THETA), -jnp.arange(half, dtype=F32) * 2.0 / rot)
    lane = jnp.arange(LANES) % head_dim
    freq = jnp.where(lane < rot, inv_freq[lane % half], 0.0).astype(F32)
    sign = jnp.where(lane < half, -1.0, jnp.where(lane < rot, 1.0, 0.0)).astype(F32)
    tm = _tile(m, 2048, 8)
    out = jax.ShapeDtypeStruct((m, LANES), F32)
    return pl.pallas_call(
        _rope_table_kernel,
        out_shape=(out, out),
        grid=(m // tm,),
        in_specs=[
            pl.BlockSpec((tm, 1), lambda i: (i, 0)),
            pl.BlockSpec((1, LANES), lambda i: (0, 0)),
            pl.BlockSpec((1, LANES), lambda i: (0, 0)),
        ],
        out_specs=(pl.BlockSpec((tm, LANES), lambda i: (i, 0)),
                   pl.BlockSpec((tm, LANES), lambda i: (i, 0))),
        compiler_params=_params("parallel"),
        name="rope_tables",
    )(positions, freq.reshape(1, LANES), sign.reshape(1, LANES))


def _rope(x, cos, sin, head_dim):
    half = head_dim // ROPE_FRACTION // 2
    lane = lax.broadcasted_iota(jnp.int32, x.shape, 1) & (head_dim - 1)
    ahead = pltpu.roll(x, LANES - half, 1)
    behind = pltpu.roll(x, half, 1)
    return x * cos + jnp.where(lane < half, ahead, behind) * sin


def _band_kernel(q_ref, k_ref, v_ref, cos_ref, sin_ref, o_ref, lse_ref, qr_ref, kr_ref, vr_ref,
                 *, dil, length, max_dist, scale):
    kj = lax.broadcasted_iota(jnp.int32, (BLOCK, 2 * BLOCK), 1)
    dist = BLOCK + lax.broadcasted_iota(jnp.int32, (BLOCK, 2 * BLOCK), 0) - kj
    band = (dist >= 0) & (dist <= max_dist)
    band0 = band & (kj >= BLOCK)
    nblocks = length // BLOCK
    kr_ref[0:BLOCK, :] = jnp.zeros((BLOCK, LANES), BF16)
    vr_ref[0:BLOCK, :] = jnp.zeros((BLOCK, LANES), BF16)

    def block(p, a, mask):
        q0 = pl.multiple_of(a * BLOCK, BLOCK)
        q = qr_ref[pl.ds(q0, BLOCK), :]
        k = kr_ref[pl.ds(q0, 2 * BLOCK), :]
        v = vr_ref[pl.ds(q0, 2 * BLOCK), :]
        s = lax.dot_general(q, k, (((1,), (1,)), ((), ())),
                            preferred_element_type=F32) * scale
        s = jnp.where(mask, s, -jnp.inf)
        m = jnp.max(s, axis=-1, keepdims=True)
        e = jnp.exp(s - m)
        l = jnp.sum(e, axis=-1, keepdims=True)
        o = jnp.dot(e.astype(BF16), v, preferred_element_type=F32) / l
        dest = pl.ds(p + q0 * dil, BLOCK, stride=dil)
        o_ref[dest, :] = o
        lse_ref[dest, :] = jnp.broadcast_to(m + jnp.log(l), (BLOCK, LANES))

    def one_class(p, c):
        rows = pl.ds(p, length, stride=dil)
        cos = cos_ref[rows, :]
        sin = sin_ref[rows, :]
        qr_ref[...] = _rope(q_ref[rows, :], cos, sin, HEAD_DIM).astype(BF16)
        kr_ref[BLOCK:, :] = _rope(k_ref[rows, :], cos, sin, HEAD_DIM).astype(BF16)
        vr_ref[BLOCK:, :] = v_ref[rows, :].astype(BF16)
        block(p, 0, band0)
        for a in range(1, group):
            block(p, a, band)

        def trip(t, c2):
            for a in range(group):
                block(p, group * t + a, band)
            return c2

        lax.fori_loop(1, nblocks // group, trip, 0)
        return c

    group = min(4, nblocks)
    assert nblocks % group == 0
    lax.fori_loop(0, dil, one_class, 0)


def _band_attention(qkv, cos, sin, group, hpg, n_heads, window, dil):
    b, s, n = qkv.shape
    length = s // dil
    assert length % BLOCK == 0 and window % dil == 0 and window // dil <= BLOCK
    h0 = group * hpg

    def col(which):
        return pl.BlockSpec((None, s, LANES), lambda bi, h: (bi, 0, which * n_heads + h0 + h))

    tab = pl.BlockSpec((None, s, LANES), lambda bi, h: (bi, 0, 0))
    out = pl.BlockSpec((None, s, LANES), lambda bi, h: (bi, 0, h))
    shape = jax.ShapeDtypeStruct((b, s, hpg * LANES), F32)
    kern = functools.partial(_band_kernel, dil=dil, length=length, max_dist=window // dil,
                             scale=HEAD_DIM ** -0.5)
    o, lse = pl.pallas_call(
        kern,
        out_shape=(shape, shape),
        grid=(b, hpg),
        in_specs=[col(0), col(1), col(2), tab, tab],
        out_specs=(out, out),
        scratch_shapes=[pltpu.VMEM((length, LANES), BF16),
                        pltpu.VMEM((length + BLOCK, LANES), BF16),
                        pltpu.VMEM((length + BLOCK, LANES), BF16)],
        compiler_params=_params("parallel", "parallel"),
        name=f"band_attention_d{dil}",
    )(qkv, qkv, qkv, cos, sin)
    return (o.reshape(b * s, hpg * LANES), lse.reshape(b * s, hpg * LANES))


def _dil_out_kernel(*refs, groups):
    o_refs = refs[:groups]
    l_refs = refs[groups:2 * groups]
    w_ref, x_ref, out_ref, a_ref = refs[2 * groups:]

    @pl.when(pl.program_id(1) == 0)
    def _():
        lses = [r[...] for r in l_refs]
        mx = functools.reduce(jnp.maximum, lses)
        es = [jnp.exp(l - mx) for l in lses]
        inv = 1.0 / functools.reduce(lambda u, v: u + v, es)
        w = o_refs[0].shape[1]
        for g in range(groups):
            a_ref[:, g * w:(g + 1) * w] = (o_refs[g][...] * (es[g] * inv)).astype(BF16)

    out_ref[...] = x_ref[...] + jnp.dot(a_ref[...], w_ref[...], preferred_element_type=F32)


def _dil_out(outs, lses, w_o, layer, x, tm_cap=512, tn_cap=1024):
    m, d = x.shape
    groups = len(outs)
    gw = outs[0].shape[1]
    k = groups * gw
    tm = _tile(m, tm_cap, 8)
    tn = _tile(d, tn_cap)
    part = pl.BlockSpec((tm, gw), lambda i, j: (i, 0))
    return pl.pallas_call(
        functools.partial(_dil_out_kernel, groups=groups),
        out_shape=jax.ShapeDtypeStruct((m, d), F32),
        grid=(m // tm, d // tn),
        in_specs=[part] * (2 * groups) + [
            pl.BlockSpec((None, k, tn), lambda i, j: (layer, 0, j)),
            pl.BlockSpec((tm, tn), lambda i, j: (i, j)),
        ],
        out_specs=pl.BlockSpec((tm, tn), lambda i, j: (i, j)),
        scratch_shapes=[pltpu.VMEM((tm, k), BF16)],
        compiler_params=_params("parallel", "arbitrary"),
        name="dil_out",
    )(*outs, *lses, w_o, x)


def _swa_kernel(sink_ref, q_ref, kc_ref, kp_ref, vc_ref, vp_ref, cc_ref, cp_ref, sc_ref, sp_ref,
                o_ref, qr_ref, kr_ref, vr_ref, *, tq, scale):
    pair = pl.program_id(1)
    has_prev = pl.program_id(2) > 0
    d = SWA_HEAD_DIM
    cos = cc_ref[...]
    sin = sc_ref[...]
    kx = jnp.concatenate([_rope(kp_ref[...].astype(F32), cp_ref[...], sp_ref[...], d),
                          _rope(kc_ref[...].astype(F32), cos, sin, d)], axis=0)
    vx = jnp.concatenate([vp_ref[...], vc_ref[...]], axis=0).astype(F32)
    low = lax.broadcasted_iota(jnp.int32, kx.shape, 1) < d
    for src, dst in ((kx, kr_ref), (vx, vr_ref)):
        swapped = pltpu.roll(src, d, 1)
        dst[0] = jnp.where(low, src, 0.0).astype(BF16)
        dst[1] = jnp.where(low, 0.0, swapped).astype(BF16)
        dst[2] = jnp.where(low, swapped, 0.0).astype(BF16)
        dst[3] = jnp.where(low, 0.0, src).astype(BF16)
    nblk = q_ref.shape[1] // LANES
    for c in range(nblk):
        sl = slice(c * LANES, (c + 1) * LANES)
        qr_ref[:, sl] = _rope(q_ref[:, sl].astype(F32), cos, sin, d).astype(BF16)

    qi = lax.broadcasted_iota(jnp.int32, (BLOCK, 2 * BLOCK), 0)
    kj = lax.broadcasted_iota(jnp.int32, (BLOCK, 2 * BLOCK), 1)
    dist = BLOCK + qi - kj
    band = (dist >= 0) & (dist <= SWA_WINDOW - 1)
    band0 = band & ((kj >= BLOCK) | has_prev)
    per_kv = nblk // (LANES // d)
    for r in range(tq // BLOCK):
        rows = slice(r * BLOCK, (r + 1) * BLOCK)
        mask = jnp.concatenate([band0 if r == 0 else band] * per_kv, axis=0)
        for kv in range(LANES // d):
            blocks = range(kv * per_kv, (kv + 1) * per_kv)
            q = jnp.concatenate([qr_ref[rows, c * LANES:(c + 1) * LANES] for c in blocks], axis=0)
            out = None
            for half in range(2):
                k = kr_ref[2 * kv + half, r * BLOCK:(r + 2) * BLOCK, :]
                v = vr_ref[2 * kv + half, r * BLOCK:(r + 2) * BLOCK, :]
                sink = jnp.concatenate(
                    [jnp.full((BLOCK, 1), sink_ref[(pair * nblk + c) * 2 + half], F32)
                     for c in blocks], axis=0)
                s = lax.dot_general(q, k, (((1,), (1,)), ((), ())),
                                    preferred_element_type=F32) * scale
                s = jnp.where(mask, s, -jnp.inf)
                m = jnp.maximum(jnp.max(s, axis=-1, keepdims=True), sink)
                p = jnp.exp(s - m)
                l = jnp.sum(p, axis=-1, keepdims=True) + jnp.exp(sink - m)
                o = jnp.dot(p.astype(BF16), v, preferred_element_type=F32) / l
                out = o if out is None else out + o
            for n, c in enumerate(blocks):
                o_ref[rows, c * LANES:(c + 1) * LANES] = (
                    out[n * BLOCK:(n + 1) * BLOCK].astype(o_ref.dtype))


def _swa_attention(qkv, cos, sin, sinks, q_heads, tq_cap=512):
    b, s, _ = qkv.shape
    d = SWA_HEAD_DIM
    kv_heads = q_heads // SWA_GROUP
    per_blk = LANES // d
    assert kv_heads % per_blk == 0
    nq = q_heads * d
    qw = per_blk * SWA_GROUP * d
    kcol = nq // LANES
    vcol = (nq + kv_heads * d) // LANES
    tq = _tile(s, tq_cap)
    rb = tq // BLOCK

    def cur(col0):
        return pl.BlockSpec((None, tq, LANES), lambda bi, p, i: (bi, i, col0 + p))

    def prev(col0):
        return pl.BlockSpec((None, BLOCK, LANES),
                            lambda bi, p, i: (bi, jnp.maximum(i * rb - 1, 0), col0 + p))

    tab_c = pl.BlockSpec((None, tq, LANES), lambda bi, p, i: (bi, i, 0))
    tab_p = pl.BlockSpec((None, BLOCK, LANES), lambda bi, p, i: (bi, jnp.maximum(i * rb - 1, 0), 0))
    kern = functools.partial(_swa_kernel, tq=tq, scale=d ** -0.5)
    return pl.pallas_call(
        kern,
        out_shape=jax.ShapeDtypeStruct((b, s, nq), BF16),
        grid=(b, kv_heads // per_blk, s // tq),
        in_specs=[
            pl.BlockSpec(memory_space=pltpu.SMEM),
            pl.BlockSpec((None, tq, qw), lambda bi, p, i: (bi, i, p)),
            cur(kcol), prev(kcol), cur(vcol), prev(vcol),
            tab_c, tab_p, tab_c, tab_p,
        ],
        out_specs=pl.BlockSpec((None, tq, qw), lambda bi, p, i: (bi, i, p)),
        scratch_shapes=[pltpu.VMEM((tq, qw), BF16),
                        pltpu.VMEM((2 * per_blk, tq + BLOCK, LANES), BF16),
                        pltpu.VMEM((2 * per_blk, tq + BLOCK, LANES), BF16)],
        compiler_params=_params("parallel", "parallel", "parallel"),
        name="swa_attention",
    )(sinks, qkv, qkv, qkv, qkv, qkv, cos, cos, sin, sin)


def _xattn_kernel(x_ref, g_ref, wq_ref, kv_ref, wo_ref, o_ref, a_ref, *, scale):
    x = x_ref[...]
    q = jnp.dot(_rms(x, g_ref[...]).astype(BF16), wq_ref[...],
                preferred_element_type=F32).astype(BF16)
    width = XA_HEADS * XA_HEAD_DIM
    for h in range(XA_HEADS):
        sl = slice(h * XA_HEAD_DIM, (h + 1) * XA_HEAD_DIM)
        k = kv_ref[:, sl]
        v = kv_ref[:, width + h * XA_HEAD_DIM:width + (h + 1) * XA_HEAD_DIM]
        s = lax.dot_general(q[:, sl], k, (((1,), (1,)), ((), ())),
                            preferred_element_type=F32) * scale
        m = jnp.max(s, axis=-1, keepdims=True)
        p = jnp.exp(s - m)
        l = jnp.sum(p, axis=-1, keepdims=True)
        a_ref[:, sl] = (jnp.dot(p.astype(BF16), v, preferred_element_type=F32) / l).astype(BF16)
    o_ref[...] = x + jnp.dot(a_ref[...], wo_ref[...], preferred_element_type=F32)


def _xattn(x, g, w_q, kv, w_o, layer, seq, tm_cap=512):
    m, d = x.shape
    n_mem = kv.shape[1]
    width = XA_HEADS * XA_HEAD_DIM
    tm = _tile(seq, tm_cap, 8)
    per_batch = seq // tm
    return pl.pallas_call(
        functools.partial(_xattn_kernel, scale=XA_HEAD_DIM ** -0.5),
        out_shape=jax.ShapeDtypeStruct((m, d), F32),
        grid=(m // tm,),
        in_specs=[
            pl.BlockSpec((tm, d), lambda i: (i, 0)),
            pl.BlockSpec((1, d), lambda i: (0, 0)),
            pl.BlockSpec((None, d, width), lambda i: (layer, 0, 0)),
            pl.BlockSpec((None, n_mem, 2 * width), lambda i: (i // per_batch, 0, 0)),
            pl.BlockSpec((None, width, d), lambda i: (layer, 0, 0)),
        ],
        out_specs=pl.BlockSpec((tm, d), lambda i: (i, 0)),
        scratch_shapes=[pltpu.VMEM((tm, width), BF16)],
        compiler_params=_params("parallel"),
        name="xattn",
    )(x, g.reshape(1, d), w_q, kv, w_o)


def kernel(x, mem, positions, ffn1_norm, ffn1_w_gate_up, ffn1_w_down, mix_norm, sb_w_qkv, sb_w_o, dil_w_qkv, dil_w_o, swa_w_qkv, swa_b_qkv, swa_sinks, swa_w_o, swa_b_o, xattn_norm, mem_norm, xattn_w_q, xattn_w_kv, xattn_w_o, ffn2_norm, ffn2_w_gate_up, ffn2_w_down, final_norm):
    b, s, d = x.shape
    m = b * s
    depth = ffn1_norm.shape[0]
    n_mem = mem.shape[1]
    sb_heads = d // HEAD_DIM
    dil_hpg = d // (4 * HEAD_DIM)
    dil_heads = dil_hpg * len(DIL_PATTERNS)
    swa_q_heads = d // SWA_HEAD_DIM

    xs = x.reshape(m, d)
    mem2 = mem.reshape(b * n_mem, d)
    pos = positions.reshape(m, 1)
    if depth > 1:
        dil_cos, dil_sin = (t.reshape(b, s, LANES) for t in _rope_tables(pos, HEAD_DIM))
    if depth > 2:
        swa_cos, swa_sin = (t.reshape(b, s, LANES) for t in _rope_tables(pos, SWA_HEAD_DIM))

    (ffn1_w_gate_up, ffn1_w_down, ffn2_w_gate_up, ffn2_w_down, sb_w_qkv, sb_w_o, dil_w_qkv, dil_w_o,
     swa_w_qkv, swa_w_o, xattn_w_q, xattn_w_kv, xattn_w_o) = (
        w.astype(BF16) for w in
        (ffn1_w_gate_up, ffn1_w_down, ffn2_w_gate_up, ffn2_w_down, sb_w_qkv, sb_w_o, dil_w_qkv,
         dil_w_o, swa_w_qkv, swa_w_o, xattn_w_q, xattn_w_kv, xattn_w_o))

    for i in range(depth):
        xs = _ffn(xs, ffn1_norm[i], ffn1_w_gate_up, ffn1_w_down, i)

        kind, j = i % 3, i // 3
        if kind == 0:
            qkv = _norm_matmul(xs, mix_norm[i], sb_w_qkv, j, None, BF16)
            o = _sb_attention(qkv.reshape(b, s, -1), sb_heads)
            xs = _proj_residual(o.reshape(m, -1), sb_w_o, j, xs, None)
        elif kind == 1:
            qkv = _norm_matmul(xs, mix_norm[i], dil_w_qkv, j, None, F32)
            qkv = qkv.reshape(b, s, -1)
            outs, lses = [], []
            for g, (window, dil) in enumerate(DIL_PATTERNS):
                o, lse = _band_attention(qkv, dil_cos, dil_sin, g, dil_hpg, dil_heads, window, dil)
                outs.append(o)
                lses.append(lse)
            xs = _dil_out(outs, lses, dil_w_o, j, xs)
        else:
            qkv = _norm_matmul(xs, mix_norm[i], swa_w_qkv, j, swa_b_qkv[j], BF16)
            o = _swa_attention(qkv.reshape(b, s, -1), swa_cos, swa_sin, swa_sinks[j], swa_q_heads)
            xs = _proj_residual(o.reshape(m, -1), swa_w_o, j, xs, swa_b_o[j])

        kv = _norm_matmul(mem2, mem_norm[i], xattn_w_kv, i, None, BF16)
        xs = _xattn(xs, xattn_norm[i], xattn_w_q, kv.reshape(b, n_mem, -1), xattn_w_o, i, s)

        xs = _ffn(xs, ffn2_norm[i], ffn2_w_gate_up, ffn2_w_down, i,
                  final_g=final_norm if i == depth - 1 else None)

    return xs.reshape(b, s, d)
```

```python
import functools

import jax
import jax.numpy as jnp
from jax import lax
from jax.experimental import pallas as pl
from jax.experimental.pallas import tpu as pltpu

F32 = jnp.float32
BF16 = jnp.bfloat16

NORM_EPS = 1e-6
ROPE_THETA = 500000.0
ROPE_FRACTION = 4
LOG2E = 1.4426950408889634
HEAD_DIM = 128
BLOCK = 128
DIL_PATTERNS = ((128, 1), (512, 4), (2048, 16))
SWA_HEAD_DIM = 64
SWA_GROUP = 8
SWA_WINDOW = 128
XA_HEADS = 4
XA_HEAD_DIM = 128

LANES = 128
V7X_VMEM_BYTES = 64 * 2**20
VMEM_LIMIT_BYTES = (V7X_VMEM_BYTES * 3) // 4
FFN_VMEM_LIMIT_BYTES = (V7X_VMEM_BYTES * 15) // 16


def _params(*semantics, vmem_limit=VMEM_LIMIT_BYTES):
    return pltpu.CompilerParams(dimension_semantics=semantics, vmem_limit_bytes=vmem_limit)


def _tile(n, cap, mult=LANES):
    if n <= cap:
        return n
    t = (cap // mult) * mult
    while t > mult and n % t:
        t -= mult
    assert n % t == 0, (n, cap, mult)
    return t


def _rms(x, g):
    ms = jnp.mean(x * x, axis=-1, keepdims=True)
    return x * lax.rsqrt(ms + NORM_EPS) * g


def _norm_matmul_kernel(x_ref, g_ref, w_ref, b_ref, o_ref, xn_ref):
    @pl.when(pl.program_id(1) == 0)
    def _():
        xn_ref[...] = _rms(x_ref[...], g_ref[...]).astype(BF16)

    acc = jnp.dot(xn_ref[...], w_ref[...], preferred_element_type=F32)
    o_ref[...] = (acc + b_ref[...]).astype(o_ref.dtype)


def _norm_matmul(x, g, w, layer, bias, out_dtype, tm_cap=1024, tn_cap=1024):
    m, d = x.shape
    n = w.shape[2]
    tm = _tile(m, tm_cap, 8)
    tn = _tile(n, tn_cap)
    if bias is None:
        bias = jnp.zeros((n,), F32)
    return pl.pallas_call(
        _norm_matmul_kernel,
        out_shape=jax.ShapeDtypeStruct((m, n), out_dtype),
        grid=(m // tm, n // tn),
        in_specs=[
            pl.BlockSpec((tm, d), lambda i, j: (i, 0)),
            pl.BlockSpec((1, d), lambda i, j: (0, 0)),
            pl.BlockSpec((None, d, tn), lambda i, j: (layer, 0, j)),
            pl.BlockSpec((1, tn), lambda i, j: (0, j)),
        ],
        out_specs=pl.BlockSpec((tm, tn), lambda i, j: (i, j)),
        scratch_shapes=[pltpu.VMEM((tm, d), BF16)],
        compiler_params=_params("parallel", "arbitrary"),
        name="norm_matmul",
    )(x, g.reshape(1, d), w, bias.reshape(1, n))


def _ffn_kernel(x_ref, g_ref, wg_ref, wu_ref, wd_ref, gf_ref, o_ref, xn_ref, *, final_norm):
    j = pl.program_id(1)

    @pl.when(j == 0)
    def _():
        xn_ref[...] = _rms(x_ref[...], g_ref[...]).astype(BF16)
        o_ref[...] = jnp.zeros_like(o_ref)

    xn = xn_ref[...]
    gate = jnp.dot(xn, wg_ref[...], preferred_element_type=F32)
    up = jnp.dot(xn, wu_ref[...], preferred_element_type=F32)
    h = gate * jax.nn.sigmoid(gate) * up
    o_ref[...] += jnp.dot(h.astype(BF16), wd_ref[...], preferred_element_type=F32)

    @pl.when(j == pl.num_programs(1) - 1)
    def _():
        y = x_ref[...] + 0.5 * o_ref[...]
        if final_norm:
            y = _rms(y, gf_ref[...])
        o_ref[...] = y


def _ffn(x, g, w_gu, w_d, layer, final_g=None, tm_cap=1024, tf_cap=512):
    m, d = x.shape
    f = w_d.shape[1]
    if final_g is not None:
        tm_cap //= 2
    tm = _tile(m, tm_cap, 8)
    tf = _tile(f, tf_cap)
    nf = f // tf
    gf = jnp.ones((d,), F32) if final_g is None else final_g
    return pl.pallas_call(
        functools.partial(_ffn_kernel, final_norm=final_g is not None),
        out_shape=jax.ShapeDtypeStruct((m, d), F32),
        grid=(m // tm, nf),
        in_specs=[
            pl.BlockSpec((tm, d), lambda i, j: (i, 0)),
            pl.BlockSpec((1, d), lambda i, j: (0, 0)),
            pl.BlockSpec((None, d, tf), lambda i, j: (layer, 0, j)),
            pl.BlockSpec((None, d, tf), lambda i, j: (layer, 0, j + nf)),
            pl.BlockSpec((None, tf, d), lambda i, j: (layer, j, 0)),
            pl.BlockSpec((1, d), lambda i, j: (0, 0)),
        ],
        out_specs=pl.BlockSpec((tm, d), lambda i, j: (i, 0)),
        scratch_shapes=[pltpu.VMEM((tm, d), BF16)],
        compiler_params=_params("parallel", "arbitrary", vmem_limit=FFN_VMEM_LIMIT_BYTES),
        name="ffn",
    )(x, g.reshape(1, d), w_gu, w_gu, w_d, gf.reshape(1, d))


def _proj_residual_kernel(a_ref, w_ref, x_ref, b_ref, o_ref):
    acc = jnp.dot(a_ref[...], w_ref[...], preferred_element_type=F32)
    o_ref[...] = x_ref[...] + acc + b_ref[...]


def _proj_residual(a, w, layer, x, bias, tm_cap=1024, tn_cap=1024):
    m, k = a.shape
    d = w.shape[2]
    tm = _tile(m, tm_cap, 8)
    tn = _tile(d, tn_cap)
    if bias is None:
        bias = jnp.zeros((d,), F32)
    return pl.pallas_call(
        _proj_residual_kernel,
        out_shape=jax.ShapeDtypeStruct((m, d), F32),
        grid=(m // tm, d // tn),
        in_specs=[
            pl.BlockSpec((tm, k), lambda i, j: (i, 0)),
            pl.BlockSpec((None, k, tn), lambda i, j: (layer, 0, j)),
            pl.BlockSpec((tm, tn), lambda i, j: (i, j)),
            pl.BlockSpec((1, tn), lambda i, j: (0, j)),
        ],
        out_specs=pl.BlockSpec((tm, tn), lambda i, j: (i, j)),
        compiler_params=_params("parallel", "arbitrary"),
        name="proj_residual",
    )(a, w, x, bias.reshape(1, d))


def _sb_kernel(q_ref, k_ref, v_ref, tri_ref, o_ref, acc_ref, car_ref, z_ref, *, tq, tk, scale):
    row0 = pl.program_id(2) * tq
    acc_ref[...] = jnp.zeros_like(acc_ref)
    car_ref[...] = jnp.zeros_like(car_ref)
    tri = tri_ref[...]
    nsub = tq // tk
    nheads = q_ref.shape[1] // HEAD_DIM
    assert nheads % 2 == 0
    cols = [slice(h * HEAD_DIM, (h + 1) * HEAD_DIM) for h in range(nheads)]

    def logits(h, ks):
        return lax.dot_general(q_ref[:, cols[h]], k_ref[pl.ds(ks, tq), cols[h]],
                               (((1,), (1,)), ((), ())),
                               preferred_element_type=F32) * (scale * LOG2E)

    def tile(ks, ks_next, masked):
        if masked:
            causal = (lax.broadcasted_iota(jnp.int32, (tq, tq), 1)
                      < lax.broadcasted_iota(jnp.int32, (tq, tq), 0))
        for h, hs in enumerate(cols):
            if h + 1 < nheads:
                z_ref[(h + 1) % 2] = logits(h + 1, ks)
            else:
                z_ref[0] = logits(0, ks_next)
            zr = z_ref.at[h % 2]
            z2 = zr[...]
            neg_abs = lax.bitcast_convert_type(
                lax.bitcast_convert_type(z2, jnp.uint32) | jnp.uint32(0x80000000), F32)
            nl = jnp.maximum(z2, 0.0) + jnp.log2(1.0 + jnp.exp2(neg_abs))
            if masked:
                nl = jnp.where(causal, nl, 0.0)
            nlb = nl.astype(BF16)
            car = car_ref[:, hs]
            parts = [None] * nsub
            for j in reversed(range(nsub)):
                sl = slice(j * tk, (j + 1) * tk)
                ssum = jnp.dot(nlb[:, sl], tri, preferred_element_type=F32)
                parts[j] = jnp.exp2(zr[:, sl] - ssum
                                    - jnp.concatenate([car] * (tk // LANES), axis=1))
                car = car + jnp.broadcast_to(ssum[:, 0:1], (tq, LANES))
            a = jnp.concatenate(parts, axis=1)
            if masked:
                a = jnp.where(causal, a, 0.0)
            acc_ref[:, hs] += jnp.dot(a.astype(BF16), v_ref[pl.ds(ks, tq), hs],
                                      preferred_element_type=F32)
            car_ref[:, hs] = car

    def start(t):
        return pl.multiple_of(jnp.maximum(row0 - (t + 1) * tq, 0), tq)

    z_ref[0] = logits(0, pl.multiple_of(row0, tq))
    tile(pl.multiple_of(row0, tq), start(0), True)

    def body(t, c):
        tile(start(t), start(t + 1), False)
        return c

    lax.fori_loop(0, pl.program_id(2), body, 0)
    o_ref[...] = acc_ref[...].astype(o_ref.dtype)


def _sb_attention(qkv, heads, tq_cap=512, tk_cap=256, heads_per_step=4):
    b, s, _ = qkv.shape
    tq = _tile(s, tq_cap)
    tk = _tile(tq, tk_cap)
    hp = heads_per_step if heads % heads_per_step == 0 else 1
    hw = hp * HEAD_DIM
    hb = heads // hp
    tri = jnp.tril(jnp.ones((tk, tk), F32)).astype(BF16)
    kern = functools.partial(_sb_kernel, tq=tq, tk=tk, scale=HEAD_DIM ** -0.5)
    return pl.pallas_call(
        kern,
        out_shape=jax.ShapeDtypeStruct((b, s, heads * HEAD_DIM), BF16),
        grid=(b, hb, s // tq),
        in_specs=[
            pl.BlockSpec((None, tq, hw), lambda bi, h, i: (bi, i, h)),
            pl.BlockSpec((None, s, hw), lambda bi, h, i: (bi, 0, hb + h)),
            pl.BlockSpec((None, s, hw), lambda bi, h, i: (bi, 0, 2 * hb + h)),
            pl.BlockSpec((tk, tk), lambda bi, h, i: (0, 0)),
        ],
        out_specs=pl.BlockSpec((None, tq, hw), lambda bi, h, i: (bi, i, h)),
        scratch_shapes=[pltpu.VMEM((tq, hw), F32), pltpu.VMEM((tq, hw), F32),
                        pltpu.VMEM((2, tq, tq), F32)],
        compiler_params=_params("parallel", "parallel", "arbitrary"),
        name="sb_attention",
    )(qkv, qkv, qkv, tri)


def _rope_table_kernel(pos_ref, freq_ref, sign_ref, cos_ref, sin_ref):
    ang = pos_ref[...].astype(F32) * freq_ref[...]
    cos_ref[...] = jnp.cos(ang)
    sin_ref[...] = jnp.sin(ang) * sign_ref[...]


def _rope_tables(positions, head_dim):
    m = positions.shape[0]
    rot = head_dim // ROPE_FRACTION
    half = rot // 2
    inv_freq = jnp.power(F32(ROPE_THETA), -jnp.arange(half, dtype=F32) * 2.0 / rot)
    lane = jnp.arange(LANES) % head_dim
    freq = jnp.where(lane < rot, inv_freq[lane % half], 0.0).astype(F32)
    sign = jnp.where(lane < half, -1.0, jnp.where(lane < rot, 1.0, 0.0)).astype(F32)
    tm = _tile(m, 2048, 8)
    out = jax.ShapeDtypeStruct((m, LANES), F32)
    return pl.pallas_call(
        _rope_table_kernel,
        out_shape=(out, out),
        grid=(m // tm,),
        in_specs=[
            pl.BlockSpec((tm, 1), lambda i: (i, 0)),
            pl.BlockSpec((1, LANES), lambda i: (0, 0)),
            pl.BlockSpec((1, LANES), lambda i: (0, 0)),
        ],
        out_specs=(pl.BlockSpec((tm, LANES), lambda i: (i, 0)),
                   pl.BlockSpec((tm, LANES), lambda i: (i, 0))),
        compiler_params=_params("parallel"),
        name="rope_tables",
    )(positions, freq.reshape(1, LANES), sign.reshape(1, LANES))


def _rope(x, cos, sin, head_dim):
    half = head_dim // ROPE_FRACTION // 2
    lane = lax.broadcasted_iota(jnp.int32, x.shape, 1) & (head_dim - 1)
    ahead = pltpu.roll(x, LANES - half, 1)
    behind = pltpu.roll(x, half, 1)
    return x * cos + jnp.where(lane < half, ahead, behind) * sin


def _band_kernel(q_ref, k_ref, v_ref, cos_ref, sin_ref, o_ref, lse_ref, qr_ref, kr_ref, vr_ref,
                 *, dil, length, max_dist, scale):
    kj = lax.broadcasted_iota(jnp.int32, (BLOCK, 2 * BLOCK), 1)
    dist = BLOCK + lax.broadcasted_iota(jnp.int32, (BLOCK, 2 * BLOCK), 0) - kj
    band = (dist >= 0) & (dist <= max_dist)
    band0 = band & (kj >= BLOCK)
    nblocks = length // BLOCK
    kr_ref[0:BLOCK, :] = jnp.zeros((BLOCK, LANES), BF16)
    vr_ref[0:BLOCK, :] = jnp.zeros((BLOCK, LANES), BF16)

    def block(p, a, mask):
        q0 = pl.multiple_of(a * BLOCK, BLOCK)
        q = qr_ref[pl.ds(q0, BLOCK), :]
        k = kr_ref[pl.ds(q0, 2 * BLOCK), :]
        v = vr_ref[pl.ds(q0, 2 * BLOCK), :]
        s = lax.dot_general(q, k, (((1,), (1,)), ((), ())),
                            preferred_element_type=F32) * scale
        s = jnp.where(mask, s, -jnp.inf)
        m = jnp.max(s, axis=-1, keepdims=True)
        e = jnp.exp(s - m)
        l = jnp.sum(e, axis=-1, keepdims=True)
        o = jnp.dot(e.astype(BF16), v, preferred_element_type=F32) / l
        dest = pl.ds(p + q0 * dil, BLOCK, stride=dil)
        o_ref[dest, :] = o
        lse_ref[dest, :] = jnp.broadcast_to(m + jnp.log(l), (BLOCK, LANES))

    def one_class(p, c):
        rows = pl.ds(p, length, stride=dil)
        cos = cos_ref[rows, :]
        sin = sin_ref[rows, :]
        qr_ref[...] = _rope(q_ref[rows, :], cos, sin, HEAD_DIM).astype(BF16)
        kr_ref[BLOCK:, :] = _rope(k_ref[rows, :], cos, sin, HEAD_DIM).astype(BF16)
        vr_ref[BLOCK:, :] = v_ref[rows, :].astype(BF16)
        block(p, 0, band0)
        for a in range(1, group):
            block(p, a, band)

        def trip(t, c2):
            for a in range(group):
                block(p, group * t + a, band)
            return c2

        lax.fori_loop(1, nblocks // group, trip, 0)
        return c

    group = min(4, nblocks)
    assert nblocks % group == 0
    lax.fori_loop(0, dil, one_class, 0)


def _band_attention(qkv, cos, sin, group, hpg, n_heads, window, dil):
    b, s, n = qkv.shape
    length = s // dil
    assert length % BLOCK == 0 and window % dil == 0 and window // dil <= BLOCK
    h0 = group * hpg

    def col(which):
        return pl.BlockSpec((None, s, LANES), lambda bi, h: (bi, 0, which * n_heads + h0 + h))

    tab = pl.BlockSpec((None, s, LANES), lambda bi, h: (bi, 0, 0))
    out = pl.BlockSpec((None, s, LANES), lambda bi, h: (bi, 0, h))
    shape = jax.ShapeDtypeStruct((b, s, hpg * LANES), F32)
    kern = functools.partial(_band_kernel, dil=dil, length=length, max_dist=window // dil,
                             scale=HEAD_DIM ** -0.5)
    o, lse = pl.pallas_call(
        kern,
        out_shape=(shape, shape),
        grid=(b, hpg),
        in_specs=[col(0), col(1), col(2), tab, tab],
        out_specs=(out, out),
        scratch_shapes=[pltpu.VMEM((length, LANES), BF16),
                        pltpu.VMEM((length + BLOCK, LANES), BF16),
                        pltpu.VMEM((length + BLOCK, LANES), BF16)],
        compiler_params=_params("parallel", "parallel"),
        name=f"band_attention_d{dil}",
    )(qkv, qkv, qkv, cos, sin)
    return (o.reshape(b * s, hpg * LANES), lse.reshape(b * s, hpg * LANES))


def _dil_out_kernel(*refs, groups):
    o_refs = refs[:groups]
    l_refs = refs[groups:2 * groups]
    w_ref, x_ref, out_ref, a_ref = refs[2 * groups:]

    @pl.when(pl.program_id(1) == 0)
    def _():
        lses = [r[...] for r in l_refs]
        mx = functools.reduce(jnp.maximum, lses)
        es = [jnp.exp(l - mx) for l in lses]
        inv = 1.0 / functools.reduce(lambda u, v: u + v, es)
        w = o_refs[0].shape[1]
        for g in range(groups):
            a_ref[:, g * w:(g + 1) * w] = (o_refs[g][...] * (es[g] * inv)).astype(BF16)

    out_ref[...] = x_ref[...] + jnp.dot(a_ref[...], w_ref[...], preferred_element_type=F32)


def _dil_out(outs, lses, w_o, layer, x, tm_cap=512, tn_cap=1024):
    m, d = x.shape
    groups = len(outs)
    gw = outs[0].shape[1]
    k = groups * gw
    tm = _tile(m, tm_cap, 8)
    tn = _tile(d, tn_cap)
    part = pl.BlockSpec((tm, gw), lambda i, j: (i, 0))
    return pl.pallas_call(
        functools.partial(_dil_out_kernel, groups=groups),
        out_shape=jax.ShapeDtypeStruct((m, d), F32),
        grid=(m // tm, d // tn),
        in_specs=[part] * (2 * groups) + [
            pl.BlockSpec((None, k, tn), lambda i, j: (layer, 0, j)),
            pl.BlockSpec((tm, tn), lambda i, j: (i, j)),
        ],
        out_specs=pl.BlockSpec((tm, tn), lambda i, j: (i, j)),
        scratch_shapes=[pltpu.VMEM((tm, k), BF16)],
        compiler_params=_params("parallel", "arbitrary"),
        name="dil_out",
    )(*outs, *lses, w_o, x)


def _swa_kernel(sink_ref, q_ref, kc_ref, kp_ref, vc_ref, vp_ref, cc_ref, cp_ref, sc_ref, sp_ref,
                o_ref, qr_ref, kr_ref, vr_ref, *, tq, scale):
    pair = pl.program_id(1)
    has_prev = pl.program_id(2) > 0
    d = SWA_HEAD_DIM
    cos = cc_ref[...]
    sin = sc_ref[...]
    kx = jnp.concatenate([_rope(kp_ref[...].astype(F32), cp_ref[...], sp_ref[...], d),
                          _rope(kc_ref[...].astype(F32), cos, sin, d)], axis=0)
    vx = jnp.concatenate([vp_ref[...], vc_ref[...]], axis=0).astype(F32)
    low = lax.broadcasted_iota(jnp.int32, kx.shape, 1) < d
    for src, dst in ((kx, kr_ref), (vx, vr_ref)):
        swapped = pltpu.roll(src, d, 1)
        dst[0] = jnp.where(low, src, 0.0).astype(BF16)
        dst[1] = jnp.where(low, 0.0, swapped).astype(BF16)
        dst[2] = jnp.where(low, swapped, 0.0).astype(BF16)
        dst[3] = jnp.where(low, 0.0, src).astype(BF16)
    nblk = q_ref.shape[1] // LANES
    for c in range(nblk):
        sl = slice(c * LANES, (c + 1) * LANES)
        qr_ref[:, sl] = _rope(q_ref[:, sl].astype(F32), cos, sin, d).astype(BF16)

    qi = lax.broadcasted_iota(jnp.int32, (BLOCK, 2 * BLOCK), 0)
    kj = lax.broadcasted_iota(jnp.int32, (BLOCK, 2 * BLOCK), 1)
    dist = BLOCK + qi - kj
    band = (dist >= 0) & (dist <= SWA_WINDOW - 1)
    band0 = band & ((kj >= BLOCK) | has_prev)
    per_kv = nblk // (LANES // d)
    for r in range(tq // BLOCK):
        rows = slice(r * BLOCK, (r + 1) * BLOCK)
        mask = jnp.concatenate([band0 if r == 0 else band] * per_kv, axis=0)
        for kv in range(LANES // d):
            blocks = range(kv * per_kv, (kv + 1) * per_kv)
            q = jnp.concatenate([qr_ref[rows, c * LANES:(c + 1) * LANES] for c in blocks], axis=0)
            out = None
            for half in range(2):
                k = kr_ref[2 * kv + half, r * BLOCK:(r + 2) * BLOCK, :]
                v = vr_ref[2 * kv + half, r * BLOCK:(r + 2) * BLOCK, :]
                sink = jnp.concatenate(
                    [jnp.full((BLOCK, 1), sink_ref[(pair * nblk + c) * 2 + half], F32)
                     for c in blocks], axis=0)
                s = lax.dot_general(q, k, (((1,), (1,)), ((), ())),
                                    preferred_element_type=F32) * scale
                s = jnp.where(mask, s, -jnp.inf)
                m = jnp.maximum(jnp.max(s, axis=-1, keepdims=True), sink)
                p = jnp.exp(s - m)
                l = jnp.sum(p, axis=-1, keepdims=True) + jnp.exp(sink - m)
                o = jnp.dot(p.astype(BF16), v, preferred_element_type=F32) / l
                out = o if out is None else out + o
            for n, c in enumerate(blocks):
                o_ref[rows, c * LANES:(c + 1) * LANES] = (
                    out[n * BLOCK:(n + 1) * BLOCK].astype(o_ref.dtype))


def _swa_attention(qkv, cos, sin, sinks, q_heads, tq_cap=512):
    b, s, _ = qkv.shape
    d = SWA_HEAD_DIM
    kv_heads = q_heads // SWA_GROUP
    per_blk = LANES // d
    assert kv_heads % per_blk == 0
    nq = q_heads * d
    qw = per_blk * SWA_GROUP * d
    kcol = nq // LANES
    vcol = (nq + kv_heads * d) // LANES
    tq = _tile(s, tq_cap)
    rb = tq // BLOCK

    def cur(col0):
        return pl.BlockSpec((None, tq, LANES), lambda bi, p, i: (bi, i, col0 + p))

    def prev(col0):
        return pl.BlockSpec((None, BLOCK, LANES),
                            lambda bi, p, i: (bi, jnp.maximum(i * rb - 1, 0), col0 + p))

    tab_c = pl.BlockSpec((None, tq, LANES), lambda bi, p, i: (bi, i, 0))
    tab_p = pl.BlockSpec((None, BLOCK, LANES), lambda bi, p, i: (bi, jnp.maximum(i * rb - 1, 0), 0))
    kern = functools.partial(_swa_kernel, tq=tq, scale=d ** -0.5)
    return pl.pallas_call(
        kern,
        out_shape=jax.ShapeDtypeStruct((b, s, nq), BF16),
        grid=(b, kv_heads // per_blk, s // tq),
        in_specs=[
            pl.BlockSpec(memory_space=pltpu.SMEM),
            pl.BlockSpec((None, tq, qw), lambda bi, p, i: (bi, i, p)),
            cur(kcol), prev(kcol), cur(vcol), prev(vcol),
            tab_c, tab_p, tab_c, tab_p,
        ],
        out_specs=pl.BlockSpec((None, tq, qw), lambda bi, p, i: (bi, i, p)),
        scratch_shapes=[pltpu.VMEM((tq, qw), BF16),
                        pltpu.VMEM((2 * per_blk, tq + BLOCK, LANES), BF16),
                        pltpu.VMEM((2 * per_blk, tq + BLOCK, LANES), BF16)],
        compiler_params=_params("parallel", "parallel", "parallel"),
        name="swa_attention",
    )(sinks, qkv, qkv, qkv, qkv, qkv, cos, cos, sin, sin)


def _xattn_kernel(x_ref, g_ref, wq_ref, kv_ref, wo_ref, o_ref, a_ref, *, scale):
    x = x_ref[...]
    q = jnp.dot(_rms(x, g_ref[...]).astype(BF16), wq_ref[...],
                preferred_element_type=F32).astype(BF16)
    width = XA_HEADS * XA_HEAD_DIM
    for h in range(XA_HEADS):
        sl = slice(h * XA_HEAD_DIM, (h + 1) * XA_HEAD_DIM)
        k = kv_ref[:, sl]
        v = kv_ref[:, width + h * XA_HEAD_DIM:width + (h + 1) * XA_HEAD_DIM]
        s = lax.dot_general(q[:, sl], k, (((1,), (1,)), ((), ())),
                            preferred_element_type=F32) * scale
        m = jnp.max(s, axis=-1, keepdims=True)
        p = jnp.exp(s - m)
        l = jnp.sum(p, axis=-1, keepdims=True)
        a_ref[:, sl] = (jnp.dot(p.astype(BF16), v, preferred_element_type=F32) / l).astype(BF16)
    o_ref[...] = x + jnp.dot(a_ref[...], wo_ref[...], preferred_element_type=F32)


def _xattn(x, g, w_q, kv, w_o, layer, seq, tm_cap=512):
    m, d = x.shape
    n_mem = kv.shape[1]
    width = XA_HEADS * XA_HEAD_DIM
    tm = _tile(seq, tm_cap, 8)
    per_batch = seq // tm
    return pl.pallas_call(
        functools.partial(_xattn_kernel, scale=XA_HEAD_DIM ** -0.5),
        out_shape=jax.ShapeDtypeStruct((m, d), F32),
        grid=(m // tm,),
        in_specs=[
            pl.BlockSpec((tm, d), lambda i: (i, 0)),
            pl.BlockSpec((1, d), lambda i: (0, 0)),
            pl.BlockSpec((None, d, width), lambda i: (layer, 0, 0)),
            pl.BlockSpec((None, n_mem, 2 * width), lambda i: (i // per_batch, 0, 0)),
            pl.BlockSpec((None, width, d), lambda i: (layer, 0, 0)),
        ],
        out_specs=pl.BlockSpec((tm, d), lambda i: (i, 0)),
        scratch_shapes=[pltpu.VMEM((tm, width), BF16)],
        compiler_params=_params("parallel"),
        name="xattn",
    )(x, g.reshape(1, d), w_q, kv, w_o)


def kernel(x, mem, positions, ffn1_norm, ffn1_w_gate_up, ffn1_w_down, mix_norm, sb_w_qkv, sb_w_o, dil_w_qkv, dil_w_o, swa_w_qkv, swa_b_qkv, swa_sinks, swa_w_o, swa_b_o, xattn_norm, mem_norm, xattn_w_q, xattn_w_kv, xattn_w_o, ffn2_norm, ffn2_w_gate_up, ffn2_w_down, final_norm):
    b, s, d = x.shape
    m = b * s
    depth = ffn1_norm.shape[0]
    n_mem = mem.shape[1]
    sb_heads = d // HEAD_DIM
    dil_hpg = d // (4 * HEAD_DIM)
    dil_heads = dil_hpg * len(DIL_PATTERNS)
    swa_q_heads = d // SWA_HEAD_DIM

    xs = x.reshape(m, d)
    mem2 = mem.reshape(b * n_mem, d)
    pos = positions.reshape(m, 1)
    if depth > 1:
        dil_cos, dil_sin = (t.reshape(b, s, LANES) for t in _rope_tables(pos, HEAD_DIM))
    if depth > 2:
        swa_cos, swa_sin = (t.reshape(b, s, LANES) for t in _rope_tables(pos, SWA_HEAD_DIM))

    (ffn1_w_gate_up, ffn1_w_down, ffn2_w_gate_up, ffn2_w_down, sb_w_qkv, sb_w_o, dil_w_qkv, dil_w_o,
     swa_w_qkv, swa_w_o, xattn_w_q, xattn_w_kv, xattn_w_o) = (
        w.astype(BF16) for w in
        (ffn1_w_gate_up, ffn1_w_down, ffn2_w_gate_up, ffn2_w_down, sb_w_qkv, sb_w_o, dil_w_qkv,
         dil_w_o, swa_w_qkv, swa_w_o, xattn_w_q, xattn_w_kv, xattn_w_o))

    for i in range(depth):
        xs = _ffn(xs, ffn1_norm[i], ffn1_w_gate_up, ffn1_w_down, i)

        kind, j = i % 3, i // 3
        if kind == 0:
            qkv = _norm_matmul(xs, mix_norm[i], sb_w_qkv, j, None, BF16)
            o = _sb_attention(qkv.reshape(b, s, -1), sb_heads)
            xs = _proj_residual(o.reshape(m, -1), sb_w_o, j, xs, None)
        elif kind == 1:
            qkv = _norm_matmul(xs, mix_norm[i], dil_w_qkv, j, None, F32)
            qkv = qkv.reshape(b, s, -1)
            outs, lses = [], []
            for g, (window, dil) in enumerate(DIL_PATTERNS):
                o, lse = _band_attention(qkv, dil_cos, dil_sin, g, dil_hpg, dil_heads, window, dil)
                outs.append(o)
                lses.append(lse)
            xs = _dil_out(outs, lses, dil_w_o, j, xs)
        else:
            qkv = _norm_matmul(xs, mix_norm[i], swa_w_qkv, j, swa_b_qkv[j], BF16)
            o = _swa_attention(qkv.reshape(b, s, -1), swa_cos, swa_sin, swa_sinks[j], swa_q_heads)
            xs = _proj_residual(o.reshape(m, -1), swa_w_o, j, xs, swa_b_o[j])

        kv = _norm_matmul(mem2, mem_norm[i], xattn_w_kv, i, None, BF16)
        xs = _xattn(xs, xattn_norm[i], xattn_w_q, kv.reshape(b, n_mem, -1), xattn_w_o, i, s)

        xs = _ffn(xs, ffn2_norm[i], ffn2_w_gate_up, ffn2_w_down, i,
                  final_g=final_norm if i == depth - 1 else None)

    return xs.reshape(b, s, d)
```

```python
import functools

import jax
import jax.numpy as jnp
from jax import lax
from jax.experimental import pallas as pl
from jax.experimental.pallas import tpu as pltpu

F32 = jnp.float32
BF16 = jnp.bfloat16

NORM_EPS = 1e-6
ROPE_THETA = 500000.0
ROPE_FRACTION = 4
LOG2E = 1.4426950408889634
HEAD_DIM = 128
BLOCK = 128
DIL_PATTERNS = ((128, 1), (512, 4), (2048, 16))
SWA_HEAD_DIM = 64
SWA_GROUP = 8
SWA_WINDOW = 128
XA_HEADS = 4
XA_HEAD_DIM = 128

LANES = 128
V7X_VMEM_BYTES = 64 * 2**20
VMEM_LIMIT_BYTES = (V7X_VMEM_BYTES * 3) // 4
FFN_VMEM_LIMIT_BYTES = (V7X_VMEM_BYTES * 15) // 16


def _params(*semantics, vmem_limit=VMEM_LIMIT_BYTES):
    return pltpu.CompilerParams(dimension_semantics=semantics, vmem_limit_bytes=vmem_limit)


def _tile(n, cap, mult=LANES):
    if n <= cap:
        return n
    t = (cap // mult) * mult
    while t > mult and n % t:
        t -= mult
    assert n % t == 0, (n, cap, mult)
    return t


def _rms(x, g):
    ms = jnp.mean(x * x, axis=-1, keepdims=True)
    return x * lax.rsqrt(ms + NORM_EPS) * g


def _norm_matmul_kernel(x_ref, g_ref, w_ref, b_ref, s_ref, o_ref, xn_ref):
    @pl.when(pl.program_id(1) == 0)
    def _():
        xn_ref[...] = _rms(x_ref[...], g_ref[...]).astype(BF16)

    acc = jnp.dot(xn_ref[...], w_ref[...], preferred_element_type=F32)
    o_ref[...] = ((acc + b_ref[...]) * s_ref[...]).astype(o_ref.dtype)


def _norm_matmul(x, g, w, layer, bias, out_dtype, col_scale=None, tm_cap=1024, tn_cap=1024):
    m, d = x.shape
    n = w.shape[2]
    tm = _tile(m, tm_cap, 8)
    tn = _tile(n, tn_cap)
    if bias is None:
        bias = jnp.zeros((n,), F32)
    if col_scale is None:
        col_scale = jnp.ones((n,), F32)
    return pl.pallas_call(
        _norm_matmul_kernel,
        out_shape=jax.ShapeDtypeStruct((m, n), out_dtype),
        grid=(m // tm, n // tn),
        in_specs=[
            pl.BlockSpec((tm, d), lambda i, j: (i, 0)),
            pl.BlockSpec((1, d), lambda i, j: (0, 0)),
            pl.BlockSpec((None, d, tn), lambda i, j: (layer, 0, j)),
            pl.BlockSpec((1, tn), lambda i, j: (0, j)),
            pl.BlockSpec((1, tn), lambda i, j: (0, j)),
        ],
        out_specs=pl.BlockSpec((tm, tn), lambda i, j: (i, j)),
        scratch_shapes=[pltpu.VMEM((tm, d), BF16)],
        compiler_params=_params("parallel", "arbitrary"),
        name="norm_matmul",
    )(x, g.reshape(1, d), w, bias.reshape(1, n), col_scale.reshape(1, n))


def _ffn_kernel(x_ref, g_ref, wg_ref, wu_ref, wd_ref, gf_ref, o_ref, xn_ref, *, final_norm):
    j = pl.program_id(1)

    @pl.when(j == 0)
    def _():
        xn_ref[...] = _rms(x_ref[...], g_ref[...]).astype(BF16)
        o_ref[...] = jnp.zeros_like(o_ref)

    xn = xn_ref[...]
    gate = jnp.dot(xn, wg_ref[...], preferred_element_type=F32)
    up = jnp.dot(xn, wu_ref[...], preferred_element_type=F32)
    h = gate * jax.nn.sigmoid(gate) * up
    o_ref[...] += jnp.dot(h.astype(BF16), wd_ref[...], preferred_element_type=F32)

    @pl.when(j == pl.num_programs(1) - 1)
    def _():
        rows = min(256, o_ref.shape[0])

        def finish(c, carry):
            sl = pl.ds(pl.multiple_of(c * rows, rows), rows)
            y = x_ref[sl, :] + 0.5 * o_ref[sl, :]
            if final_norm:
                y = _rms(y, gf_ref[...])
            o_ref[sl, :] = y
            return carry

        lax.fori_loop(0, o_ref.shape[0] // rows, finish, 0)


def _ffn(x, g, w_gu, w_d, layer, final_g=None, tm_cap=1024, tf_cap=512):
    m, d = x.shape
    f = w_d.shape[1]
    tm = _tile(m, tm_cap, 8)
    tf = _tile(f, tf_cap)
    nf = f // tf
    gf = jnp.ones((d,), F32) if final_g is None else final_g
    return pl.pallas_call(
        functools.partial(_ffn_kernel, final_norm=final_g is not None),
        out_shape=jax.ShapeDtypeStruct((m, d), F32),
        grid=(m // tm, nf),
        in_specs=[
            pl.BlockSpec((tm, d), lambda i, j: (i, 0)),
            pl.BlockSpec((1, d), lambda i, j: (0, 0)),
            pl.BlockSpec((None, d, tf), lambda i, j: (layer, 0, j)),
            pl.BlockSpec((None, d, tf), lambda i, j: (layer, 0, j + nf)),
            pl.BlockSpec((None, tf, d), lambda i, j: (layer, j, 0)),
            pl.BlockSpec((1, d), lambda i, j: (0, 0)),
        ],
        out_specs=pl.BlockSpec((tm, d), lambda i, j: (i, 0)),
        scratch_shapes=[pltpu.VMEM((tm, d), BF16)],
        compiler_params=_params("parallel", "arbitrary", vmem_limit=FFN_VMEM_LIMIT_BYTES),
        name="ffn",
    )(x, g.reshape(1, d), w_gu, w_gu, w_d, gf.reshape(1, d))


def _proj_residual_kernel(a_ref, w_ref, x_ref, b_ref, o_ref):
    acc = jnp.dot(a_ref[...], w_ref[...], preferred_element_type=F32)
    o_ref[...] = x_ref[...] + acc + b_ref[...]


def _proj_residual(a, w, layer, x, bias, tm_cap=1024, tn_cap=1024):
    m, k = a.shape
    d = w.shape[2]
    tm = _tile(m, tm_cap, 8)
    tn = _tile(d, tn_cap)
    if bias is None:
        bias = jnp.zeros((d,), F32)
    return pl.pallas_call(
        _proj_residual_kernel,
        out_shape=jax.ShapeDtypeStruct((m, d), F32),
        grid=(m // tm, d // tn),
        in_specs=[
            pl.BlockSpec((tm, k), lambda i, j: (i, 0)),
            pl.BlockSpec((None, k, tn), lambda i, j: (layer, 0, j)),
            pl.BlockSpec((tm, tn), lambda i, j: (i, j)),
            pl.BlockSpec((1, tn), lambda i, j: (0, j)),
        ],
        out_specs=pl.BlockSpec((tm, tn), lambda i, j: (i, j)),
        compiler_params=_params("parallel", "arbitrary"),
        name="proj_residual",
    )(a, w, x, bias.reshape(1, d))


def _sb_kernel(q_ref, k_ref, v_ref, tri_ref, o_ref, acc_ref, car_ref, z_ref, *, tq, tk):
    row0 = pl.program_id(2) * tq
    acc_ref[...] = jnp.zeros_like(acc_ref)
    tri = tri_ref[...]
    nsub = tq // tk
    nheads = q_ref.shape[1] // HEAD_DIM
    assert nheads % 2 == 0
    cols = [slice(h * HEAD_DIM, (h + 1) * HEAD_DIM) for h in range(nheads)]

    def logits(h, ks):
        return lax.dot_general(q_ref[:, cols[h]], k_ref[pl.ds(ks, tq), cols[h]],
                               (((1,), (1,)), ((), ())), preferred_element_type=F32)

    def neg_log2_keep(z2):
        neg_abs = lax.bitcast_convert_type(
            lax.bitcast_convert_type(z2, jnp.uint32) | jnp.uint32(0x80000000), F32)
        return jnp.maximum(z2, 0.0) + jnp.log2(1.0 + jnp.exp2(neg_abs))

    def prefetch(h, ks, ks_next):
        if h + 1 < nheads:
            z_ref[(h + 1) % 2] = logits(h + 1, ks)
        else:
            z_ref[0] = logits(0, ks_next)

    def full_tile(ks, ks_next):
        for h, hs in enumerate(cols):
            prefetch(h, ks, ks_next)
            zr = z_ref.at[h % 2]
            nlb = neg_log2_keep(zr[...]).astype(BF16)
            car = car_ref[:, hs]
            parts = [None] * nsub
            for j in reversed(range(nsub)):
                sl = slice(j * tk, (j + 1) * tk)
                ssum = jnp.dot(nlb[:, sl], tri, preferred_element_type=F32)
                parts[j] = jnp.exp2(zr[:, sl] - ssum
                                    - jnp.concatenate([car] * (tk // LANES), axis=1))
                car = car + jnp.broadcast_to(ssum[:, 0:1], (tq, LANES))
            a = jnp.concatenate(parts, axis=1)
            acc_ref[:, hs] += jnp.dot(a.astype(BF16), v_ref[pl.ds(ks, tq), hs],
                                      preferred_element_type=F32)
            car_ref[:, hs] = car

    def diagonal_tile(ks, ks_next):
        lower = (lax.broadcasted_iota(jnp.int32, (tk, tk), 1)
                 < lax.broadcasted_iota(jnp.int32, (tk, tk), 0))
        for h, hs in enumerate(cols):
            prefetch(h, ks, ks_next)
            zr = z_ref.at[h % 2]
            car = jnp.zeros((tq, LANES), F32)
            for j in reversed(range(nsub)):
                r0 = j * tk
                nr = tq - r0
                sl = slice(j * tk, (j + 1) * tk)
                mask = lower if nr == tk else jnp.concatenate(
                    [lower, jnp.ones((nr - tk, tk), jnp.bool_)], axis=0)
                z2 = zr[r0:, sl]
                nl = jnp.where(mask, neg_log2_keep(z2), 0.0)
                ssum = jnp.dot(nl.astype(BF16), tri, preferred_element_type=F32)
                a = jnp.exp2(z2 - ssum - jnp.concatenate([car[r0:]] * (tk // LANES), axis=1))
                a = jnp.where(mask, a, 0.0)
                acc_ref[r0:, hs] += jnp.dot(a.astype(BF16), v_ref[pl.ds(ks + r0, tk), hs],
                                            preferred_element_type=F32)
                upd = car[r0:] + jnp.broadcast_to(ssum[:, 0:1], (nr, LANES))
                car = upd if r0 == 0 else jnp.concatenate([car[:r0], upd], axis=0)
            car_ref[:, hs] = car

    def start(t):
        return pl.multiple_of(jnp.maximum(row0 - (t + 1) * tq, 0), tq)

    z_ref[0] = logits(0, pl.multiple_of(row0, tq))
    diagonal_tile(pl.multiple_of(row0, tq), start(0))

    def body(t, c):
        full_tile(start(t), start(t + 1))
        return c

    lax.fori_loop(0, pl.program_id(2), body, 0)
    o_ref[...] = acc_ref[...].astype(o_ref.dtype)


def _sb_q_scale(heads):
    width = heads * HEAD_DIM
    return jnp.concatenate([jnp.full((width,), HEAD_DIM ** -0.5 * LOG2E, F32),
                            jnp.ones((2 * width,), F32)])


def _sb_attention(qkv, heads, tq_cap=512, tk_cap=256, heads_per_step=4):
    b, s, _ = qkv.shape
    tq = _tile(s, tq_cap)
    tk = _tile(tq, tk_cap)
    hp = heads_per_step if heads % heads_per_step == 0 else 1
    hw = hp * HEAD_DIM
    hb = heads // hp
    tri = jnp.tril(jnp.ones((tk, tk), F32)).astype(BF16)
    kern = functools.partial(_sb_kernel, tq=tq, tk=tk)
    return pl.pallas_call(
        kern,
        out_shape=jax.ShapeDtypeStruct((b, s, heads * HEAD_DIM), BF16),
        grid=(b, hb, s // tq),
        in_specs=[
            pl.BlockSpec((None, tq, hw), lambda bi, h, i: (bi, i, h)),
            pl.BlockSpec((None, s, hw), lambda bi, h, i: (bi, 0, hb + h)),
            pl.BlockSpec((None, s, hw), lambda bi, h, i: (bi, 0, 2 * hb + h)),
            pl.BlockSpec((tk, tk), lambda bi, h, i: (0, 0)),
        ],
        out_specs=pl.BlockSpec((None, tq, hw), lambda bi, h, i: (bi, i, h)),
        scratch_shapes=[pltpu.VMEM((tq, hw), F32), pltpu.VMEM((tq, hw), F32),
                        pltpu.VMEM((2, tq, tq), F32)],
        compiler_params=_params("parallel", "parallel", "arbitrary"),
        name="sb_attention",
    )(qkv, qkv, qkv, tri)


def _rope_table_kernel(pos_ref, freq_ref, sign_ref, cos_ref, sin_ref):
    ang = pos_ref[...].astype(F32) * freq_ref[...]
    cos_ref[...] = jnp.cos(ang)
    sin_ref[...] = jnp.sin(ang) * sign_ref[...]


def _rope_tables(positions, head_dim):
    m = positions.shape[0]
    rot = head_dim // ROPE_FRACTION
    half = rot // 2
    inv_freq = jnp.power(F32(ROPE_THETA), -jnp.arange(half, dtype=F32) * 2.0 / rot)
    lane = jnp.arange(LANES) % head_dim
    freq = jnp.where(lane < rot, inv_freq[lane % half], 0.0).astype(F32)
    sign = jnp.where(lane < half, -1.0, jnp.where(lane < rot, 1.0, 0.0)).astype(F32)
    tm = _tile(m, 2048, 8)
    out = jax.ShapeDtypeStruct((m, LANES), F32)
    return pl.pallas_call(
        _rope_table_kernel,
        out_shape=(out, out),
        grid=(m // tm,),
        in_specs=[
            pl.BlockSpec((tm, 1), lambda i: (i, 0)),
            pl.BlockSpec((1, LANES), lambda i: (0, 0)),
            pl.BlockSpec((1, LANES), lambda i: (0, 0)),
        ],
        out_specs=(pl.BlockSpec((tm, LANES), lambda i: (i, 0)),
                   pl.BlockSpec((tm, LANES), lambda i: (i, 0))),
        compiler_params=_params("parallel"),
        name="rope_tables",
    )(positions, freq.reshape(1, LANES), sign.reshape(1, LANES))


def _rope(x, cos, sin, head_dim):
    half = head_dim // ROPE_FRACTION // 2
    lane = lax.broadcasted_iota(jnp.int32, x.shape, 1) & (head_dim - 1)
    ahead = pltpu.roll(x, LANES - half, 1)
    behind = pltpu.roll(x, half, 1)
    return x * cos + jnp.where(lane < half, ahead, behind) * sin


def _band_kernel(q_ref, k_ref, v_ref, cos_ref, sin_ref, o_ref, lse_ref, qr_ref, kr_ref, vr_ref,
                 *, dil, length, max_dist, scale):
    kj = lax.broadcasted_iota(jnp.int32, (BLOCK, 2 * BLOCK), 1)
    dist = BLOCK + lax.broadcasted_iota(jnp.int32, (BLOCK, 2 * BLOCK), 0) - kj
    band = (dist >= 0) & (dist <= max_dist)
    band0 = band & (kj >= BLOCK)
    nblocks = length // BLOCK
    kr_ref[:, 0:BLOCK, :] = jnp.zeros((kr_ref.shape[0], BLOCK, LANES), BF16)
    vr_ref[:, 0:BLOCK, :] = jnp.zeros((vr_ref.shape[0], BLOCK, LANES), BF16)

    def block(p, slot, a, mask):
        q0 = pl.multiple_of(a * BLOCK, BLOCK)
        q = qr_ref[slot, pl.ds(q0, BLOCK), :]
        k = kr_ref[slot, pl.ds(q0, 2 * BLOCK), :]
        v = vr_ref[slot, pl.ds(q0, 2 * BLOCK), :]
        s = lax.dot_general(q, k, (((1,), (1,)), ((), ())),
                            preferred_element_type=F32) * scale
        s = jnp.where(mask, s, -jnp.inf)
        m = jnp.max(s, axis=-1, keepdims=True)
        e = jnp.exp(s - m)
        l = jnp.sum(e, axis=-1, keepdims=True)
        o = jnp.dot(e.astype(BF16), v, preferred_element_type=F32) / l
        dest = pl.ds(p + q0 * dil, BLOCK, stride=dil)
        o_ref[dest, :] = o
        lse_ref[dest, :] = jnp.broadcast_to(m + jnp.log(l), (BLOCK, LANES))

    ncls = qr_ref.shape[0]
    group = min(4, nblocks)
    assert nblocks % group == 0 and dil % ncls == 0

    def classes(t, c):
        ps = [t * ncls + slot for slot in range(ncls)]
        for slot, p in enumerate(ps):
            rows = pl.ds(p, length, stride=dil)
            cos = cos_ref[rows, :]
            sin = sin_ref[rows, :]
            qr_ref[slot] = _rope(q_ref[rows, :], cos, sin, HEAD_DIM).astype(BF16)
            kr_ref[slot, BLOCK:, :] = _rope(k_ref[rows, :], cos, sin, HEAD_DIM).astype(BF16)
            vr_ref[slot, BLOCK:, :] = v_ref[rows, :].astype(BF16)
        for slot, p in enumerate(ps):
            block(p, slot, 0, band0)
            for a in range(1, group):
                block(p, slot, a, band)

        def trip(u, c2):
            for slot, p in enumerate(ps):
                for a in range(group):
                    block(p, slot, group * u + a, band)
            return c2

        lax.fori_loop(1, nblocks // group, trip, 0)
        return c

    lax.fori_loop(0, dil // ncls, classes, 0)


def _band_attention(qkv, cos, sin, group, hpg, n_heads, window, dil):
    b, s, n = qkv.shape
    length = s // dil
    assert length % BLOCK == 0 and window % dil == 0 and window // dil <= BLOCK
    h0 = group * hpg

    def col(which):
        return pl.BlockSpec((None, s, LANES), lambda bi, h: (bi, 0, which * n_heads + h0 + h))

    tab = pl.BlockSpec((None, s, LANES), lambda bi, h: (bi, 0, 0))
    out = pl.BlockSpec((None, s, LANES), lambda bi, h: (bi, 0, h))
    shape = jax.ShapeDtypeStruct((b, s, hpg * LANES), F32)
    ncls = min(dil, max(1, (4 * BLOCK) // length))
    kern = functools.partial(_band_kernel, dil=dil, length=length, max_dist=window // dil,
                             scale=HEAD_DIM ** -0.5)
    o, lse = pl.pallas_call(
        kern,
        out_shape=(shape, shape),
        grid=(b, hpg),
        in_specs=[col(0), col(1), col(2), tab, tab],
        out_specs=(out, out),
        scratch_shapes=[pltpu.VMEM((ncls, length, LANES), BF16),
                        pltpu.VMEM((ncls, length + BLOCK, LANES), BF16),
                        pltpu.VMEM((ncls, length + BLOCK, LANES), BF16)],
        compiler_params=_params("parallel", "parallel"),
        name=f"band_attention_d{dil}",
    )(qkv, qkv, qkv, cos, sin)
    return (o.reshape(b * s, hpg * LANES), lse.reshape(b * s, hpg * LANES))


def _dil_out_kernel(*refs, groups):
    o_refs = refs[:groups]
    l_refs = refs[groups:2 * groups]
    w_ref, x_ref, out_ref, a_ref = refs[2 * groups:]

    @pl.when(pl.program_id(1) == 0)
    def _():
        lses = [r[...] for r in l_refs]
        mx = functools.reduce(jnp.maximum, lses)
        es = [jnp.exp(l - mx) for l in lses]
        inv = 1.0 / functools.reduce(lambda u, v: u + v, es)
        w = o_refs[0].shape[1]
        for g in range(groups):
            a_ref[:, g * w:(g + 1) * w] = (o_refs[g][...] * (es[g] * inv)).astype(BF16)

    out_ref[...] = x_ref[...] + jnp.dot(a_ref[...], w_ref[...], preferred_element_type=F32)


def _dil_out(outs, lses, w_o, layer, x, tm_cap=512, tn_cap=1024):
    m, d = x.shape
    groups = len(outs)
    gw = outs[0].shape[1]
    k = groups * gw
    tm = _tile(m, tm_cap, 8)
    tn = _tile(d, tn_cap)
    part = pl.BlockSpec((tm, gw), lambda i, j: (i, 0))
    return pl.pallas_call(
        functools.partial(_dil_out_kernel, groups=groups),
        out_shape=jax.ShapeDtypeStruct((m, d), F32),
        grid=(m // tm, d // tn),
        in_specs=[part] * (2 * groups) + [
            pl.BlockSpec((None, k, tn), lambda i, j: (layer, 0, j)),
            pl.BlockSpec((tm, tn), lambda i, j: (i, j)),
        ],
        out_specs=pl.BlockSpec((tm, tn), lambda i, j: (i, j)),
        scratch_shapes=[pltpu.VMEM((tm, k), BF16)],
        compiler_params=_params("parallel", "arbitrary"),
        name="dil_out",
    )(*outs, *lses, w_o, x)


def _swa_kernel(sink_ref, q_ref, kc_ref, kp_ref, vc_ref, vp_ref, cc_ref, cp_ref, sc_ref, sp_ref,
                o_ref, qr_ref, kr_ref, vr_ref, *, tq, scale):
    pair = pl.program_id(1)
    has_prev = pl.program_id(2) > 0
    d = SWA_HEAD_DIM
    cos = cc_ref[...]
    sin = sc_ref[...]
    kx = jnp.concatenate([_rope(kp_ref[...].astype(F32), cp_ref[...], sp_ref[...], d),
                          _rope(kc_ref[...].astype(F32), cos, sin, d)], axis=0)
    vx = jnp.concatenate([vp_ref[...], vc_ref[...]], axis=0).astype(F32)
    low = lax.broadcasted_iota(jnp.int32, kx.shape, 1) < d
    for src, dst in ((kx, kr_ref), (vx, vr_ref)):
        swapped = pltpu.roll(src, d, 1)
        dst[0] = jnp.where(low, src, 0.0).astype(BF16)
        dst[1] = jnp.where(low, 0.0, swapped).astype(BF16)
        dst[2] = jnp.where(low, swapped, 0.0).astype(BF16)
        dst[3] = jnp.where(low, 0.0, src).astype(BF16)
    nblk = q_ref.shape[1] // LANES
    for c in range(nblk):
        sl = slice(c * LANES, (c + 1) * LANES)
        qr_ref[:, sl] = _rope(q_ref[:, sl].astype(F32), cos, sin, d).astype(BF16)

    qi = lax.broadcasted_iota(jnp.int32, (BLOCK, 2 * BLOCK), 0)
    kj = lax.broadcasted_iota(jnp.int32, (BLOCK, 2 * BLOCK), 1)
    dist = BLOCK + qi - kj
    band = (dist >= 0) & (dist <= SWA_WINDOW - 1)
    band0 = band & ((kj >= BLOCK) | has_prev)
    per_kv = nblk // (LANES // d)
    for r in range(tq // BLOCK):
        rows = slice(r * BLOCK, (r + 1) * BLOCK)
        mask = jnp.concatenate([band0 if r == 0 else band] * per_kv, axis=0)
        for kv in range(LANES // d):
            blocks = range(kv * per_kv, (kv + 1) * per_kv)
            q = jnp.concatenate([qr_ref[rows, c * LANES:(c + 1) * LANES] for c in blocks], axis=0)
            out = None
            for half in range(2):
                k = kr_ref[2 * kv + half, r * BLOCK:(r + 2) * BLOCK, :]
                v = vr_ref[2 * kv + half, r * BLOCK:(r + 2) * BLOCK, :]
                sink = jnp.concatenate(
                    [jnp.full((BLOCK, 1), sink_ref[(pair * nblk + c) * 2 + half], F32)
                     for c in blocks], axis=0)
                s = lax.dot_general(q, k, (((1,), (1,)), ((), ())),
                                    preferred_element_type=F32) * scale
                s = jnp.where(mask, s, -jnp.inf)
                m = jnp.maximum(jnp.max(s, axis=-1, keepdims=True), sink)
                p = jnp.exp(s - m)
                l = jnp.sum(p, axis=-1, keepdims=True) + jnp.exp(sink - m)
                o = jnp.dot(p.astype(BF16), v, preferred_element_type=F32) / l
                out = o if out is None else out + o
            for n, c in enumerate(blocks):
                o_ref[rows, c * LANES:(c + 1) * LANES] = (
                    out[n * BLOCK:(n + 1) * BLOCK].astype(o_ref.dtype))


def _swa_attention(qkv, cos, sin, sinks, q_heads, tq_cap=512):
    b, s, _ = qkv.shape
    d = SWA_HEAD_DIM
    kv_heads = q_heads // SWA_GROUP
    per_blk = LANES // d
    assert kv_heads % per_blk == 0
    nq = q_heads * d
    qw = per_blk * SWA_GROUP * d
    kcol = nq // LANES
    vcol = (nq + kv_heads * d) // LANES
    tq = _tile(s, tq_cap)
    rb = tq // BLOCK

    def cur(col0):
        return pl.BlockSpec((None, tq, LANES), lambda bi, p, i: (bi, i, col0 + p))

    def prev(col0):
        return pl.BlockSpec((None, BLOCK, LANES),
                            lambda bi, p, i: (bi, jnp.maximum(i * rb - 1, 0), col0 + p))

    tab_c = pl.BlockSpec((None, tq, LANES), lambda bi, p, i: (bi, i, 0))
    tab_p = pl.BlockSpec((None, BLOCK, LANES), lambda bi, p, i: (bi, jnp.maximum(i * rb - 1, 0), 0))
    kern = functools.partial(_swa_kernel, tq=tq, scale=d ** -0.5)
    return pl.pallas_call(
        kern,
        out_shape=jax.ShapeDtypeStruct((b, s, nq), BF16),
        grid=(b, kv_heads // per_blk, s // tq),
        in_specs=[
            pl.BlockSpec(memory_space=pltpu.SMEM),
            pl.BlockSpec((None, tq, qw), lambda bi, p, i: (bi, i, p)),
            cur(kcol), prev(kcol), cur(vcol), prev(vcol),
            tab_c, tab_p, tab_c, tab_p,
        ],
        out_specs=pl.BlockSpec((None, tq, qw), lambda bi, p, i: (bi, i, p)),
        scratch_shapes=[pltpu.VMEM((tq, qw), BF16),
                        pltpu.VMEM((2 * per_blk, tq + BLOCK, LANES), BF16),
                        pltpu.VMEM((2 * per_blk, tq + BLOCK, LANES), BF16)],
        compiler_params=_params("parallel", "parallel", "parallel"),
        name="swa_attention",
    )(sinks, qkv, qkv, qkv, qkv, qkv, cos, cos, sin, sin)


def _xattn_kernel(x_ref, g_ref, wq_ref, kv_ref, wo_ref, o_ref, a_ref, *, scale):
    x = x_ref[...]
    q = jnp.dot(_rms(x, g_ref[...]).astype(BF16), wq_ref[...],
                preferred_element_type=F32).astype(BF16)
    width = XA_HEADS * XA_HEAD_DIM
    for h in range(XA_HEADS):
        sl = slice(h * XA_HEAD_DIM, (h + 1) * XA_HEAD_DIM)
        k = kv_ref[:, sl]
        v = kv_ref[:, width + h * XA_HEAD_DIM:width + (h + 1) * XA_HEAD_DIM]
        s = lax.dot_general(q[:, sl], k, (((1,), (1,)), ((), ())),
                            preferred_element_type=F32) * scale
        m = jnp.max(s, axis=-1, keepdims=True)
        p = jnp.exp(s - m)
        l = jnp.sum(p, axis=-1, keepdims=True)
        a_ref[:, sl] = (jnp.dot(p.astype(BF16), v, preferred_element_type=F32) / l).astype(BF16)
    o_ref[...] = x + jnp.dot(a_ref[...], wo_ref[...], preferred_element_type=F32)


def _xattn(x, g, w_q, kv, w_o, layer, seq, tm_cap=512):
    m, d = x.shape
    n_mem = kv.shape[1]
    width = XA_HEADS * XA_HEAD_DIM
    tm = _tile(seq, tm_cap, 8)
    per_batch = seq // tm
    return pl.pallas_call(
        functools.partial(_xattn_kernel, scale=XA_HEAD_DIM ** -0.5),
        out_shape=jax.ShapeDtypeStruct((m, d), F32),
        grid=(m // tm,),
        in_specs=[
            pl.BlockSpec((tm, d), lambda i: (i, 0)),
            pl.BlockSpec((1, d), lambda i: (0, 0)),
            pl.BlockSpec((None, d, width), lambda i: (layer, 0, 0)),
            pl.BlockSpec((None, n_mem, 2 * width), lambda i: (i // per_batch, 0, 0)),
            pl.BlockSpec((None, width, d), lambda i: (layer, 0, 0)),
        ],
        out_specs=pl.BlockSpec((tm, d), lambda i: (i, 0)),
        scratch_shapes=[pltpu.VMEM((tm, width), BF16)],
        compiler_params=_params("parallel"),
        name="xattn",
    )(x, g.reshape(1, d), w_q, kv, w_o)


def kernel(x, mem, positions, ffn1_norm, ffn1_w_gate_up, ffn1_w_down, mix_norm, sb_w_qkv, sb_w_o, dil_w_qkv, dil_w_o, swa_w_qkv, swa_b_qkv, swa_sinks, swa_w_o, swa_b_o, xattn_norm, mem_norm, xattn_w_q, xattn_w_kv, xattn_w_o, ffn2_norm, ffn2_w_gate_up, ffn2_w_down, final_norm):
    b, s, d = x.shape
    m = b * s
    depth = ffn1_norm.shape[0]
    n_mem = mem.shape[1]
    sb_heads = d // HEAD_DIM
    dil_hpg = d // (4 * HEAD_DIM)
    dil_heads = dil_hpg * len(DIL_PATTERNS)
    swa_q_heads = d // SWA_HEAD_DIM

    xs = x.reshape(m, d)
    mem2 = mem.reshape(b * n_mem, d)
    pos = positions.reshape(m, 1)
    if depth > 1:
        dil_cos, dil_sin = (t.reshape(b, s, LANES) for t in _rope_tables(pos, HEAD_DIM))
    if depth > 2:
        swa_cos, swa_sin = (t.reshape(b, s, LANES) for t in _rope_tables(pos, SWA_HEAD_DIM))

    (ffn1_w_gate_up, ffn1_w_down, ffn2_w_gate_up, ffn2_w_down, sb_w_qkv, sb_w_o, dil_w_qkv, dil_w_o,
     swa_w_qkv, swa_w_o, xattn_w_q, xattn_w_kv, xattn_w_o) = (
        w.astype(BF16) for w in
        (ffn1_w_gate_up, ffn1_w_down, ffn2_w_gate_up, ffn2_w_down, sb_w_qkv, sb_w_o, dil_w_qkv,
         dil_w_o, swa_w_qkv, swa_w_o, xattn_w_q, xattn_w_kv, xattn_w_o))

    for i in range(depth):
        xs = _ffn(xs, ffn1_norm[i], ffn1_w_gate_up, ffn1_w_down, i)

        kind, j = i % 3, i // 3
        if kind == 0:
            qkv = _norm_matmul(xs, mix_norm[i], sb_w_qkv, j, None, BF16,
                               col_scale=_sb_q_scale(sb_heads))
            o = _sb_attention(qkv.reshape(b, s, -1), sb_heads)
            xs = _proj_residual(o.reshape(m, -1), sb_w_o, j, xs, None)
        elif kind == 1:
            qkv = _norm_matmul(xs, mix_norm[i], dil_w_qkv, j, None, F32)
            qkv = qkv.reshape(b, s, -1)
            outs, lses = [], []
            for g, (window, dil) in enumerate(DIL_PATTERNS):
                o, lse = _band_attention(qkv, dil_cos, dil_sin, g, dil_hpg, dil_heads, window, dil)
                outs.append(o)
                lses.append(lse)
            xs = _dil_out(outs, lses, dil_w_o, j, xs)
        else:
            qkv = _norm_matmul(xs, mix_norm[i], swa_w_qkv, j, swa_b_qkv[j], BF16)
            o = _swa_attention(qkv.reshape(b, s, -1), swa_cos, swa_sin, swa_sinks[j], swa_q_heads)
            xs = _proj_residual(o.reshape(m, -1), swa_w_o, j, xs, swa_b_o[j])

        kv = _norm_matmul(mem2, mem_norm[i], xattn_w_kv, i, None, BF16)
        xs = _xattn(xs, xattn_norm[i], xattn_w_q, kv.reshape(b, n_mem, -1), xattn_w_o, i, s)

        xs = _ffn(xs, ffn2_norm[i], ffn2_w_gate_up, ffn2_w_down, i,
                  final_g=final_norm if i == depth - 1 else None)

    return xs.reshape(b, s, d)
```

```python
import functools

import jax
import jax.numpy as jnp
from jax import lax
from jax.experimental import pallas as pl
from jax.experimental.pallas import tpu as pltpu

F32 = jnp.float32
BF16 = jnp.bfloat16

NORM_EPS = 1e-6
ROPE_THETA = 500000.0
ROPE_FRACTION = 4
LOG2E = 1.4426950408889634
HEAD_DIM = 128
BLOCK = 128
DIL_PATTERNS = ((128, 1), (512, 4), (2048, 16))
SWA_HEAD_DIM = 64
SWA_GROUP = 8
SWA_WINDOW = 128
XA_HEADS = 4
XA_HEAD_DIM = 128

LANES = 128
V7X_VMEM_BYTES = 64 * 2**20
VMEM_LIMIT_BYTES = (V7X_VMEM_BYTES * 3) // 4
FFN_VMEM_LIMIT_BYTES = (V7X_VMEM_BYTES * 15) // 16


def _params(*semantics, vmem_limit=VMEM_LIMIT_BYTES):
    return pltpu.CompilerParams(dimension_semantics=semantics, vmem_limit_bytes=vmem_limit)


def _tile(n, cap, mult=LANES):
    if n <= cap:
        return n
    t = (cap // mult) * mult
    while t > mult and n % t:
        t -= mult
    assert n % t == 0, (n, cap, mult)
    return t


def _rms(x, g):
    ms = jnp.mean(x * x, axis=-1, keepdims=True)
    return x * lax.rsqrt(ms + NORM_EPS) * g


def _norm_matmul_kernel(x_ref, g_ref, w_ref, b_ref, s_ref, o_ref, xn_ref):
    @pl.when(pl.program_id(1) == 0)
    def _():
        xn_ref[...] = _rms(x_ref[...], g_ref[...]).astype(BF16)

    acc = jnp.dot(xn_ref[...], w_ref[...], preferred_element_type=F32)
    o_ref[...] = ((acc + b_ref[...]) * s_ref[...]).astype(o_ref.dtype)


def _norm_matmul(x, g, w, layer, bias, out_dtype, col_scale=None, tm_cap=1024, tn_cap=1024):
    m, d = x.shape
    n = w.shape[2]
    tm = _tile(m, tm_cap, 8)
    tn = _tile(n, tn_cap)
    if bias is None:
        bias = jnp.zeros((n,), F32)
    if col_scale is None:
        col_scale = jnp.ones((n,), F32)
    return pl.pallas_call(
        _norm_matmul_kernel,
        out_shape=jax.ShapeDtypeStruct((m, n), out_dtype),
        grid=(m // tm, n // tn),
        in_specs=[
            pl.BlockSpec((tm, d), lambda i, j: (i, 0)),
            pl.BlockSpec((1, d), lambda i, j: (0, 0)),
            pl.BlockSpec((None, d, tn), lambda i, j: (layer, 0, j)),
            pl.BlockSpec((1, tn), lambda i, j: (0, j)),
            pl.BlockSpec((1, tn), lambda i, j: (0, j)),
        ],
        out_specs=pl.BlockSpec((tm, tn), lambda i, j: (i, j)),
        scratch_shapes=[pltpu.VMEM((tm, d), BF16)],
        compiler_params=_params("parallel", "arbitrary"),
        name="norm_matmul",
    )(x, g.reshape(1, d), w, bias.reshape(1, n), col_scale.reshape(1, n))


def _ffn_kernel(x_ref, g_ref, wg_ref, wu_ref, wd_ref, gf_ref, *rest, final_norm, convert):
    if convert:
        ngu_ref, nd_ref, o_ref, ngu_out, nd_out, xn_ref = rest
        ngu_out[...] = ngu_ref[...].astype(BF16)
        nd_out[...] = nd_ref[...].astype(BF16)
    else:
        o_ref, xn_ref = rest
    j = pl.program_id(1)

    @pl.when(j == 0)
    def _():
        xn_ref[...] = _rms(x_ref[...], g_ref[...]).astype(BF16)
        o_ref[...] = jnp.zeros_like(o_ref)

    xn = xn_ref[...]
    gate = jnp.dot(xn, wg_ref[...], preferred_element_type=F32)
    up = jnp.dot(xn, wu_ref[...], preferred_element_type=F32)
    h = gate * jax.nn.sigmoid(gate) * up
    o_ref[...] += jnp.dot(h.astype(BF16), wd_ref[...], preferred_element_type=F32)

    @pl.when(j == pl.num_programs(1) - 1)
    def _():
        rows = min(256, o_ref.shape[0])

        def finish(c, carry):
            sl = pl.ds(pl.multiple_of(c * rows, rows), rows)
            y = x_ref[sl, :] + 0.5 * o_ref[sl, :]
            if final_norm:
                y = _rms(y, gf_ref[...])
            o_ref[sl, :] = y
            return carry

        lax.fori_loop(0, o_ref.shape[0] // rows, finish, 0)


def _ffn_tiles(m, f, tm_cap=1024, tf_cap=512):
    return _tile(m, tm_cap, 8), _tile(f, tf_cap)


def _ffn_can_convert(m, d, f):
    tm, tf = _ffn_tiles(m, f)
    steps_i = m // tm
    return d % (steps_i * 16) == 0 and tf % (steps_i * 16) == 0


def _ffn(x, g, w_gu, w_d, final_g=None, convert=None):
    m, d = x.shape
    f = w_d.shape[0]
    tm, tf = _ffn_tiles(m, f)
    nf = f // tf
    ni = m // tm
    gf = jnp.ones((d,), F32) if final_g is None else final_g
    in_specs = [
        pl.BlockSpec((tm, d), lambda i, j: (i, 0)),
        pl.BlockSpec((1, d), lambda i, j: (0, 0)),
        pl.BlockSpec((d, tf), lambda i, j: (0, j)),
        pl.BlockSpec((d, tf), lambda i, j: (0, j + nf)),
        pl.BlockSpec((tf, d), lambda i, j: (j, 0)),
        pl.BlockSpec((1, d), lambda i, j: (0, 0)),
    ]
    args = [x, g.reshape(1, d), w_gu, w_gu, w_d, gf.reshape(1, d)]
    out_specs = [pl.BlockSpec((tm, d), lambda i, j: (i, 0))]
    out_shape = [jax.ShapeDtypeStruct((m, d), F32)]
    if convert is not None:
        src_gu, src_d, layer = convert
        gu_blk = (d // ni, 2 * tf)
        d_blk = (tf // ni, d)
        in_specs += [pl.BlockSpec((None,) + gu_blk, lambda i, j: (layer, i, j)),
                     pl.BlockSpec((None,) + d_blk, lambda i, j: (layer, i * nf + j, 0))]
        args += [src_gu, src_d]
        out_specs += [pl.BlockSpec(gu_blk, lambda i, j: (i, j)),
                      pl.BlockSpec(d_blk, lambda i, j: (i * nf + j, 0))]
        out_shape += [jax.ShapeDtypeStruct((d, 2 * f), BF16), jax.ShapeDtypeStruct((f, d), BF16)]
    outs = pl.pallas_call(
        functools.partial(_ffn_kernel, final_norm=final_g is not None,
                          convert=convert is not None),
        out_shape=out_shape,
        grid=(ni, nf),
        in_specs=in_specs,
        out_specs=out_specs,
        scratch_shapes=[pltpu.VMEM((tm, d), BF16)],
        compiler_params=_params("parallel", "arbitrary", vmem_limit=FFN_VMEM_LIMIT_BYTES),
        name="ffn",
    )(*args)
    return outs[0] if convert is None else outs


def _proj_residual_kernel(a_ref, w_ref, x_ref, b_ref, o_ref):
    acc = jnp.dot(a_ref[...], w_ref[...], preferred_element_type=F32)
    o_ref[...] = x_ref[...] + acc + b_ref[...]


def _proj_residual(a, w, layer, x, bias, tm_cap=1024, tn_cap=1024):
    m, k = a.shape
    d = w.shape[2]
    tm = _tile(m, tm_cap, 8)
    tn = _tile(d, tn_cap)
    if bias is None:
        bias = jnp.zeros((d,), F32)
    return pl.pallas_call(
        _proj_residual_kernel,
        out_shape=jax.ShapeDtypeStruct((m, d), F32),
        grid=(m // tm, d // tn),
        in_specs=[
            pl.BlockSpec((tm, k), lambda i, j: (i, 0)),
            pl.BlockSpec((None, k, tn), lambda i, j: (layer, 0, j)),
            pl.BlockSpec((tm, tn), lambda i, j: (i, j)),
            pl.BlockSpec((1, tn), lambda i, j: (0, j)),
        ],
        out_specs=pl.BlockSpec((tm, tn), lambda i, j: (i, j)),
        compiler_params=_params("parallel", "arbitrary"),
        name="proj_residual",
    )(a, w, x, bias.reshape(1, d))


def _sb_kernel(q_ref, k_ref, v_ref, tri_ref, o_ref, acc_ref, car_ref, z_ref, *, tq, tk):
    row0 = pl.program_id(2) * tq
    acc_ref[...] = jnp.zeros_like(acc_ref)
    tri = tri_ref[...]
    nsub = tq // tk
    nheads = q_ref.shape[1] // HEAD_DIM
    assert nheads % 2 == 0
    cols = [slice(h * HEAD_DIM, (h + 1) * HEAD_DIM) for h in range(nheads)]

    def logits(h, ks):
        return lax.dot_general(q_ref[:, cols[h]], k_ref[pl.ds(ks, tq), cols[h]],
                               (((1,), (1,)), ((), ())), preferred_element_type=F32)

    def neg_log2_keep(z2):
        neg_abs = lax.bitcast_convert_type(
            lax.bitcast_convert_type(z2, jnp.uint32) | jnp.uint32(0x80000000), F32)
        return jnp.maximum(z2, 0.0) + jnp.log2(1.0 + jnp.exp2(neg_abs))

    def prefetch(h, ks, ks_next):
        if h + 1 < nheads:
            z_ref[(h + 1) % 2] = logits(h + 1, ks)
        else:
            z_ref[0] = logits(0, ks_next)

    def full_tile(ks, ks_next):
        for h, hs in enumerate(cols):
            prefetch(h, ks, ks_next)
            zr = z_ref.at[h % 2]
            nlb = neg_log2_keep(zr[...]).astype(BF16)
            car = car_ref[:, hs]
            parts = [None] * nsub
            for j in reversed(range(nsub)):
                sl = slice(j * tk, (j + 1) * tk)
                ssum = jnp.dot(nlb[:, sl], tri, preferred_element_type=F32)
                parts[j] = jnp.exp2(zr[:, sl] - ssum
                                    - jnp.concatenate([car] * (tk // LANES), axis=1))
                car = car + jnp.broadcast_to(ssum[:, 0:1], (tq, LANES))
            a = jnp.concatenate(parts, axis=1)
            acc_ref[:, hs] += jnp.dot(a.astype(BF16), v_ref[pl.ds(ks, tq), hs],
                                      preferred_element_type=F32)
            car_ref[:, hs] = car

    def diagonal_tile(ks, ks_next):
        lower = (lax.broadcasted_iota(jnp.int32, (tk, tk), 1)
                 < lax.broadcasted_iota(jnp.int32, (tk, tk), 0))
        for h, hs in enumerate(cols):
            prefetch(h, ks, ks_next)
            zr = z_ref.at[h % 2]
            car = jnp.zeros((tq, LANES), F32)
            for j in reversed(range(nsub)):
                r0 = j * tk
                nr = tq - r0
                sl = slice(j * tk, (j + 1) * tk)
                mask = lower if nr == tk else jnp.concatenate(
                    [lower, jnp.ones((nr - tk, tk), jnp.bool_)], axis=0)
                z2 = zr[r0:, sl]
                nl = jnp.where(mask, neg_log2_keep(z2), 0.0)
                ssum = jnp.dot(nl.astype(BF16), tri, preferred_element_type=F32)
                a = jnp.exp2(z2 - ssum - jnp.concatenate([car[r0:]] * (tk // LANES), axis=1))
                a = jnp.where(mask, a, 0.0)
                acc_ref[r0:, hs] += jnp.dot(a.astype(BF16), v_ref[pl.ds(ks + r0, tk), hs],
                                            preferred_element_type=F32)
                upd = car[r0:] + jnp.broadcast_to(ssum[:, 0:1], (nr, LANES))
                car = upd if r0 == 0 else jnp.concatenate([car[:r0], upd], axis=0)
            car_ref[:, hs] = car

    def start(t):
        return pl.multiple_of(jnp.maximum(row0 - (t + 1) * tq, 0), tq)

    z_ref[0] = logits(0, pl.multiple_of(row0, tq))
    diagonal_tile(pl.multiple_of(row0, tq), start(0))

    def body(t, c):
        full_tile(start(t), start(t + 1))
        return c

    lax.fori_loop(0, pl.program_id(2), body, 0)
    o_ref[...] = acc_ref[...].astype(o_ref.dtype)


def _sb_q_scale(heads):
    width = heads * HEAD_DIM
    return jnp.concatenate([jnp.full((width,), HEAD_DIM ** -0.5 * LOG2E, F32),
                            jnp.ones((2 * width,), F32)])


def _sb_attention(qkv, heads, tq_cap=512, tk_cap=256, heads_per_step=4):
    b, s, _ = qkv.shape
    tq = _tile(s, tq_cap)
    tk = _tile(tq, tk_cap)
    hp = heads_per_step if heads % heads_per_step == 0 else 1
    hw = hp * HEAD_DIM
    hb = heads // hp
    tri = jnp.tril(jnp.ones((tk, tk), F32)).astype(BF16)
    kern = functools.partial(_sb_kernel, tq=tq, tk=tk)
    return pl.pallas_call(
        kern,
        out_shape=jax.ShapeDtypeStruct((b, s, heads * HEAD_DIM), BF16),
        grid=(b, hb, s // tq),
        in_specs=[
            pl.BlockSpec((None, tq, hw), lambda bi, h, i: (bi, i, h)),
            pl.BlockSpec((None, s, hw), lambda bi, h, i: (bi, 0, hb + h)),
            pl.BlockSpec((None, s, hw), lambda bi, h, i: (bi, 0, 2 * hb + h)),
            pl.BlockSpec((tk, tk), lambda bi, h, i: (0, 0)),
        ],
        out_specs=pl.BlockSpec((None, tq, hw), lambda bi, h, i: (bi, i, h)),
        scratch_shapes=[pltpu.VMEM((tq, hw), F32), pltpu.VMEM((tq, hw), F32),
                        pltpu.VMEM((2, tq, tq), F32)],
        compiler_params=_params("parallel", "parallel", "arbitrary"),
        name="sb_attention",
    )(qkv, qkv, qkv, tri)


def _rope_table_kernel(pos_ref, freq_ref, sign_ref, cos_ref, sin_ref):
    ang = pos_ref[...].astype(F32) * freq_ref[...]
    cos_ref[...] = jnp.cos(ang)
    sin_ref[...] = jnp.sin(ang) * sign_ref[...]


def _rope_tables(positions, head_dim):
    m = positions.shape[0]
    rot = head_dim // ROPE_FRACTION
    half = rot // 2
    inv_freq = jnp.power(F32(ROPE_THETA), -jnp.arange(half, dtype=F32) * 2.0 / rot)
    lane = jnp.arange(LANES) % head_dim
    freq = jnp.where(lane < rot, inv_freq[lane % half], 0.0).astype(F32)
    sign = jnp.where(lane < half, -1.0, jnp.where(lane < rot, 1.0, 0.0)).astype(F32)
    tm = _tile(m, 2048, 8)
    out = jax.ShapeDtypeStruct((m, LANES), F32)
    return pl.pallas_call(
        _rope_table_kernel,
        out_shape=(out, out),
        grid=(m // tm,),
        in_specs=[
            pl.BlockSpec((tm, 1), lambda i: (i, 0)),
            pl.BlockSpec((1, LANES), lambda i: (0, 0)),
            pl.BlockSpec((1, LANES), lambda i: (0, 0)),
        ],
        out_specs=(pl.BlockSpec((tm, LANES), lambda i: (i, 0)),
                   pl.BlockSpec((tm, LANES), lambda i: (i, 0))),
        compiler_params=_params("parallel"),
        name="rope_tables",
    )(positions, freq.reshape(1, LANES), sign.reshape(1, LANES))


def _rope(x, cos, sin, head_dim):
    half = head_dim // ROPE_FRACTION // 2
    lane = lax.broadcasted_iota(jnp.int32, x.shape, 1) & (head_dim - 1)
    ahead = pltpu.roll(x, LANES - half, 1)
    behind = pltpu.roll(x, half, 1)
    return x * cos + jnp.where(lane < half, ahead, behind) * sin


def _band_kernel(q_ref, k_ref, v_ref, cos_ref, sin_ref, o_ref, lse_ref, qr_ref, kr_ref, vr_ref,
                 *, dil, length, max_dist, scale):
    kj = lax.broadcasted_iota(jnp.int32, (BLOCK, 2 * BLOCK), 1)
    dist = BLOCK + lax.broadcasted_iota(jnp.int32, (BLOCK, 2 * BLOCK), 0) - kj
    band = (dist >= 0) & (dist <= max_dist)
    band0 = band & (kj >= BLOCK)
    nblocks = length // BLOCK
    kr_ref[:, 0:BLOCK, :] = jnp.zeros((kr_ref.shape[0], BLOCK, LANES), BF16)
    vr_ref[:, 0:BLOCK, :] = jnp.zeros((vr_ref.shape[0], BLOCK, LANES), BF16)

    def block(p, slot, a, mask):
        q0 = pl.multiple_of(a * BLOCK, BLOCK)
        q = qr_ref[slot, pl.ds(q0, BLOCK), :]
        k = kr_ref[slot, pl.ds(q0, 2 * BLOCK), :]
        v = vr_ref[slot, pl.ds(q0, 2 * BLOCK), :]
        s = lax.dot_general(q, k, (((1,), (1,)), ((), ())),
                            preferred_element_type=F32) * scale
        s = jnp.where(mask, s, -jnp.inf)
        m = jnp.max(s, axis=-1, keepdims=True)
        e = jnp.exp(s - m)
        l = jnp.sum(e, axis=-1, keepdims=True)
        o = jnp.dot(e.astype(BF16), v, preferred_element_type=F32) / l
        dest = pl.ds(p + q0 * dil, BLOCK, stride=dil)
        o_ref[dest, :] = o
        lse_ref[dest, :] = jnp.broadcast_to(m + jnp.log(l), (BLOCK, LANES))

    ncls = qr_ref.shape[0]
    group = min(4, nblocks)
    assert nblocks % group == 0 and dil % ncls == 0

    def classes(t, c):
        ps = [t * ncls + slot for slot in range(ncls)]
        for slot, p in enumerate(ps):
            rows = pl.ds(p, length, stride=dil)
            cos = cos_ref[rows, :]
            sin = sin_ref[rows, :]
            qr_ref[slot] = _rope(q_ref[rows, :], cos, sin, HEAD_DIM).astype(BF16)
            kr_ref[slot, BLOCK:, :] = _rope(k_ref[rows, :], cos, sin, HEAD_DIM).astype(BF16)
            vr_ref[slot, BLOCK:, :] = v_ref[rows, :].astype(BF16)
        for slot, p in enumerate(ps):
            block(p, slot, 0, band0)
            for a in range(1, group):
                block(p, slot, a, band)

        def trip(u, c2):
            for slot, p in enumerate(ps):
                for a in range(group):
                    block(p, slot, group * u + a, band)
            return c2

        lax.fori_loop(1, nblocks // group, trip, 0)
        return c

    lax.fori_loop(0, dil // ncls, classes, 0)


def _band_attention(qkv, cos, sin, group, hpg, n_heads, window, dil):
    b, s, n = qkv.shape
    length = s // dil
    assert length % BLOCK == 0 and window % dil == 0 and window // dil <= BLOCK
    h0 = group * hpg

    def col(which):
        return pl.BlockSpec((None, s, LANES), lambda bi, h: (bi, 0, which * n_heads + h0 + h))

    tab = pl.BlockSpec((None, s, LANES), lambda bi, h: (bi, 0, 0))
    out = pl.BlockSpec((None, s, LANES), lambda bi, h: (bi, 0, h))
    shape = jax.ShapeDtypeStruct((b, s, hpg * LANES), F32)
    ncls = min(dil, max(1, (4 * BLOCK) // length))
    kern = functools.partial(_band_kernel, dil=dil, length=length, max_dist=window // dil,
                             scale=HEAD_DIM ** -0.5)
    o, lse = pl.pallas_call(
        kern,
        out_shape=(shape, shape),
        grid=(b, hpg),
        in_specs=[col(0), col(1), col(2), tab, tab],
        out_specs=(out, out),
        scratch_shapes=[pltpu.VMEM((ncls, length, LANES), BF16),
                        pltpu.VMEM((ncls, length + BLOCK, LANES), BF16),
                        pltpu.VMEM((ncls, length + BLOCK, LANES), BF16)],
        compiler_params=_params("parallel", "parallel"),
        name=f"band_attention_d{dil}",
    )(qkv, qkv, qkv, cos, sin)
    return (o.reshape(b * s, hpg * LANES), lse.reshape(b * s, hpg * LANES))


def _dil_out_kernel(*refs, groups):
    o_refs = refs[:groups]
    l_refs = refs[groups:2 * groups]
    w_ref, x_ref, out_ref, a_ref = refs[2 * groups:]

    @pl.when(pl.program_id(1) == 0)
    def _():
        lses = [r[...] for r in l_refs]
        mx = functools.reduce(jnp.maximum, lses)
        es = [jnp.exp(l - mx) for l in lses]
        inv = 1.0 / functools.reduce(lambda u, v: u + v, es)
        w = o_refs[0].shape[1]
        for g in range(groups):
            a_ref[:, g * w:(g + 1) * w] = (o_refs[g][...] * (es[g] * inv)).astype(BF16)

    out_ref[...] = x_ref[...] + jnp.dot(a_ref[...], w_ref[...], preferred_element_type=F32)


def _dil_out(outs, lses, w_o, layer, x, tm_cap=512, tn_cap=1024):
    m, d = x.shape
    groups = len(outs)
    gw = outs[0].shape[1]
    k = groups * gw
    tm = _tile(m, tm_cap, 8)
    tn = _tile(d, tn_cap)
    part = pl.BlockSpec((tm, gw), lambda i, j: (i, 0))
    return pl.pallas_call(
        functools.partial(_dil_out_kernel, groups=groups),
        out_shape=jax.ShapeDtypeStruct((m, d), F32),
        grid=(m // tm, d // tn),
        in_specs=[part] * (2 * groups) + [
            pl.BlockSpec((None, k, tn), lambda i, j: (layer, 0, j)),
            pl.BlockSpec((tm, tn), lambda i, j: (i, j)),
        ],
        out_specs=pl.BlockSpec((tm, tn), lambda i, j: (i, j)),
        scratch_shapes=[pltpu.VMEM((tm, k), BF16)],
        compiler_params=_params("parallel", "arbitrary"),
        name="dil_out",
    )(*outs, *lses, w_o, x)


def _swa_kernel(sink_ref, q_ref, kc_ref, kp_ref, vc_ref, vp_ref, cc_ref, cp_ref, sc_ref, sp_ref,
                o_ref, qr_ref, kr_ref, vr_ref, *, tq, scale):
    pair = pl.program_id(1)
    has_prev = pl.program_id(2) > 0
    d = SWA_HEAD_DIM
    cos = cc_ref[...]
    sin = sc_ref[...]
    kx = jnp.concatenate([_rope(kp_ref[...].astype(F32), cp_ref[...], sp_ref[...], d),
                          _rope(kc_ref[...].astype(F32), cos, sin, d)], axis=0)
    vx = jnp.concatenate([vp_ref[...], vc_ref[...]], axis=0).astype(F32)
    low = lax.broadcasted_iota(jnp.int32, kx.shape, 1) < d
    for src, dst in ((kx, kr_ref), (vx, vr_ref)):
        swapped = pltpu.roll(src, d, 1)
        dst[0] = jnp.where(low, src, 0.0).astype(BF16)
        dst[1] = jnp.where(low, 0.0, swapped).astype(BF16)
        dst[2] = jnp.where(low, swapped, 0.0).astype(BF16)
        dst[3] = jnp.where(low, 0.0, src).astype(BF16)
    nblk = q_ref.shape[1] // LANES
    for c in range(nblk):
        sl = slice(c * LANES, (c + 1) * LANES)
        qr_ref[:, sl] = _rope(q_ref[:, sl].astype(F32), cos, sin, d).astype(BF16)

    qi = lax.broadcasted_iota(jnp.int32, (BLOCK, 2 * BLOCK), 0)
    kj = lax.broadcasted_iota(jnp.int32, (BLOCK, 2 * BLOCK), 1)
    dist = BLOCK + qi - kj
    band = (dist >= 0) & (dist <= SWA_WINDOW - 1)
    band0 = band & ((kj >= BLOCK) | has_prev)
    per_kv = nblk // (LANES // d)
    for r in range(tq // BLOCK):
        rows = slice(r * BLOCK, (r + 1) * BLOCK)
        mask = jnp.concatenate([band0 if r == 0 else band] * per_kv, axis=0)
        for kv in range(LANES // d):
            blocks = range(kv * per_kv, (kv + 1) * per_kv)
            q = jnp.concatenate([qr_ref[rows, c * LANES:(c + 1) * LANES] for c in blocks], axis=0)
            out = None
            for half in range(2):
                k = kr_ref[2 * kv + half, r * BLOCK:(r + 2) * BLOCK, :]
                v = vr_ref[2 * kv + half, r * BLOCK:(r + 2) * BLOCK, :]
                sink = jnp.concatenate(
                    [jnp.full((BLOCK, 1), sink_ref[(pair * nblk + c) * 2 + half], F32)
                     for c in blocks], axis=0)
                s = lax.dot_general(q, k, (((1,), (1,)), ((), ())),
                                    preferred_element_type=F32) * scale
                s = jnp.where(mask, s, -jnp.inf)
                m = jnp.maximum(jnp.max(s, axis=-1, keepdims=True), sink)
                p = jnp.exp(s - m)
                l = jnp.sum(p, axis=-1, keepdims=True) + jnp.exp(sink - m)
                o = jnp.dot(p.astype(BF16), v, preferred_element_type=F32) / l
                out = o if out is None else out + o
            for n, c in enumerate(blocks):
                o_ref[rows, c * LANES:(c + 1) * LANES] = (
                    out[n * BLOCK:(n + 1) * BLOCK].astype(o_ref.dtype))


def _swa_attention(qkv, cos, sin, sinks, q_heads, tq_cap=512):
    b, s, _ = qkv.shape
    d = SWA_HEAD_DIM
    kv_heads = q_heads // SWA_GROUP
    per_blk = LANES // d
    assert kv_heads % per_blk == 0
    nq = q_heads * d
    qw = per_blk * SWA_GROUP * d
    kcol = nq // LANES
    vcol = (nq + kv_heads * d) // LANES
    tq = _tile(s, tq_cap)
    rb = tq // BLOCK

    def cur(col0):
        return pl.BlockSpec((None, tq, LANES), lambda bi, p, i: (bi, i, col0 + p))

    def prev(col0):
        return pl.BlockSpec((None, BLOCK, LANES),
                            lambda bi, p, i: (bi, jnp.maximum(i * rb - 1, 0), col0 + p))

    tab_c = pl.BlockSpec((None, tq, LANES), lambda bi, p, i: (bi, i, 0))
    tab_p = pl.BlockSpec((None, BLOCK, LANES), lambda bi, p, i: (bi, jnp.maximum(i * rb - 1, 0), 0))
    kern = functools.partial(_swa_kernel, tq=tq, scale=d ** -0.5)
    return pl.pallas_call(
        kern,
        out_shape=jax.ShapeDtypeStruct((b, s, nq), BF16),
        grid=(b, kv_heads // per_blk, s // tq),
        in_specs=[
            pl.BlockSpec(memory_space=pltpu.SMEM),
            pl.BlockSpec((None, tq, qw), lambda bi, p, i: (bi, i, p)),
            cur(kcol), prev(kcol), cur(vcol), prev(vcol),
            tab_c, tab_p, tab_c, tab_p,
        ],
        out_specs=pl.BlockSpec((None, tq, qw), lambda bi, p, i: (bi, i, p)),
        scratch_shapes=[pltpu.VMEM((tq, qw), BF16),
                        pltpu.VMEM((2 * per_blk, tq + BLOCK, LANES), BF16),
                        pltpu.VMEM((2 * per_blk, tq + BLOCK, LANES), BF16)],
        compiler_params=_params("parallel", "parallel", "parallel"),
        name="swa_attention",
    )(sinks, qkv, qkv, qkv, qkv, qkv, cos, cos, sin, sin)


def _xattn_kernel(x_ref, g_ref, wq_ref, kv_ref, wo_ref, o_ref, a_ref, *, scale):
    x = x_ref[...]
    q = jnp.dot(_rms(x, g_ref[...]).astype(BF16), wq_ref[...],
                preferred_element_type=F32).astype(BF16)
    width = XA_HEADS * XA_HEAD_DIM
    for h in range(XA_HEADS):
        sl = slice(h * XA_HEAD_DIM, (h + 1) * XA_HEAD_DIM)
        k = kv_ref[:, sl]
        v = kv_ref[:, width + h * XA_HEAD_DIM:width + (h + 1) * XA_HEAD_DIM]
        s = lax.dot_general(q[:, sl], k, (((1,), (1,)), ((), ())),
                            preferred_element_type=F32) * scale
        m = jnp.max(s, axis=-1, keepdims=True)
        p = jnp.exp(s - m)
        l = jnp.sum(p, axis=-1, keepdims=True)
        a_ref[:, sl] = (jnp.dot(p.astype(BF16), v, preferred_element_type=F32) / l).astype(BF16)
    o_ref[...] = x + jnp.dot(a_ref[...], wo_ref[...], preferred_element_type=F32)


def _xattn(x, g, w_q, kv, w_o, layer, seq, tm_cap=512):
    m, d = x.shape
    n_mem = kv.shape[1]
    width = XA_HEADS * XA_HEAD_DIM
    tm = _tile(seq, tm_cap, 8)
    per_batch = seq // tm
    return pl.pallas_call(
        functools.partial(_xattn_kernel, scale=XA_HEAD_DIM ** -0.5),
        out_shape=jax.ShapeDtypeStruct((m, d), F32),
        grid=(m // tm,),
        in_specs=[
            pl.BlockSpec((tm, d), lambda i: (i, 0)),
            pl.BlockSpec((1, d), lambda i: (0, 0)),
            pl.BlockSpec((None, d, width), lambda i: (layer, 0, 0)),
            pl.BlockSpec((None, n_mem, 2 * width), lambda i: (i // per_batch, 0, 0)),
            pl.BlockSpec((None, width, d), lambda i: (layer, 0, 0)),
        ],
        out_specs=pl.BlockSpec((tm, d), lambda i: (i, 0)),
        scratch_shapes=[pltpu.VMEM((tm, width), BF16)],
        compiler_params=_params("parallel"),
        name="xattn",
    )(x, g.reshape(1, d), w_q, kv, w_o)


def kernel(x, mem, positions, ffn1_norm, ffn1_w_gate_up, ffn1_w_down, mix_norm, sb_w_qkv, sb_w_o, dil_w_qkv, dil_w_o, swa_w_qkv, swa_b_qkv, swa_sinks, swa_w_o, swa_b_o, xattn_norm, mem_norm, xattn_w_q, xattn_w_kv, xattn_w_o, ffn2_norm, ffn2_w_gate_up, ffn2_w_down, final_norm):
    b, s, d = x.shape
    m = b * s
    depth = ffn1_norm.shape[0]
    n_mem = mem.shape[1]
    sb_heads = d // HEAD_DIM
    dil_hpg = d // (4 * HEAD_DIM)
    dil_heads = dil_hpg * len(DIL_PATTERNS)
    swa_q_heads = d // SWA_HEAD_DIM

    xs = x.reshape(m, d)
    mem2 = mem.reshape(b * n_mem, d)
    pos = positions.reshape(m, 1)
    if depth > 1:
        dil_cos, dil_sin = (t.reshape(b, s, LANES) for t in _rope_tables(pos, HEAD_DIM))
    if depth > 2:
        swa_cos, swa_sin = (t.reshape(b, s, LANES) for t in _rope_tables(pos, SWA_HEAD_DIM))

    (sb_w_qkv, sb_w_o, dil_w_qkv, dil_w_o, swa_w_qkv, swa_w_o, xattn_w_q, xattn_w_kv,
     xattn_w_o) = (w.astype(BF16) for w in (sb_w_qkv, sb_w_o, dil_w_qkv, dil_w_o, swa_w_qkv,
                                            swa_w_o, xattn_w_q, xattn_w_kv, xattn_w_o))
    chain = _ffn_can_convert(m, d, ffn1_w_down.shape[1])

    def cast_ffn(w_gu, w_d, layer):
        return w_gu[layer].astype(BF16), w_d[layer].astype(BF16)

    w1 = cast_ffn(ffn1_w_gate_up, ffn1_w_down, 0)
    for i in range(depth):
        if chain:
            xs, *w2 = _ffn(xs, ffn1_norm[i], *w1, convert=(ffn2_w_gate_up, ffn2_w_down, i))
        else:
            xs = _ffn(xs, ffn1_norm[i], *w1)
            w2 = cast_ffn(ffn2_w_gate_up, ffn2_w_down, i)

        kind, j = i % 3, i // 3
        if kind == 0:
            qkv = _norm_matmul(xs, mix_norm[i], sb_w_qkv, j, None, BF16,
                               col_scale=_sb_q_scale(sb_heads))
            o = _sb_attention(qkv.reshape(b, s, -1), sb_heads)
            xs = _proj_residual(o.reshape(m, -1), sb_w_o, j, xs, None)
        elif kind == 1:
            qkv = _norm_matmul(xs, mix_norm[i], dil_w_qkv, j, None, F32)
            qkv = qkv.reshape(b, s, -1)
            outs, lses = [], []
            for g, (window, dil) in enumerate(DIL_PATTERNS):
                o, lse = _band_attention(qkv, dil_cos, dil_sin, g, dil_hpg, dil_heads, window, dil)
                outs.append(o)
                lses.append(lse)
            xs = _dil_out(outs, lses, dil_w_o, j, xs)
        else:
            qkv = _norm_matmul(xs, mix_norm[i], swa_w_qkv, j, swa_b_qkv[j], BF16)
            o = _swa_attention(qkv.reshape(b, s, -1), swa_cos, swa_sin, swa_sinks[j], swa_q_heads)
            xs = _proj_residual(o.reshape(m, -1), swa_w_o, j, xs, swa_b_o[j])

        kv = _norm_matmul(mem2, mem_norm[i], xattn_w_kv, i, None, BF16)
        xs = _xattn(xs, xattn_norm[i], xattn_w_q, kv.reshape(b, n_mem, -1), xattn_w_o, i, s)

        last = i == depth - 1
        if chain and not last:
            xs, *w1 = _ffn(xs, ffn2_norm[i], *w2, convert=(ffn1_w_gate_up, ffn1_w_down, i + 1))
        else:
            xs = _ffn(xs, ffn2_norm[i], *w2, final_g=final_norm if last else None)
            if not last:
                w1 = cast_ffn(ffn1_w_gate_up, ffn1_w_down, i + 1)

    return xs.reshape(b, s, d)
```

```python
import functools

import jax
import jax.numpy as jnp
from jax import lax
from jax.experimental import pallas as pl
from jax.experimental.pallas import tpu as pltpu

F32 = jnp.float32
BF16 = jnp.bfloat16

NORM_EPS = 1e-6
ROPE_THETA = 500000.0
ROPE_FRACTION = 4
LOG2E = 1.4426950408889634
HEAD_DIM = 128
BLOCK = 128
DIL_PATTERNS = ((128, 1), (512, 4), (2048, 16))
SWA_HEAD_DIM = 64
SWA_GROUP = 8
SWA_WINDOW = 128
XA_HEADS = 4
XA_HEAD_DIM = 128

LANES = 128
V7X_VMEM_BYTES = 64 * 2**20
VMEM_LIMIT_BYTES = (V7X_VMEM_BYTES * 3) // 4
FFN_VMEM_LIMIT_BYTES = (V7X_VMEM_BYTES * 15) // 16


def _params(*semantics, vmem_limit=VMEM_LIMIT_BYTES):
    return pltpu.CompilerParams(dimension_semantics=semantics, vmem_limit_bytes=vmem_limit)


def _tile(n, cap, mult=LANES):
    if n <= cap:
        return n
    t = (cap // mult) * mult
    while t > mult and n % t:
        t -= mult
    assert n % t == 0, (n, cap, mult)
    return t


def _rms(x, g):
    ms = jnp.mean(x * x, axis=-1, keepdims=True)
    return x * lax.rsqrt(ms + NORM_EPS) * g


def _norm_matmul_kernel(x_ref, g_ref, w_ref, b_ref, s_ref, o_ref, xn_ref):
    @pl.when(pl.program_id(1) == 0)
    def _():
        xn_ref[...] = _rms(x_ref[...], g_ref[...]).astype(BF16)

    acc = jnp.dot(xn_ref[...], w_ref[...], preferred_element_type=F32)
    o_ref[...] = ((acc + b_ref[...]) * s_ref[...]).astype(o_ref.dtype)


def _norm_matmul(x, g, w, layer, bias, out_dtype, col_scale=None, tm_cap=1024):
    m, d = x.shape
    n = w.shape[2]
    tm = _tile(m, tm_cap, 8)
    tn = _tile(n, (9 * 2**20) // (2 * d + jnp.dtype(out_dtype).itemsize * tm) // LANES * LANES)
    if bias is None:
        bias = jnp.zeros((n,), F32)
    if col_scale is None:
        col_scale = jnp.ones((n,), F32)
    return pl.pallas_call(
        _norm_matmul_kernel,
        out_shape=jax.ShapeDtypeStruct((m, n), out_dtype),
        grid=(m // tm, n // tn),
        in_specs=[
            pl.BlockSpec((tm, d), lambda i, j: (i, 0)),
            pl.BlockSpec((1, d), lambda i, j: (0, 0)),
            pl.BlockSpec((None, d, tn), lambda i, j: (layer, 0, j)),
            pl.BlockSpec((1, tn), lambda i, j: (0, j)),
            pl.BlockSpec((1, tn), lambda i, j: (0, j)),
        ],
        out_specs=pl.BlockSpec((tm, tn), lambda i, j: (i, j)),
        scratch_shapes=[pltpu.VMEM((tm, d), BF16)],
        compiler_params=_params("parallel", "arbitrary"),
        name="norm_matmul",
    )(x, g.reshape(1, d), w, bias.reshape(1, n), col_scale.reshape(1, n))


def _ffn_kernel(x_ref, g_ref, wg_ref, wu_ref, wd_ref, gf_ref, *rest, final_norm, convert):
    if convert:
        ngu_ref, nd_ref, o_ref, ngu_out, nd_out, xn_ref = rest
        ngu_out[...] = ngu_ref[...].astype(BF16)
        nd_out[...] = nd_ref[...].astype(BF16)
    else:
        o_ref, xn_ref = rest
    j = pl.program_id(1)

    @pl.when(j == 0)
    def _():
        xn_ref[...] = _rms(x_ref[...], g_ref[...]).astype(BF16)
        o_ref[...] = jnp.zeros_like(o_ref)

    xn = xn_ref[...]
    gate = jnp.dot(xn, wg_ref[...], preferred_element_type=F32)
    up = jnp.dot(xn, wu_ref[...], preferred_element_type=F32)
    h = gate * jax.nn.sigmoid(gate) * up
    o_ref[...] += jnp.dot(h.astype(BF16), wd_ref[...], preferred_element_type=F32)

    @pl.when(j == pl.num_programs(1) - 1)
    def _():
        rows = min(256, o_ref.shape[0])

        def finish(c, carry):
            sl = pl.ds(pl.multiple_of(c * rows, rows), rows)
            y = x_ref[sl, :] + 0.5 * o_ref[sl, :]
            if final_norm:
                y = _rms(y, gf_ref[...])
            o_ref[sl, :] = y
            return carry

        lax.fori_loop(0, o_ref.shape[0] // rows, finish, 0)


def _ffn_tiles(m, f, tm_cap=1024, tf_cap=512):
    return _tile(m, tm_cap, 8), _tile(f, tf_cap)


def _ffn_can_convert(m, d, f):
    tm, tf = _ffn_tiles(m, f)
    steps_i = m // tm
    return d % (steps_i * 16) == 0 and tf % (steps_i * 16) == 0


def _ffn(x, g, w_gu, w_d, final_g=None, convert=None):
    m, d = x.shape
    f = w_d.shape[0]
    tm, tf = _ffn_tiles(m, f)
    nf = f // tf
    ni = m // tm
    gf = jnp.ones((d,), F32) if final_g is None else final_g
    in_specs = [
        pl.BlockSpec((tm, d), lambda i, j: (i, 0)),
        pl.BlockSpec((1, d), lambda i, j: (0, 0)),
        pl.BlockSpec((d, tf), lambda i, j: (0, j)),
        pl.BlockSpec((d, tf), lambda i, j: (0, j + nf)),
        pl.BlockSpec((tf, d), lambda i, j: (j, 0)),
        pl.BlockSpec((1, d), lambda i, j: (0, 0)),
    ]
    args = [x, g.reshape(1, d), w_gu, w_gu, w_d, gf.reshape(1, d)]
    out_specs = [pl.BlockSpec((tm, d), lambda i, j: (i, 0))]
    out_shape = [jax.ShapeDtypeStruct((m, d), F32)]
    if convert is not None:
        src_gu, src_d, layer = convert
        gu_blk = (d // ni, 2 * tf)
        d_blk = (tf // ni, d)
        in_specs += [pl.BlockSpec((None,) + gu_blk, lambda i, j: (layer, i, j)),
                     pl.BlockSpec((None,) + d_blk, lambda i, j: (layer, i * nf + j, 0))]
        args += [src_gu, src_d]
        out_specs += [pl.BlockSpec(gu_blk, lambda i, j: (i, j)),
                      pl.BlockSpec(d_blk, lambda i, j: (i * nf + j, 0))]
        out_shape += [jax.ShapeDtypeStruct((d, 2 * f), BF16), jax.ShapeDtypeStruct((f, d), BF16)]
    outs = pl.pallas_call(
        functools.partial(_ffn_kernel, final_norm=final_g is not None,
                          convert=convert is not None),
        out_shape=out_shape,
        grid=(ni, nf),
        in_specs=in_specs,
        out_specs=out_specs,
        scratch_shapes=[pltpu.VMEM((tm, d), BF16)],
        compiler_params=_params("parallel", "arbitrary", vmem_limit=FFN_VMEM_LIMIT_BYTES),
        name="ffn",
    )(*args)
    return outs[0] if convert is None else outs


def _proj_residual_kernel(a_ref, w_ref, x_ref, b_ref, o_ref):
    acc = jnp.dot(a_ref[...], w_ref[...], preferred_element_type=F32)
    o_ref[...] = x_ref[...] + acc + b_ref[...]


def _proj_residual(a, w, layer, x, bias, tm_cap=512, tn_cap=2048):
    m, k = a.shape
    d = w.shape[2]
    tm = _tile(m, tm_cap, 8)
    tn = _tile(d, tn_cap)
    if bias is None:
        bias = jnp.zeros((d,), F32)
    return pl.pallas_call(
        _proj_residual_kernel,
        out_shape=jax.ShapeDtypeStruct((m, d), F32),
        grid=(m // tm, d // tn),
        in_specs=[
            pl.BlockSpec((tm, k), lambda i, j: (i, 0)),
            pl.BlockSpec((None, k, tn), lambda i, j: (layer, 0, j)),
            pl.BlockSpec((tm, tn), lambda i, j: (i, j)),
            pl.BlockSpec((1, tn), lambda i, j: (0, j)),
        ],
        out_specs=pl.BlockSpec((tm, tn), lambda i, j: (i, j)),
        compiler_params=_params("parallel", "arbitrary"),
        name="proj_residual",
    )(a, w, x, bias.reshape(1, d))


def _sb_kernel(q_ref, k_ref, v_ref, tri_ref, o_ref, acc_ref, car_ref, z_ref, *, tq, tk):
    row0 = pl.program_id(2) * tq
    acc_ref[...] = jnp.zeros_like(acc_ref)
    tri = tri_ref[...]
    nsub = tq // tk
    nheads = q_ref.shape[1] // HEAD_DIM
    assert nheads % 2 == 0
    cols = [slice(h * HEAD_DIM, (h + 1) * HEAD_DIM) for h in range(nheads)]

    def logits(h, ks):
        return lax.dot_general(q_ref[:, cols[h]], k_ref[pl.ds(ks, tq), cols[h]],
                               (((1,), (1,)), ((), ())), preferred_element_type=F32)

    def neg_log2_keep(z2):
        neg_abs = lax.bitcast_convert_type(
            lax.bitcast_convert_type(z2, jnp.uint32) | jnp.uint32(0x80000000), F32)
        return jnp.maximum(z2, 0.0) + jnp.log2(1.0 + jnp.exp2(neg_abs))

    def prefetch(h, ks, ks_next):
        if h + 1 < nheads:
            z_ref[(h + 1) % 2] = logits(h + 1, ks)
        else:
            z_ref[0] = logits(0, ks_next)

    def full_tile(ks, ks_next):
        for h, hs in enumerate(cols):
            prefetch(h, ks, ks_next)
            zr = z_ref.at[h % 2]
            nlb = neg_log2_keep(zr[...]).astype(BF16)
            car = car_ref[:, hs]
            parts = [None] * nsub
            for j in reversed(range(nsub)):
                sl = slice(j * tk, (j + 1) * tk)
                ssum = jnp.dot(nlb[:, sl], tri, preferred_element_type=F32)
                parts[j] = jnp.exp2(zr[:, sl] - ssum
                                    - jnp.concatenate([car] * (tk // LANES), axis=1))
                car = car + jnp.broadcast_to(ssum[:, 0:1], (tq, LANES))
            a = jnp.concatenate(parts, axis=1)
            acc_ref[:, hs] += jnp.dot(a.astype(BF16), v_ref[pl.ds(ks, tq), hs],
                                      preferred_element_type=F32)
            car_ref[:, hs] = car

    def diagonal_tile(ks, ks_next):
        lower = (lax.broadcasted_iota(jnp.int32, (tk, tk), 1)
                 < lax.broadcasted_iota(jnp.int32, (tk, tk), 0))
        for h, hs in enumerate(cols):
            prefetch(h, ks, ks_next)
            zr = z_ref.at[h % 2]
            car = jnp.zeros((tq, LANES), F32)
            for j in reversed(range(nsub)):
                r0 = j * tk
                nr = tq - r0
                sl = slice(j * tk, (j + 1) * tk)
                mask = lower if nr == tk else jnp.concatenate(
                    [lower, jnp.ones((nr - tk, tk), jnp.bool_)], axis=0)
                z2 = zr[r0:, sl]
                nl = jnp.where(mask, neg_log2_keep(z2), 0.0)
                ssum = jnp.dot(nl.astype(BF16), tri, preferred_element_type=F32)
                a = jnp.exp2(z2 - ssum - jnp.concatenate([car[r0:]] * (tk // LANES), axis=1))
                a = jnp.where(mask, a, 0.0)
                acc_ref[r0:, hs] += jnp.dot(a.astype(BF16), v_ref[pl.ds(ks + r0, tk), hs],
                                            preferred_element_type=F32)
                upd = car[r0:] + jnp.broadcast_to(ssum[:, 0:1], (nr, LANES))
                car = upd if r0 == 0 else jnp.concatenate([car[:r0], upd], axis=0)
            car_ref[:, hs] = car

    def start(t):
        return pl.multiple_of(jnp.maximum(row0 - (t + 1) * tq, 0), tq)

    z_ref[0] = logits(0, pl.multiple_of(row0, tq))
    diagonal_tile(pl.multiple_of(row0, tq), start(0))

    def body(t, c):
        full_tile(start(t), start(t + 1))
        return c

    lax.fori_loop(0, pl.program_id(2), body, 0)
    o_ref[...] = acc_ref[...].astype(o_ref.dtype)


def _sb_q_scale(heads):
    width = heads * HEAD_DIM
    return jnp.concatenate([jnp.full((width,), HEAD_DIM ** -0.5 * LOG2E, F32),
                            jnp.ones((2 * width,), F32)])


def _sb_attention(qkv, heads, tq_cap=512, tk_cap=256, heads_per_step=4):
    b, s, _ = qkv.shape
    tq = _tile(s, tq_cap)
    tk = _tile(tq, tk_cap)
    hp = heads_per_step if heads % heads_per_step == 0 else 1
    hw = hp * HEAD_DIM
    hb = heads // hp
    tri = jnp.tril(jnp.ones((tk, tk), F32)).astype(BF16)
    kern = functools.partial(_sb_kernel, tq=tq, tk=tk)
    return pl.pallas_call(
        kern,
        out_shape=jax.ShapeDtypeStruct((b, s, heads * HEAD_DIM), BF16),
        grid=(b, hb, s // tq),
        in_specs=[
            pl.BlockSpec((None, tq, hw), lambda bi, h, i: (bi, i, h)),
            pl.BlockSpec((None, s, hw), lambda bi, h, i: (bi, 0, hb + h)),
            pl.BlockSpec((None, s, hw), lambda bi, h, i: (bi, 0, 2 * hb + h)),
            pl.BlockSpec((tk, tk), lambda bi, h, i: (0, 0)),
        ],
        out_specs=pl.BlockSpec((None, tq, hw), lambda bi, h, i: (bi, i, h)),
        scratch_shapes=[pltpu.VMEM((tq, hw), F32), pltpu.VMEM((tq, hw), F32),
                        pltpu.VMEM((2, tq, tq), F32)],
        compiler_params=_params("parallel", "parallel", "arbitrary"),
        name="sb_attention",
    )(qkv, qkv, qkv, tri)


def _rope_table_kernel(pos_ref, freq_ref, sign_ref, cos_ref, sin_ref):
    ang = pos_ref[...].astype(F32) * freq_ref[...]
    cos_ref[...] = jnp.cos(ang)
    sin_ref[...] = jnp.sin(ang) * sign_ref[...]


def _rope_tables(positions, head_dim):
    m = positions.shape[0]
    rot = head_dim // ROPE_FRACTION
    half = rot // 2
    inv_freq = jnp.power(F32(ROPE_THETA), -jnp.arange(half, dtype=F32) * 2.0 / rot)
    lane = jnp.arange(LANES) % head_dim
    freq = jnp.where(lane < rot, inv_freq[lane % half], 0.0).astype(F32)
    sign = jnp.where(lane < half, -1.0, jnp.where(lane < rot, 1.0, 0.0)).astype(F32)
    tm = _tile(m, 2048, 8)
    out = jax.ShapeDtypeStruct((m, LANES), F32)
    return pl.pallas_call(
        _rope_table_kernel,
        out_shape=(out, out),
        grid=(m // tm,),
        in_specs=[
            pl.BlockSpec((tm, 1), lambda i: (i, 0)),
            pl.BlockSpec((1, LANES), lambda i: (0, 0)),
            pl.BlockSpec((1, LANES), lambda i: (0, 0)),
        ],
        out_specs=(pl.BlockSpec((tm, LANES), lambda i: (i, 0)),
                   pl.BlockSpec((tm, LANES), lambda i: (i, 0))),
        compiler_params=_params("parallel"),
        name="rope_tables",
    )(positions, freq.reshape(1, LANES), sign.reshape(1, LANES))


def _rope(x, cos, sin, head_dim):
    half = head_dim // ROPE_FRACTION // 2
    lane = lax.broadcasted_iota(jnp.int32, x.shape, 1) & (head_dim - 1)
    ahead = pltpu.roll(x, LANES - half, 1)
    behind = pltpu.roll(x, half, 1)
    return x * cos + jnp.where(lane < half, ahead, behind) * sin


def _band_kernel(q_ref, k_ref, v_ref, cos_ref, sin_ref, o_ref, lse_ref, qr_ref, kr_ref, vr_ref,
                 *, dil, length, max_dist, scale):
    kj = lax.broadcasted_iota(jnp.int32, (BLOCK, 2 * BLOCK), 1)
    dist = BLOCK + lax.broadcasted_iota(jnp.int32, (BLOCK, 2 * BLOCK), 0) - kj
    band = (dist >= 0) & (dist <= max_dist)
    band0 = band & (kj >= BLOCK)
    nblocks = length // BLOCK
    kr_ref[:, 0:BLOCK, :] = jnp.zeros((kr_ref.shape[0], BLOCK, LANES), BF16)
    vr_ref[:, 0:BLOCK, :] = jnp.zeros((vr_ref.shape[0], BLOCK, LANES), BF16)

    def block(p, slot, a, mask):
        q0 = pl.multiple_of(a * BLOCK, BLOCK)
        q = qr_ref[slot, pl.ds(q0, BLOCK), :]
        k = kr_ref[slot, pl.ds(q0, 2 * BLOCK), :]
        v = vr_ref[slot, pl.ds(q0, 2 * BLOCK), :]
        s = lax.dot_general(q, k, (((1,), (1,)), ((), ())),
                            preferred_element_type=F32) * scale
        s = jnp.where(mask, s, -jnp.inf)
        m = jnp.max(s, axis=-1, keepdims=True)
        e = jnp.exp(s - m)
        l = jnp.sum(e, axis=-1, keepdims=True)
        o = jnp.dot(e.astype(BF16), v, preferred_element_type=F32) / l
        dest = pl.ds(p + q0 * dil, BLOCK, stride=dil)
        o_ref[dest, :] = o
        lse_ref[dest, :] = jnp.broadcast_to(m + jnp.log(l), (BLOCK, LANES))

    ncls = qr_ref.shape[0]
    group = min(4, nblocks)
    assert nblocks % group == 0 and dil % ncls == 0

    def classes(t, c):
        ps = [t * ncls + slot for slot in range(ncls)]
        for slot, p in enumerate(ps):
            rows = pl.ds(p, length, stride=dil)
            cos = cos_ref[rows, :]
            sin = sin_ref[rows, :]
            qr_ref[slot] = _rope(q_ref[rows, :], cos, sin, HEAD_DIM).astype(BF16)
            kr_ref[slot, BLOCK:, :] = _rope(k_ref[rows, :], cos, sin, HEAD_DIM).astype(BF16)
            vr_ref[slot, BLOCK:, :] = v_ref[rows, :].astype(BF16)
        for slot, p in enumerate(ps):
            block(p, slot, 0, band0)
            for a in range(1, group):
                block(p, slot, a, band)

        def trip(u, c2):
            for slot, p in enumerate(ps):
                for a in range(group):
                    block(p, slot, group * u + a, band)
            return c2

        lax.fori_loop(1, nblocks // group, trip, 0)
        return c

    lax.fori_loop(0, dil // ncls, classes, 0)


def _band_attention(qkv, cos, sin, group, hpg, n_heads, window, dil):
    b, s, n = qkv.shape
    length = s // dil
    assert length % BLOCK == 0 and window % dil == 0 and window // dil <= BLOCK
    h0 = group * hpg

    def col(which):
        return pl.BlockSpec((None, s, LANES), lambda bi, h: (bi, 0, which * n_heads + h0 + h))

    tab = pl.BlockSpec((None, s, LANES), lambda bi, h: (bi, 0, 0))
    out = pl.BlockSpec((None, s, LANES), lambda bi, h: (bi, 0, h))
    shape = jax.ShapeDtypeStruct((b, s, hpg * LANES), F32)
    ncls = min(dil, max(1, (4 * BLOCK) // length))
    kern = functools.partial(_band_kernel, dil=dil, length=length, max_dist=window // dil,
                             scale=HEAD_DIM ** -0.5)
    o, lse = pl.pallas_call(
        kern,
        out_shape=(shape, shape),
        grid=(b, hpg),
        in_specs=[col(0), col(1), col(2), tab, tab],
        out_specs=(out, out),
        scratch_shapes=[pltpu.VMEM((ncls, length, LANES), BF16),
                        pltpu.VMEM((ncls, length + BLOCK, LANES), BF16),
                        pltpu.VMEM((ncls, length + BLOCK, LANES), BF16)],
        compiler_params=_params("parallel", "parallel"),
        name=f"band_attention_d{dil}",
    )(qkv, qkv, qkv, cos, sin)
    return (o.reshape(b * s, hpg * LANES), lse.reshape(b * s, hpg * LANES))


def _dil_out_kernel(*refs, groups):
    o_refs = refs[:groups]
    l_refs = refs[groups:2 * groups]
    w_ref, x_ref, out_ref, a_ref = refs[2 * groups:]

    @pl.when(pl.program_id(1) == 0)
    def _():
        lses = [r[...] for r in l_refs]
        mx = functools.reduce(jnp.maximum, lses)
        es = [jnp.exp(l - mx) for l in lses]
        inv = 1.0 / functools.reduce(lambda u, v: u + v, es)
        w = o_refs[0].shape[1]
        for g in range(groups):
            a_ref[:, g * w:(g + 1) * w] = (o_refs[g][...] * (es[g] * inv)).astype(BF16)

    out_ref[...] = x_ref[...] + jnp.dot(a_ref[...], w_ref[...], preferred_element_type=F32)


def _dil_out(outs, lses, w_o, layer, x, tm_cap=512, tn_cap=2048):
    m, d = x.shape
    groups = len(outs)
    gw = outs[0].shape[1]
    k = groups * gw
    tm = _tile(m, tm_cap, 8)
    tn = _tile(d, tn_cap)
    part = pl.BlockSpec((tm, gw), lambda i, j: (i, 0))
    return pl.pallas_call(
        functools.partial(_dil_out_kernel, groups=groups),
        out_shape=jax.ShapeDtypeStruct((m, d), F32),
        grid=(m // tm, d // tn),
        in_specs=[part] * (2 * groups) + [
            pl.BlockSpec((None, k, tn), lambda i, j: (layer, 0, j)),
            pl.BlockSpec((tm, tn), lambda i, j: (i, j)),
        ],
        out_specs=pl.BlockSpec((tm, tn), lambda i, j: (i, j)),
        scratch_shapes=[pltpu.VMEM((tm, k), BF16)],
        compiler_params=_params("parallel", "arbitrary"),
        name="dil_out",
    )(*outs, *lses, w_o, x)


def _swa_kernel(sink_ref, q_ref, kc_ref, kp_ref, vc_ref, vp_ref, cc_ref, cp_ref, sc_ref, sp_ref,
                o_ref, qr_ref, kr_ref, vr_ref, *, tq, scale):
    pair = pl.program_id(1)
    has_prev = pl.program_id(2) > 0
    d = SWA_HEAD_DIM
    cos = cc_ref[...]
    sin = sc_ref[...]
    kx = jnp.concatenate([_rope(kp_ref[...].astype(F32), cp_ref[...], sp_ref[...], d),
                          _rope(kc_ref[...].astype(F32), cos, sin, d)], axis=0)
    vx = jnp.concatenate([vp_ref[...], vc_ref[...]], axis=0).astype(F32)
    low = lax.broadcasted_iota(jnp.int32, kx.shape, 1) < d
    for src, dst in ((kx, kr_ref), (vx, vr_ref)):
        swapped = pltpu.roll(src, d, 1)
        dst[0] = jnp.where(low, src, 0.0).astype(BF16)
        dst[1] = jnp.where(low, 0.0, swapped).astype(BF16)
        dst[2] = jnp.where(low, swapped, 0.0).astype(BF16)
        dst[3] = jnp.where(low, 0.0, src).astype(BF16)
    nblk = q_ref.shape[1] // LANES
    for c in range(nblk):
        sl = slice(c * LANES, (c + 1) * LANES)
        qr_ref[:, sl] = _rope(q_ref[:, sl].astype(F32), cos, sin, d).astype(BF16)

    qi = lax.broadcasted_iota(jnp.int32, (BLOCK, 2 * BLOCK), 0)
    kj = lax.broadcasted_iota(jnp.int32, (BLOCK, 2 * BLOCK), 1)
    dist = BLOCK + qi - kj
    band = (dist >= 0) & (dist <= SWA_WINDOW - 1)
    band0 = band & ((kj >= BLOCK) | has_prev)
    per_kv = nblk // (LANES // d)
    for r in range(tq // BLOCK):
        rows = slice(r * BLOCK, (r + 1) * BLOCK)
        mask = jnp.concatenate([band0 if r == 0 else band] * per_kv, axis=0)
        for kv in range(LANES // d):
            blocks = range(kv * per_kv, (kv + 1) * per_kv)
            q = jnp.concatenate([qr_ref[rows, c * LANES:(c + 1) * LANES] for c in blocks], axis=0)
            out = None
            for half in range(2):
                k = kr_ref[2 * kv + half, r * BLOCK:(r + 2) * BLOCK, :]
                v = vr_ref[2 * kv + half, r * BLOCK:(r + 2) * BLOCK, :]
                sink = jnp.concatenate(
                    [jnp.full((BLOCK, 1), sink_ref[(pair * nblk + c) * 2 + half], F32)
                     for c in blocks], axis=0)
                s = lax.dot_general(q, k, (((1,), (1,)), ((), ())),
                                    preferred_element_type=F32) * scale
                s = jnp.where(mask, s, -jnp.inf)
                m = jnp.maximum(jnp.max(s, axis=-1, keepdims=True), sink)
                p = jnp.exp(s - m)
                l = jnp.sum(p, axis=-1, keepdims=True) + jnp.exp(sink - m)
                o = jnp.dot(p.astype(BF16), v, preferred_element_type=F32) / l
                out = o if out is None else out + o
            for n, c in enumerate(blocks):
                o_ref[rows, c * LANES:(c + 1) * LANES] = (
                    out[n * BLOCK:(n + 1) * BLOCK].astype(o_ref.dtype))


def _swa_attention(qkv, cos, sin, sinks, q_heads, tq_cap=512):
    b, s, _ = qkv.shape
    d = SWA_HEAD_DIM
    kv_heads = q_heads // SWA_GROUP
    per_blk = LANES // d
    assert kv_heads % per_blk == 0
    nq = q_heads * d
    qw = per_blk * SWA_GROUP * d
    kcol = nq // LANES
    vcol = (nq + kv_heads * d) // LANES
    tq = _tile(s, tq_cap)
    rb = tq // BLOCK

    def cur(col0):
        return pl.BlockSpec((None, tq, LANES), lambda bi, p, i: (bi, i, col0 + p))

    def prev(col0):
        return pl.BlockSpec((None, BLOCK, LANES),
                            lambda bi, p, i: (bi, jnp.maximum(i * rb - 1, 0), col0 + p))

    tab_c = pl.BlockSpec((None, tq, LANES), lambda bi, p, i: (bi, i, 0))
    tab_p = pl.BlockSpec((None, BLOCK, LANES), lambda bi, p, i: (bi, jnp.maximum(i * rb - 1, 0), 0))
    kern = functools.partial(_swa_kernel, tq=tq, scale=d ** -0.5)
    return pl.pallas_call(
        kern,
        out_shape=jax.ShapeDtypeStruct((b, s, nq), BF16),
        grid=(b, kv_heads // per_blk, s // tq),
        in_specs=[
            pl.BlockSpec(memory_space=pltpu.SMEM),
            pl.BlockSpec((None, tq, qw), lambda bi, p, i: (bi, i, p)),
            cur(kcol), prev(kcol), cur(vcol), prev(vcol),
            tab_c, tab_p, tab_c, tab_p,
        ],
        out_specs=pl.BlockSpec((None, tq, qw), lambda bi, p, i: (bi, i, p)),
        scratch_shapes=[pltpu.VMEM((tq, qw), BF16),
                        pltpu.VMEM((2 * per_blk, tq + BLOCK, LANES), BF16),
                        pltpu.VMEM((2 * per_blk, tq + BLOCK, LANES), BF16)],
        compiler_params=_params("parallel", "parallel", "parallel"),
        name="swa_attention",
    )(sinks, qkv, qkv, qkv, qkv, qkv, cos, cos, sin, sin)


def _xattn_kernel(x_ref, g_ref, wq_ref, kv_ref, wo_ref, o_ref, a_ref, *, scale):
    x = x_ref[...]
    q = jnp.dot(_rms(x, g_ref[...]).astype(BF16), wq_ref[...],
                preferred_element_type=F32).astype(BF16)
    width = XA_HEADS * XA_HEAD_DIM
    for h in range(XA_HEADS):
        sl = slice(h * XA_HEAD_DIM, (h + 1) * XA_HEAD_DIM)
        k = kv_ref[:, sl]
        v = kv_ref[:, width + h * XA_HEAD_DIM:width + (h + 1) * XA_HEAD_DIM]
        s = lax.dot_general(q[:, sl], k, (((1,), (1,)), ((), ())),
                            preferred_element_type=F32) * scale
        m = jnp.max(s, axis=-1, keepdims=True)
        p = jnp.exp(s - m)
        l = jnp.sum(p, axis=-1, keepdims=True)
        a_ref[:, sl] = (jnp.dot(p.astype(BF16), v, preferred_element_type=F32) / l).astype(BF16)
    o_ref[...] = x + jnp.dot(a_ref[...], wo_ref[...], preferred_element_type=F32)


def _xattn(x, g, w_q, kv, w_o, layer, seq, tm_cap=512):
    m, d = x.shape
    n_mem = kv.shape[1]
    width = XA_HEADS * XA_HEAD_DIM
    tm = _tile(seq, tm_cap, 8)
    per_batch = seq // tm
    return pl.pallas_call(
        functools.partial(_xattn_kernel, scale=XA_HEAD_DIM ** -0.5),
        out_shape=jax.ShapeDtypeStruct((m, d), F32),
        grid=(m // tm,),
        in_specs=[
            pl.BlockSpec((tm, d), lambda i: (i, 0)),
            pl.BlockSpec((1, d), lambda i: (0, 0)),
            pl.BlockSpec((None, d, width), lambda i: (layer, 0, 0)),
            pl.BlockSpec((None, n_mem, 2 * width), lambda i: (i // per_batch, 0, 0)),
            pl.BlockSpec((None, width, d), lambda i: (layer, 0, 0)),
        ],
        out_specs=pl.BlockSpec((tm, d), lambda i: (i, 0)),
        scratch_shapes=[pltpu.VMEM((tm, width), BF16)],
        compiler_params=_params("parallel"),
        name="xattn",
    )(x, g.reshape(1, d), w_q, kv, w_o)


def kernel(x, mem, positions, ffn1_norm, ffn1_w_gate_up, ffn1_w_down, mix_norm, sb_w_qkv, sb_w_o, dil_w_qkv, dil_w_o, swa_w_qkv, swa_b_qkv, swa_sinks, swa_w_o, swa_b_o, xattn_norm, mem_norm, xattn_w_q, xattn_w_kv, xattn_w_o, ffn2_norm, ffn2_w_gate_up, ffn2_w_down, final_norm):
    b, s, d = x.shape
    m = b * s
    depth = ffn1_norm.shape[0]
    n_mem = mem.shape[1]
    sb_heads = d // HEAD_DIM
    dil_hpg = d // (4 * HEAD_DIM)
    dil_heads = dil_hpg * len(DIL_PATTERNS)
    swa_q_heads = d // SWA_HEAD_DIM

    xs = x.reshape(m, d)
    mem2 = mem.reshape(b * n_mem, d)
    pos = positions.reshape(m, 1)
    if depth > 1:
        dil_cos, dil_sin = (t.reshape(b, s, LANES) for t in _rope_tables(pos, HEAD_DIM))
    if depth > 2:
        swa_cos, swa_sin = (t.reshape(b, s, LANES) for t in _rope_tables(pos, SWA_HEAD_DIM))

    (sb_w_qkv, sb_w_o, dil_w_qkv, dil_w_o, swa_w_qkv, swa_w_o, xattn_w_q, xattn_w_kv,
     xattn_w_o) = (w.astype(BF16) for w in (sb_w_qkv, sb_w_o, dil_w_qkv, dil_w_o, swa_w_qkv,
                                            swa_w_o, xattn_w_q, xattn_w_kv, xattn_w_o))
    chain = _ffn_can_convert(m, d, ffn1_w_down.shape[1])

    def cast_ffn(w_gu, w_d, layer):
        return w_gu[layer].astype(BF16), w_d[layer].astype(BF16)

    w1 = cast_ffn(ffn1_w_gate_up, ffn1_w_down, 0)
    for i in range(depth):
        if chain:
            xs, *w2 = _ffn(xs, ffn1_norm[i], *w1, convert=(ffn2_w_gate_up, ffn2_w_down, i))
        else:
            xs = _ffn(xs, ffn1_norm[i], *w1)
            w2 = cast_ffn(ffn2_w_gate_up, ffn2_w_down, i)

        kind, j = i % 3, i // 3
        if kind == 0:
            qkv = _norm_matmul(xs, mix_norm[i], sb_w_qkv, j, None, BF16,
                               col_scale=_sb_q_scale(sb_heads))
            o = _sb_attention(qkv.reshape(b, s, -1), sb_heads)
            xs = _proj_residual(o.reshape(m, -1), sb_w_o, j, xs, None)
        elif kind == 1:
            qkv = _norm_matmul(xs, mix_norm[i], dil_w_qkv, j, None, F32)
            qkv = qkv.reshape(b, s, -1)
            outs, lses = [], []
            for g, (window, dil) in enumerate(DIL_PATTERNS):
                o, lse = _band_attention(qkv, dil_cos, dil_sin, g, dil_hpg, dil_heads, window, dil)
                outs.append(o)
                lses.append(lse)
            xs = _dil_out(outs, lses, dil_w_o, j, xs)
        else:
            qkv = _norm_matmul(xs, mix_norm[i], swa_w_qkv, j, swa_b_qkv[j], BF16)
            o = _swa_attention(qkv.reshape(b, s, -1), swa_cos, swa_sin, swa_sinks[j], swa_q_heads)
            xs = _proj_residual(o.reshape(m, -1), swa_w_o, j, xs, swa_b_o[j])

        kv = _norm_matmul(mem2, mem_norm[i], xattn_w_kv, i, None, BF16)
        xs = _xattn(xs, xattn_norm[i], xattn_w_q, kv.reshape(b, n_mem, -1), xattn_w_o, i, s)

        last = i == depth - 1
        if chain and not last:
            xs, *w1 = _ffn(xs, ffn2_norm[i], *w2, convert=(ffn1_w_gate_up, ffn1_w_down, i + 1))
        else:
            xs = _ffn(xs, ffn2_norm[i], *w2, final_g=final_norm if last else None)
            if not last:
                w1 = cast_ffn(ffn1_w_gate_up, ffn1_w_down, i + 1)

    return xs.reshape(b, s, d)
```

```python
import functools

import jax
import jax.numpy as jnp
from jax import lax
from jax.experimental import pallas as pl
from jax.experimental.pallas import tpu as pltpu

F32 = jnp.float32
BF16 = jnp.bfloat16

NORM_EPS = 1e-6
ROPE_THETA = 500000.0
ROPE_FRACTION = 4
LOG2E = 1.4426950408889634
HEAD_DIM = 128
BLOCK = 128
DIL_PATTERNS = ((128, 1), (512, 4), (2048, 16))
SWA_HEAD_DIM = 64
SWA_GROUP = 8
SWA_WINDOW = 128
XA_HEADS = 4
XA_HEAD_DIM = 128

LANES = 128
V7X_VMEM_BYTES = 64 * 2**20
VMEM_LIMIT_BYTES = (V7X_VMEM_BYTES * 3) // 4
FFN_VMEM_LIMIT_BYTES = (V7X_VMEM_BYTES * 15) // 16


def _params(*semantics, vmem_limit=VMEM_LIMIT_BYTES):
    return pltpu.CompilerParams(dimension_semantics=semantics, vmem_limit_bytes=vmem_limit)


def _tile(n, cap, mult=LANES):
    if n <= cap:
        return n
    t = (cap // mult) * mult
    while t > mult and n % t:
        t -= mult
    assert n % t == 0, (n, cap, mult)
    return t


def _rms(x, g):
    ms = jnp.mean(x * x, axis=-1, keepdims=True)
    return x * lax.rsqrt(ms + NORM_EPS) * g


def _norm_matmul_kernel(x_ref, g_ref, w_ref, b_ref, s_ref, o_ref, xn_ref):
    def tile(first):
        if first:
            xn = _rms(x_ref[...], g_ref[...]).astype(BF16)
            xn_ref[...] = xn
        else:
            xn = xn_ref[...]
        acc = jnp.dot(xn, w_ref[...], preferred_element_type=F32)
        o_ref[...] = ((acc + b_ref[...]) * s_ref[...]).astype(o_ref.dtype)

    pl.when(pl.program_id(1) == 0)(functools.partial(tile, True))
    pl.when(pl.program_id(1) > 0)(functools.partial(tile, False))


def _norm_matmul(x, g, w, layer, bias, out_dtype, col_scale=None, tm_cap=1024):
    m, d = x.shape
    n = w.shape[2]
    tm = _tile(m, tm_cap, 8)
    tn = _tile(n, (9 * 2**20) // (2 * d + jnp.dtype(out_dtype).itemsize * tm) // LANES * LANES)
    if bias is None:
        bias = jnp.zeros((n,), F32)
    if col_scale is None:
        col_scale = jnp.ones((n,), F32)
    return pl.pallas_call(
        _norm_matmul_kernel,
        out_shape=jax.ShapeDtypeStruct((m, n), out_dtype),
        grid=(m // tm, n // tn),
        in_specs=[
            pl.BlockSpec((tm, d), lambda i, j: (i, 0)),
            pl.BlockSpec((1, d), lambda i, j: (0, 0)),
            pl.BlockSpec((None, d, tn), lambda i, j: (layer, 0, j)),
            pl.BlockSpec((1, tn), lambda i, j: (0, j)),
            pl.BlockSpec((1, tn), lambda i, j: (0, j)),
        ],
        out_specs=pl.BlockSpec((tm, tn), lambda i, j: (i, j)),
        scratch_shapes=[pltpu.VMEM((tm, d), BF16)],
        compiler_params=_params("parallel", "arbitrary"),
        name="norm_matmul",
    )(x, g.reshape(1, d), w, bias.reshape(1, n), col_scale.reshape(1, n))


def _ffn_kernel(x_ref, g_ref, wg_ref, wu_ref, wd_ref, gf_ref, *rest, nf, final_norm, convert):
    if convert:
        ngu_ref, nd_ref, o_ref, ngu_out, nd_out, xn_ref = rest
        ngu_out[...] = ngu_ref[...].astype(BF16)
        nd_out[...] = nd_ref[...].astype(BF16)
    else:
        o_ref, xn_ref = rest
    j = pl.program_id(1)

    def chunk(first, last):
        if first:
            xn = _rms(x_ref[...], g_ref[...]).astype(BF16)
            xn_ref[...] = xn
        else:
            xn = xn_ref[...]
        gate = jnp.dot(xn, wg_ref[...], preferred_element_type=F32)
        up = jnp.dot(xn, wu_ref[...], preferred_element_type=F32)
        h = gate * jax.nn.sigmoid(gate) * up
        down = jnp.dot(h.astype(BF16), wd_ref[...], preferred_element_type=F32)
        if not first:
            down = o_ref[...] + down
        o_ref[...] = x_ref[...] + 0.5 * down if last else down

    if final_norm:
        pl.when(j == 0)(functools.partial(chunk, True, False))
        pl.when(j > 0)(functools.partial(chunk, False, False))

        @pl.when(j == nf - 1)
        def _():
            rows = min(256, o_ref.shape[0])

            def finish(c, carry):
                sl = pl.ds(pl.multiple_of(c * rows, rows), rows)
                o_ref[sl, :] = _rms(x_ref[sl, :] + 0.5 * o_ref[sl, :], gf_ref[...])
                return carry

            lax.fori_loop(0, o_ref.shape[0] // rows, finish, 0)
    elif nf == 1:
        chunk(True, True)
    else:
        pl.when(j == 0)(functools.partial(chunk, True, False))
        pl.when((j > 0) & (j < nf - 1))(functools.partial(chunk, False, False))
        pl.when(j == nf - 1)(functools.partial(chunk, False, True))


def _ffn_tiles(m, f, tm_cap=1024, tf_cap=512):
    return _tile(m, tm_cap, 8), _tile(f, tf_cap)


def _ffn_can_convert(m, d, f):
    tm, tf = _ffn_tiles(m, f)
    steps_i = m // tm
    return d % (steps_i * 16) == 0 and tf % (steps_i * 16) == 0


def _ffn(x, g, w_gu, w_d, final_g=None, convert=None):
    m, d = x.shape
    f = w_d.shape[0]
    tm, tf = _ffn_tiles(m, f)
    nf = f // tf
    ni = m // tm
    gf = jnp.ones((d,), F32) if final_g is None else final_g
    in_specs = [
        pl.BlockSpec((tm, d), lambda i, j: (i, 0)),
        pl.BlockSpec((1, d), lambda i, j: (0, 0)),
        pl.BlockSpec((d, tf), lambda i, j: (0, j)),
        pl.BlockSpec((d, tf), lambda i, j: (0, j + nf)),
        pl.BlockSpec((tf, d), lambda i, j: (j, 0)),
        pl.BlockSpec((1, d), lambda i, j: (0, 0)),
    ]
    args = [x, g.reshape(1, d), w_gu, w_gu, w_d, gf.reshape(1, d)]
    out_specs = [pl.BlockSpec((tm, d), lambda i, j: (i, 0))]
    out_shape = [jax.ShapeDtypeStruct((m, d), F32)]
    if convert is not None:
        src_gu, src_d, layer = convert
        gu_blk = (d // ni, 2 * tf)
        d_blk = (tf // ni, d)
        in_specs += [pl.BlockSpec((None,) + gu_blk, lambda i, j: (layer, i, j)),
                     pl.BlockSpec((None,) + d_blk, lambda i, j: (layer, i * nf + j, 0))]
        args += [src_gu, src_d]
        out_specs += [pl.BlockSpec(gu_blk, lambda i, j: (i, j)),
                      pl.BlockSpec(d_blk, lambda i, j: (i * nf + j, 0))]
        out_shape += [jax.ShapeDtypeStruct((d, 2 * f), BF16), jax.ShapeDtypeStruct((f, d), BF16)]
    outs = pl.pallas_call(
        functools.partial(_ffn_kernel, nf=nf, final_norm=final_g is not None,
                          convert=convert is not None),
        out_shape=out_shape,
        grid=(ni, nf),
        in_specs=in_specs,
        out_specs=out_specs,
        scratch_shapes=[pltpu.VMEM((tm, d), BF16)],
        compiler_params=_params("parallel", "arbitrary", vmem_limit=FFN_VMEM_LIMIT_BYTES),
        name="ffn",
    )(*args)
    return outs[0] if convert is None else outs


def _proj_residual_kernel(a_ref, w_ref, x_ref, b_ref, o_ref):
    acc = jnp.dot(a_ref[...], w_ref[...], preferred_element_type=F32)
    o_ref[...] = x_ref[...] + acc + b_ref[...]


def _proj_residual(a, w, layer, x, bias, tm_cap=512, tn_cap=2048):
    m, k = a.shape
    d = w.shape[2]
    tm = _tile(m, tm_cap, 8)
    tn = _tile(d, tn_cap)
    if bias is None:
        bias = jnp.zeros((d,), F32)
    return pl.pallas_call(
        _proj_residual_kernel,
        out_shape=jax.ShapeDtypeStruct((m, d), F32),
        grid=(m // tm, d // tn),
        in_specs=[
            pl.BlockSpec((tm, k), lambda i, j: (i, 0)),
            pl.BlockSpec((None, k, tn), lambda i, j: (layer, 0, j)),
            pl.BlockSpec((tm, tn), lambda i, j: (i, j)),
            pl.BlockSpec((1, tn), lambda i, j: (0, j)),
        ],
        out_specs=pl.BlockSpec((tm, tn), lambda i, j: (i, j)),
        compiler_params=_params("parallel", "arbitrary"),
        name="proj_residual",
    )(a, w, x, bias.reshape(1, d))


def _sb_kernel(q_ref, k_ref, v_ref, tri_ref, o_ref, acc_ref, car_ref, z_ref, *, tq, tk):
    row0 = pl.program_id(2) * tq
    acc_ref[...] = jnp.zeros_like(acc_ref)
    tri = tri_ref[...]
    nsub = tq // tk
    nheads = q_ref.shape[1] // HEAD_DIM
    assert nheads % 2 == 0
    cols = [slice(h * HEAD_DIM, (h + 1) * HEAD_DIM) for h in range(nheads)]

    def logits(h, ks):
        return lax.dot_general(q_ref[:, cols[h]], k_ref[pl.ds(ks, tq), cols[h]],
                               (((1,), (1,)), ((), ())), preferred_element_type=F32)

    def neg_log2_keep(z2):
        neg_abs = lax.bitcast_convert_type(
            lax.bitcast_convert_type(z2, jnp.uint32) | jnp.uint32(0x80000000), F32)
        return jnp.maximum(z2, 0.0) + jnp.log2(1.0 + jnp.exp2(neg_abs))

    def prefetch(h, ks, ks_next):
        if h + 1 < nheads:
            z_ref[(h + 1) % 2] = logits(h + 1, ks)
        else:
            z_ref[0] = logits(0, ks_next)

    def full_tile(ks, ks_next):
        for h, hs in enumerate(cols):
            prefetch(h, ks, ks_next)
            zr = z_ref.at[h % 2]
            nlb = neg_log2_keep(zr[...]).astype(BF16)
            car = car_ref[:, hs]
            parts = [None] * nsub
            for j in reversed(range(nsub)):
                sl = slice(j * tk, (j + 1) * tk)
                ssum = jnp.dot(nlb[:, sl], tri, preferred_element_type=F32)
                parts[j] = jnp.exp2(zr[:, sl] - ssum
                                    - jnp.concatenate([car] * (tk // LANES), axis=1))
                car = car + jnp.broadcast_to(ssum[:, 0:1], (tq, LANES))
            a = jnp.concatenate(parts, axis=1)
            acc_ref[:, hs] += jnp.dot(a.astype(BF16), v_ref[pl.ds(ks, tq), hs],
                                      preferred_element_type=F32)
            car_ref[:, hs] = car

    def diagonal_tile(ks, ks_next):
        lower = (lax.broadcasted_iota(jnp.int32, (tk, tk), 1)
                 < lax.broadcasted_iota(jnp.int32, (tk, tk), 0))
        for h, hs in enumerate(cols):
            prefetch(h, ks, ks_next)
            zr = z_ref.at[h % 2]
            car = jnp.zeros((tq, LANES), F32)
            for j in reversed(range(nsub)):
                r0 = j * tk
                nr = tq - r0
                sl = slice(j * tk, (j + 1) * tk)
                mask = lower if nr == tk else jnp.concatenate(
                    [lower, jnp.ones((nr - tk, tk), jnp.bool_)], axis=0)
                z2 = zr[r0:, sl]
                nl = jnp.where(mask, neg_log2_keep(z2), 0.0)
                ssum = jnp.dot(nl.astype(BF16), tri, preferred_element_type=F32)
                a = jnp.exp2(z2 - ssum - jnp.concatenate([car[r0:]] * (tk // LANES), axis=1))
                a = jnp.where(mask, a, 0.0)
                acc_ref[r0:, hs] += jnp.dot(a.astype(BF16), v_ref[pl.ds(ks + r0, tk), hs],
                                            preferred_element_type=F32)
                upd = car[r0:] + jnp.broadcast_to(ssum[:, 0:1], (nr, LANES))
                car = upd if r0 == 0 else jnp.concatenate([car[:r0], upd], axis=0)
            car_ref[:, hs] = car

    def start(t):
        return pl.multiple_of(jnp.maximum(row0 - (t + 1) * tq, 0), tq)

    z_ref[0] = logits(0, pl.multiple_of(row0, tq))
    diagonal_tile(pl.multiple_of(row0, tq), start(0))

    def body(t, c):
        full_tile(start(t), start(t + 1))
        return c

    lax.fori_loop(0, pl.program_id(2), body, 0)
    o_ref[...] = acc_ref[...].astype(o_ref.dtype)


def _sb_q_scale(heads):
    width = heads * HEAD_DIM
    return jnp.concatenate([jnp.full((width,), HEAD_DIM ** -0.5 * LOG2E, F32),
                            jnp.ones((2 * width,), F32)])


def _sb_attention(qkv, heads, tq_cap=512, tk_cap=256, heads_per_step=4):
    b, s, _ = qkv.shape
    tq = _tile(s, tq_cap)
    tk = _tile(tq, tk_cap)
    hp = heads_per_step if heads % heads_per_step == 0 else 1
    hw = hp * HEAD_DIM
    hb = heads // hp
    tri = jnp.tril(jnp.ones((tk, tk), F32)).astype(BF16)
    kern = functools.partial(_sb_kernel, tq=tq, tk=tk)
    return pl.pallas_call(
        kern,
        out_shape=jax.ShapeDtypeStruct((b, s, heads * HEAD_DIM), BF16),
        grid=(b, hb, s // tq),
        in_specs=[
            pl.BlockSpec((None, tq, hw), lambda bi, h, i: (bi, i, h)),
            pl.BlockSpec((None, s, hw), lambda bi, h, i: (bi, 0, hb + h)),
            pl.BlockSpec((None, s, hw), lambda bi, h, i: (bi, 0, 2 * hb + h)),
            pl.BlockSpec((tk, tk), lambda bi, h, i: (0, 0)),
        ],
        out_specs=pl.BlockSpec((None, tq, hw), lambda bi, h, i: (bi, i, h)),
        scratch_shapes=[pltpu.VMEM((tq, hw), F32), pltpu.VMEM((tq, hw), F32),
                        pltpu.VMEM((2, tq, tq), F32)],
        compiler_params=_params("parallel", "parallel", "arbitrary"),
        name="sb_attention",
    )(qkv, qkv, qkv, tri)


def _rope_table_kernel(pos_ref, freq_ref, sign_ref, cos_ref, sin_ref):
    ang = pos_ref[...].astype(F32) * freq_ref[...]
    cos_ref[...] = jnp.cos(ang)
    sin_ref[...] = jnp.sin(ang) * sign_ref[...]


def _rope_tables(positions, head_dim):
    m = positions.shape[0]
    rot = head_dim // ROPE_FRACTION
    half = rot // 2
    inv_freq = jnp.power(F32(ROPE_THETA), -jnp.arange(half, dtype=F32) * 2.0 / rot)
    lane = jnp.arange(LANES) % head_dim
    freq = jnp.where(lane < rot, inv_freq[lane % half], 0.0).astype(F32)
    sign = jnp.where(lane < half, -1.0, jnp.where(lane < rot, 1.0, 0.0)).astype(F32)
    tm = _tile(m, 2048, 8)
    out = jax.ShapeDtypeStruct((m, LANES), F32)
    return pl.pallas_call(
        _rope_table_kernel,
        out_shape=(out, out),
        grid=(m // tm,),
        in_specs=[
            pl.BlockSpec((tm, 1), lambda i: (i, 0)),
            pl.BlockSpec((1, LANES), lambda i: (0, 0)),
            pl.BlockSpec((1, LANES), lambda i: (0, 0)),
        ],
        out_specs=(pl.BlockSpec((tm, LANES), lambda i: (i, 0)),
                   pl.BlockSpec((tm, LANES), lambda i: (i, 0))),
        compiler_params=_params("parallel"),
        name="rope_tables",
    )(positions, freq.reshape(1, LANES), sign.reshape(1, LANES))


def _rope(x, cos, sin, head_dim):
    half = head_dim // ROPE_FRACTION // 2
    lane = lax.broadcasted_iota(jnp.int32, x.shape, 1) & (head_dim - 1)
    ahead = pltpu.roll(x, LANES - half, 1)
    behind = pltpu.roll(x, half, 1)
    return x * cos + jnp.where(lane < half, ahead, behind) * sin


def _band_kernel(q_ref, k_ref, v_ref, cos_ref, sin_ref, o_ref, lse_ref, qr_ref, kr_ref, vr_ref,
                 *, dil, length, max_dist, scale):
    kj = lax.broadcasted_iota(jnp.int32, (BLOCK, 2 * BLOCK), 1)
    dist = BLOCK + lax.broadcasted_iota(jnp.int32, (BLOCK, 2 * BLOCK), 0) - kj
    band = (dist >= 0) & (dist <= max_dist)
    band0 = band & (kj >= BLOCK)
    nblocks = length // BLOCK
    kr_ref[:, 0:BLOCK, :] = jnp.zeros((kr_ref.shape[0], BLOCK, LANES), BF16)
    vr_ref[:, 0:BLOCK, :] = jnp.zeros((vr_ref.shape[0], BLOCK, LANES), BF16)

    def block(p, slot, a, mask):
        q0 = pl.multiple_of(a * BLOCK, BLOCK)
        q = qr_ref[slot, pl.ds(q0, BLOCK), :]
        k = kr_ref[slot, pl.ds(q0, 2 * BLOCK), :]
        v = vr_ref[slot, pl.ds(q0, 2 * BLOCK), :]
        s = lax.dot_general(q, k, (((1,), (1,)), ((), ())),
                            preferred_element_type=F32) * scale
        s = jnp.where(mask, s, -jnp.inf)
        m = jnp.max(s, axis=-1, keepdims=True)
        e = jnp.exp(s - m)
        l = jnp.sum(e, axis=-1, keepdims=True)
        o = jnp.dot(e.astype(BF16), v, preferred_element_type=F32) / l
        dest = pl.ds(p + q0 * dil, BLOCK, stride=dil)
        o_ref[dest, :] = o
        lse_ref[dest, :] = jnp.broadcast_to(m + jnp.log(l), (BLOCK, LANES))

    ncls = qr_ref.shape[0]
    group = min(4, nblocks)
    assert nblocks % group == 0 and dil % ncls == 0

    def classes(t, c):
        ps = [t * ncls + slot for slot in range(ncls)]
        for slot, p in enumerate(ps):
            rows = pl.ds(p, length, stride=dil)
            cos = cos_ref[rows, :]
            sin = sin_ref[rows, :]
            qr_ref[slot] = _rope(q_ref[rows, :], cos, sin, HEAD_DIM).astype(BF16)
            kr_ref[slot, BLOCK:, :] = _rope(k_ref[rows, :], cos, sin, HEAD_DIM).astype(BF16)
            vr_ref[slot, BLOCK:, :] = v_ref[rows, :].astype(BF16)
        for slot, p in enumerate(ps):
            block(p, slot, 0, band0)
            for a in range(1, group):
                block(p, slot, a, band)

        def trip(u, c2):
            for slot, p in enumerate(ps):
                for a in range(group):
                    block(p, slot, group * u + a, band)
            return c2

        lax.fori_loop(1, nblocks // group, trip, 0)
        return c

    lax.fori_loop(0, dil // ncls, classes, 0)


def _band_attention(qkv, cos, sin, group, hpg, n_heads, window, dil):
    b, s, n = qkv.shape
    length = s // dil
    assert length % BLOCK == 0 and window % dil == 0 and window // dil <= BLOCK
    h0 = group * hpg

    def col(which):
        return pl.BlockSpec((None, s, LANES), lambda bi, h: (bi, 0, which * n_heads + h0 + h))

    tab = pl.BlockSpec((None, s, LANES), lambda bi, h: (bi, 0, 0))
    out = pl.BlockSpec((None, s, LANES), lambda bi, h: (bi, 0, h))
    shape = jax.ShapeDtypeStruct((b, s, hpg * LANES), F32)
    ncls = min(dil, max(1, (4 * BLOCK) // length))
    kern = functools.partial(_band_kernel, dil=dil, length=length, max_dist=window // dil,
                             scale=HEAD_DIM ** -0.5)
    o, lse = pl.pallas_call(
        kern,
        out_shape=(shape, shape),
        grid=(b, hpg),
        in_specs=[col(0), col(1), col(2), tab, tab],
        out_specs=(out, out),
        scratch_shapes=[pltpu.VMEM((ncls, length, LANES), BF16),
                        pltpu.VMEM((ncls, length + BLOCK, LANES), BF16),
                        pltpu.VMEM((ncls, length + BLOCK, LANES), BF16)],
        compiler_params=_params("parallel", "parallel"),
        name=f"band_attention_d{dil}",
    )(qkv, qkv, qkv, cos, sin)
    return (o.reshape(b * s, hpg * LANES), lse.reshape(b * s, hpg * LANES))


def _dil_out_kernel(*refs, groups):
    o_refs = refs[:groups]
    l_refs = refs[groups:2 * groups]
    w_ref, x_ref, out_ref, a_ref = refs[2 * groups:]

    @pl.when(pl.program_id(1) == 0)
    def _():
        lses = [r[...] for r in l_refs]
        mx = functools.reduce(jnp.maximum, lses)
        es = [jnp.exp(l - mx) for l in lses]
        inv = 1.0 / functools.reduce(lambda u, v: u + v, es)
        w = o_refs[0].shape[1]
        for g in range(groups):
            a_ref[:, g * w:(g + 1) * w] = (o_refs[g][...] * (es[g] * inv)).astype(BF16)

    out_ref[...] = x_ref[...] + jnp.dot(a_ref[...], w_ref[...], preferred_element_type=F32)


def _dil_out(outs, lses, w_o, layer, x, tm_cap=512, tn_cap=2048):
    m, d = x.shape
    groups = len(outs)
    gw = outs[0].shape[1]
    k = groups * gw
    tm = _tile(m, tm_cap, 8)
    tn = _tile(d, tn_cap)
    part = pl.BlockSpec((tm, gw), lambda i, j: (i, 0))
    return pl.pallas_call(
        functools.partial(_dil_out_kernel, groups=groups),
        out_shape=jax.ShapeDtypeStruct((m, d), F32),
        grid=(m // tm, d // tn),
        in_specs=[part] * (2 * groups) + [
            pl.BlockSpec((None, k, tn), lambda i, j: (layer, 0, j)),
            pl.BlockSpec((tm, tn), lambda i, j: (i, j)),
        ],
        out_specs=pl.BlockSpec((tm, tn), lambda i, j: (i, j)),
        scratch_shapes=[pltpu.VMEM((tm, k), BF16)],
        compiler_params=_params("parallel", "arbitrary"),
        name="dil_out",
    )(*outs, *lses, w_o, x)


def _swa_kernel(sink_ref, q_ref, kc_ref, kp_ref, vc_ref, vp_ref, cc_ref, cp_ref, sc_ref, sp_ref,
                o_ref, qr_ref, kr_ref, vr_ref, *, tq, scale):
    pair = pl.program_id(1)
    has_prev = pl.program_id(2) > 0
    d = SWA_HEAD_DIM
    cos = cc_ref[...]
    sin = sc_ref[...]
    kx = jnp.concatenate([_rope(kp_ref[...].astype(F32), cp_ref[...], sp_ref[...], d),
                          _rope(kc_ref[...].astype(F32), cos, sin, d)], axis=0)
    vx = jnp.concatenate([vp_ref[...], vc_ref[...]], axis=0).astype(F32)
    low = lax.broadcasted_iota(jnp.int32, kx.shape, 1) < d
    for src, dst in ((kx, kr_ref), (vx, vr_ref)):
        swapped = pltpu.roll(src, d, 1)
        dst[0] = jnp.where(low, src, 0.0).astype(BF16)
        dst[1] = jnp.where(low, 0.0, swapped).astype(BF16)
        dst[2] = jnp.where(low, swapped, 0.0).astype(BF16)
        dst[3] = jnp.where(low, 0.0, src).astype(BF16)
    nblk = q_ref.shape[1] // LANES
    for c in range(nblk):
        sl = slice(c * LANES, (c + 1) * LANES)
        qr_ref[:, sl] = _rope(q_ref[:, sl].astype(F32), cos, sin, d).astype(BF16)

    qi = lax.broadcasted_iota(jnp.int32, (BLOCK, 2 * BLOCK), 0)
    kj = lax.broadcasted_iota(jnp.int32, (BLOCK, 2 * BLOCK), 1)
    dist = BLOCK + qi - kj
    band = (dist >= 0) & (dist <= SWA_WINDOW - 1)
    band0 = band & ((kj >= BLOCK) | has_prev)
    per_kv = nblk // (LANES // d)
    for r in range(tq // BLOCK):
        rows = slice(r * BLOCK, (r + 1) * BLOCK)
        mask = jnp.concatenate([band0 if r == 0 else band] * per_kv, axis=0)
        for kv in range(LANES // d):
            blocks = range(kv * per_kv, (kv + 1) * per_kv)
            q = jnp.concatenate([qr_ref[rows, c * LANES:(c + 1) * LANES] for c in blocks], axis=0)
            out = None
            for half in range(2):
                k = kr_ref[2 * kv + half, r * BLOCK:(r + 2) * BLOCK, :]
                v = vr_ref[2 * kv + half, r * BLOCK:(r + 2) * BLOCK, :]
                sink = jnp.concatenate(
                    [jnp.full((BLOCK, 1), sink_ref[(pair * nblk + c) * 2 + half], F32)
                     for c in blocks], axis=0)
                s = lax.dot_general(q, k, (((1,), (1,)), ((), ())),
                                    preferred_element_type=F32) * scale
                s = jnp.where(mask, s, -jnp.inf)
                m = jnp.maximum(jnp.max(s, axis=-1, keepdims=True), sink)
                p = jnp.exp(s - m)
                l = jnp.sum(p, axis=-1, keepdims=True) + jnp.exp(sink - m)
                o = jnp.dot(p.astype(BF16), v, preferred_element_type=F32) / l
                out = o if out is None else out + o
            for n, c in enumerate(blocks):
                o_ref[rows, c * LANES:(c + 1) * LANES] = (
                    out[n * BLOCK:(n + 1) * BLOCK].astype(o_ref.dtype))


def _swa_attention(qkv, cos, sin, sinks, q_heads, tq_cap=512):
    b, s, _ = qkv.shape
    d = SWA_HEAD_DIM
    kv_heads = q_heads // SWA_GROUP
    per_blk = LANES // d
    assert kv_heads % per_blk == 0
    nq = q_heads * d
    qw = per_blk * SWA_GROUP * d
    kcol = nq // LANES
    vcol = (nq + kv_heads * d) // LANES
    tq = _tile(s, tq_cap)
    rb = tq // BLOCK

    def cur(col0):
        return pl.BlockSpec((None, tq, LANES), lambda bi, p, i: (bi, i, col0 + p))

    def prev(col0):
        return pl.BlockSpec((None, BLOCK, LANES),
                            lambda bi, p, i: (bi, jnp.maximum(i * rb - 1, 0), col0 + p))

    tab_c = pl.BlockSpec((None, tq, LANES), lambda bi, p, i: (bi, i, 0))
    tab_p = pl.BlockSpec((None, BLOCK, LANES), lambda bi, p, i: (bi, jnp.maximum(i * rb - 1, 0), 0))
    kern = functools.partial(_swa_kernel, tq=tq, scale=d ** -0.5)
    return pl.pallas_call(
        kern,
        out_shape=jax.ShapeDtypeStruct((b, s, nq), BF16),
        grid=(b, kv_heads // per_blk, s // tq),
        in_specs=[
            pl.BlockSpec(memory_space=pltpu.SMEM),
            pl.BlockSpec((None, tq, qw), lambda bi, p, i: (bi, i, p)),
            cur(kcol), prev(kcol), cur(vcol), prev(vcol),
            tab_c, tab_p, tab_c, tab_p,
        ],
        out_specs=pl.BlockSpec((None, tq, qw), lambda bi, p, i: (bi, i, p)),
        scratch_shapes=[pltpu.VMEM((tq, qw), BF16),
                        pltpu.VMEM((2 * per_blk, tq + BLOCK, LANES), BF16),
                        pltpu.VMEM((2 * per_blk, tq + BLOCK, LANES), BF16)],
        compiler_params=_params("parallel", "parallel", "parallel"),
        name="swa_attention",
    )(sinks, qkv, qkv, qkv, qkv, qkv, cos, cos, sin, sin)


def _xattn_kernel(x_ref, g_ref, wq_ref, kv_ref, wo_ref, o_ref, a_ref, *, scale):
    x = x_ref[...]
    q = jnp.dot(_rms(x, g_ref[...]).astype(BF16), wq_ref[...],
                preferred_element_type=F32).astype(BF16)
    width = XA_HEADS * XA_HEAD_DIM
    for h in range(XA_HEADS):
        sl = slice(h * XA_HEAD_DIM, (h + 1) * XA_HEAD_DIM)
        k = kv_ref[:, sl]
        v = kv_ref[:, width + h * XA_HEAD_DIM:width + (h + 1) * XA_HEAD_DIM]
        s = lax.dot_general(q[:, sl], k, (((1,), (1,)), ((), ())),
                            preferred_element_type=F32) * scale
        m = jnp.max(s, axis=-1, keepdims=True)
        p = jnp.exp(s - m)
        l = jnp.sum(p, axis=-1, keepdims=True)
        a_ref[:, sl] = (jnp.dot(p.astype(BF16), v, preferred_element_type=F32) / l).astype(BF16)
    o_ref[...] = x + jnp.dot(a_ref[...], wo_ref[...], preferred_element_type=F32)


def _xattn(x, g, w_q, kv, w_o, layer, seq, tm_cap=512):
    m, d = x.shape
    n_mem = kv.shape[1]
    width = XA_HEADS * XA_HEAD_DIM
    tm = _tile(seq, tm_cap, 8)
    per_batch = seq // tm
    return pl.pallas_call(
        functools.partial(_xattn_kernel, scale=XA_HEAD_DIM ** -0.5),
        out_shape=jax.ShapeDtypeStruct((m, d), F32),
        grid=(m // tm,),
        in_specs=[
            pl.BlockSpec((tm, d), lambda i: (i, 0)),
            pl.BlockSpec((1, d), lambda i: (0, 0)),
            pl.BlockSpec((None, d, width), lambda i: (layer, 0, 0)),
            pl.BlockSpec((None, n_mem, 2 * width), lambda i: (i // per_batch, 0, 0)),
            pl.BlockSpec((None, width, d), lambda i: (layer, 0, 0)),
        ],
        out_specs=pl.BlockSpec((tm, d), lambda i: (i, 0)),
        scratch_shapes=[pltpu.VMEM((tm, width), BF16)],
        compiler_params=_params("parallel"),
        name="xattn",
    )(x, g.reshape(1, d), w_q, kv, w_o)


def kernel(x, mem, positions, ffn1_norm, ffn1_w_gate_up, ffn1_w_down, mix_norm, sb_w_qkv, sb_w_o, dil_w_qkv, dil_w_o, swa_w_qkv, swa_b_qkv, swa_sinks, swa_w_o, swa_b_o, xattn_norm, mem_norm, xattn_w_q, xattn_w_kv, xattn_w_o, ffn2_norm, ffn2_w_gate_up, ffn2_w_down, final_norm):
    b, s, d = x.shape
    m = b * s
    depth = ffn1_norm.shape[0]
    n_mem = mem.shape[1]
    sb_heads = d // HEAD_DIM
    dil_hpg = d // (4 * HEAD_DIM)
    dil_heads = dil_hpg * len(DIL_PATTERNS)
    swa_q_heads = d // SWA_HEAD_DIM

    xs = x.reshape(m, d)
    mem2 = mem.reshape(b * n_mem, d)
    pos = positions.reshape(m, 1)
    if depth > 1:
        dil_cos, dil_sin = (t.reshape(b, s, LANES) for t in _rope_tables(pos, HEAD_DIM))
    if depth > 2:
        swa_cos, swa_sin = (t.reshape(b, s, LANES) for t in _rope_tables(pos, SWA_HEAD_DIM))

    (sb_w_qkv, sb_w_o, dil_w_qkv, dil_w_o, swa_w_qkv, swa_w_o, xattn_w_q, xattn_w_kv,
     xattn_w_o) = (w.astype(BF16) for w in (sb_w_qkv, sb_w_o, dil_w_qkv, dil_w_o, swa_w_qkv,
                                            swa_w_o, xattn_w_q, xattn_w_kv, xattn_w_o))
    chain = _ffn_can_convert(m, d, ffn1_w_down.shape[1])

    def cast_ffn(w_gu, w_d, layer):
        return w_gu[layer].astype(BF16), w_d[layer].astype(BF16)

    w1 = cast_ffn(ffn1_w_gate_up, ffn1_w_down, 0)
    for i in range(depth):
        if chain:
            xs, *w2 = _ffn(xs, ffn1_norm[i], *w1, convert=(ffn2_w_gate_up, ffn2_w_down, i))
        else:
            xs = _ffn(xs, ffn1_norm[i], *w1)
            w2 = cast_ffn(ffn2_w_gate_up, ffn2_w_down, i)

        kind, j = i % 3, i // 3
        if kind == 0:
            qkv = _norm_matmul(xs, mix_norm[i], sb_w_qkv, j, None, BF16,
                               col_scale=_sb_q_scale(sb_heads))
            o = _sb_attention(qkv.reshape(b, s, -1), sb_heads)
            xs = _proj_residual(o.reshape(m, -1), sb_w_o, j, xs, None)
        elif kind == 1:
            qkv = _norm_matmul(xs, mix_norm[i], dil_w_qkv, j, None, F32)
            qkv = qkv.reshape(b, s, -1)
            outs, lses = [], []
            for g, (window, dil) in enumerate(DIL_PATTERNS):
                o, lse = _band_attention(qkv, dil_cos, dil_sin, g, dil_hpg, dil_heads, window, dil)
                outs.append(o)
                lses.append(lse)
            xs = _dil_out(outs, lses, dil_w_o, j, xs)
        else:
            qkv = _norm_matmul(xs, mix_norm[i], swa_w_qkv, j, swa_b_qkv[j], BF16)
            o = _swa_attention(qkv.reshape(b, s, -1), swa_cos, swa_sin, swa_sinks[j], swa_q_heads)
            xs = _proj_residual(o.reshape(m, -1), swa_w_o, j, xs, swa_b_o[j])

        kv = _norm_matmul(mem2, mem_norm[i], xattn_w_kv, i, None, BF16)
        xs = _xattn(xs, xattn_norm[i], xattn_w_q, kv.reshape(b, n_mem, -1), xattn_w_o, i, s)

        last = i == depth - 1
        if chain and not last:
            xs, *w1 = _ffn(xs, ffn2_norm[i], *w2, convert=(ffn1_w_gate_up, ffn1_w_down, i + 1))
        else:
            xs = _ffn(xs, ffn2_norm[i], *w2, final_g=final_norm if last else None)
            if not last:
                w1 = cast_ffn(ffn1_w_gate_up, ffn1_w_down, i + 1)

    return xs.reshape(b, s, d)
```

```python
import functools

import jax
import jax.numpy as jnp
from jax import lax
from jax.experimental import pallas as pl
from jax.experimental.pallas import tpu as pltpu

F32 = jnp.float32
BF16 = jnp.bfloat16

NORM_EPS = 1e-6
ROPE_THETA = 500000.0
ROPE_FRACTION = 4
LOG2E = 1.4426950408889634
HEAD_DIM = 128
BLOCK = 128
DIL_PATTERNS = ((128, 1), (512, 4), (2048, 16))
SWA_HEAD_DIM = 64
SWA_GROUP = 8
SWA_WINDOW = 128
XA_HEADS = 4
XA_HEAD_DIM = 128

LANES = 128
V7X_VMEM_BYTES = 64 * 2**20
VMEM_LIMIT_BYTES = (V7X_VMEM_BYTES * 3) // 4
FFN_VMEM_LIMIT_BYTES = (V7X_VMEM_BYTES * 15) // 16


def _params(*semantics, vmem_limit=VMEM_LIMIT_BYTES):
    return pltpu.CompilerParams(dimension_semantics=semantics, vmem_limit_bytes=vmem_limit)


def _tile(n, cap, mult=LANES):
    if n <= cap:
        return n
    t = (cap // mult) * mult
    while t > mult and n % t:
        t -= mult
    assert n % t == 0, (n, cap, mult)
    return t


def _rms(x, g):
    ms = jnp.mean(x * x, axis=-1, keepdims=True)
    return x * lax.rsqrt(ms + NORM_EPS) * g


def _norm_matmul_kernel(x_ref, g_ref, w_ref, b_ref, s_ref, o_ref, xn_ref):
    def tile(first):
        if first:
            xn = _rms(x_ref[...], g_ref[...]).astype(BF16)
            xn_ref[...] = xn
        else:
            xn = xn_ref[...]
        acc = jnp.dot(xn, w_ref[...], preferred_element_type=F32)
        o_ref[...] = ((acc + b_ref[...]) * s_ref[...]).astype(o_ref.dtype)

    pl.when(pl.program_id(1) == 0)(functools.partial(tile, True))
    pl.when(pl.program_id(1) > 0)(functools.partial(tile, False))


def _norm_matmul(x, g, w, layer, bias, out_dtype, col_scale=None, tm_cap=1024):
    m, d = x.shape
    n = w.shape[2]
    tm = _tile(m, tm_cap, 8)
    tn = _tile(n, (9 * 2**20) // (2 * d + jnp.dtype(out_dtype).itemsize * tm) // LANES * LANES)
    if bias is None:
        bias = jnp.zeros((n,), F32)
    if col_scale is None:
        col_scale = jnp.ones((n,), F32)
    return pl.pallas_call(
        _norm_matmul_kernel,
        out_shape=jax.ShapeDtypeStruct((m, n), out_dtype),
        grid=(m // tm, n // tn),
        in_specs=[
            pl.BlockSpec((tm, d), lambda i, j: (i, 0)),
            pl.BlockSpec((1, d), lambda i, j: (0, 0)),
            pl.BlockSpec((None, d, tn), lambda i, j: (layer, 0, j)),
            pl.BlockSpec((1, tn), lambda i, j: (0, j)),
            pl.BlockSpec((1, tn), lambda i, j: (0, j)),
        ],
        out_specs=pl.BlockSpec((tm, tn), lambda i, j: (i, j)),
        scratch_shapes=[pltpu.VMEM((tm, d), BF16)],
        compiler_params=_params("parallel", "arbitrary"),
        name="norm_matmul",
    )(x, g.reshape(1, d), w, bias.reshape(1, n), col_scale.reshape(1, n))


def _ffn_kernel(x_ref, g_ref, wg_ref, wu_ref, wd_ref, gf_ref, *rest, nf, final_norm, convert):
    if convert:
        ngu_ref, nd_ref, o_ref, ngu_out, nd_out, xn_ref = rest
        ngu_out[...] = ngu_ref[...].astype(BF16)
        nd_out[...] = nd_ref[...].astype(BF16)
    else:
        o_ref, xn_ref = rest
    j = pl.program_id(1)

    def chunk(first, last):
        if first:
            xn = _rms(x_ref[...], g_ref[...]).astype(BF16)
            xn_ref[...] = xn
        else:
            xn = xn_ref[...]
        gate = jnp.dot(xn, wg_ref[...], preferred_element_type=F32)
        up = jnp.dot(xn, wu_ref[...], preferred_element_type=F32)
        h = gate * jax.nn.sigmoid(gate) * up
        down = jnp.dot(h.astype(BF16), wd_ref[...], preferred_element_type=F32)
        if not first:
            down = o_ref[...] + down
        o_ref[...] = x_ref[...] + 0.5 * down if last else down

    if final_norm:
        pl.when(j == 0)(functools.partial(chunk, True, False))
        pl.when(j > 0)(functools.partial(chunk, False, False))

        @pl.when(j == nf - 1)
        def _():
            rows = min(256, o_ref.shape[0])

            def finish(c, carry):
                sl = pl.ds(pl.multiple_of(c * rows, rows), rows)
                o_ref[sl, :] = _rms(x_ref[sl, :] + 0.5 * o_ref[sl, :], gf_ref[...])
                return carry

            lax.fori_loop(0, o_ref.shape[0] // rows, finish, 0)
    elif nf == 1:
        chunk(True, True)
    else:
        pl.when(j == 0)(functools.partial(chunk, True, False))
        pl.when((j > 0) & (j < nf - 1))(functools.partial(chunk, False, False))
        pl.when(j == nf - 1)(functools.partial(chunk, False, True))


def _ffn_tiles(m, f, tm_cap=1024, tf_cap=512):
    return _tile(m, tm_cap, 8), _tile(f, tf_cap)


def _ffn_can_convert(m, d, f):
    tm, tf = _ffn_tiles(m, f)
    steps_i = m // tm
    return d % (steps_i * 16) == 0 and tf % (steps_i * 16) == 0


def _ffn(x, g, w_gu, w_d, final_g=None, convert=None):
    m, d = x.shape
    f = w_d.shape[0]
    tm, tf = _ffn_tiles(m, f)
    nf = f // tf
    ni = m // tm
    gf = jnp.ones((d,), F32) if final_g is None else final_g
    in_specs = [
        pl.BlockSpec((tm, d), lambda i, j: (i, 0)),
        pl.BlockSpec((1, d), lambda i, j: (0, 0)),
        pl.BlockSpec((d, tf), lambda i, j: (0, j)),
        pl.BlockSpec((d, tf), lambda i, j: (0, j + nf)),
        pl.BlockSpec((tf, d), lambda i, j: (j, 0)),
        pl.BlockSpec((1, d), lambda i, j: (0, 0)),
    ]
    args = [x, g.reshape(1, d), w_gu, w_gu, w_d, gf.reshape(1, d)]
    out_specs = [pl.BlockSpec((tm, d), lambda i, j: (i, 0))]
    out_shape = [jax.ShapeDtypeStruct((m, d), F32)]
    if convert is not None:
        src_gu, src_d, layer = convert
        gu_blk = (d // ni, 2 * tf)
        d_blk = (tf // ni, d)
        in_specs += [pl.BlockSpec((None,) + gu_blk, lambda i, j: (layer, i, j)),
                     pl.BlockSpec((None,) + d_blk, lambda i, j: (layer, i * nf + j, 0))]
        args += [src_gu, src_d]
        out_specs += [pl.BlockSpec(gu_blk, lambda i, j: (i, j)),
                      pl.BlockSpec(d_blk, lambda i, j: (i * nf + j, 0))]
        out_shape += [jax.ShapeDtypeStruct((d, 2 * f), BF16), jax.ShapeDtypeStruct((f, d), BF16)]
    outs = pl.pallas_call(
        functools.partial(_ffn_kernel, nf=nf, final_norm=final_g is not None,
                          convert=convert is not None),
        out_shape=out_shape,
        grid=(ni, nf),
        in_specs=in_specs,
        out_specs=out_specs,
        scratch_shapes=[pltpu.VMEM((tm, d), BF16)],
        compiler_params=_params("parallel", "arbitrary", vmem_limit=FFN_VMEM_LIMIT_BYTES),
        name="ffn",
    )(*args)
    return outs[0] if convert is None else outs


def _proj_residual_kernel(a_ref, w_ref, x_ref, b_ref, o_ref):
    acc = jnp.dot(a_ref[...], w_ref[...], preferred_element_type=F32)
    o_ref[...] = x_ref[...] + acc + b_ref[...]


def _proj_residual(a, w, layer, x, bias, tm_cap=512, tn_cap=2048):
    m, k = a.shape
    d = w.shape[2]
    tm = _tile(m, tm_cap, 8)
    tn = _tile(d, tn_cap)
    if bias is None:
        bias = jnp.zeros((d,), F32)
    return pl.pallas_call(
        _proj_residual_kernel,
        out_shape=jax.ShapeDtypeStruct((m, d), F32),
        grid=(m // tm, d // tn),
        in_specs=[
            pl.BlockSpec((tm, k), lambda i, j: (i, 0)),
            pl.BlockSpec((None, k, tn), lambda i, j: (layer, 0, j)),
            pl.BlockSpec((tm, tn), lambda i, j: (i, j)),
            pl.BlockSpec((1, tn), lambda i, j: (0, j)),
        ],
        out_specs=pl.BlockSpec((tm, tn), lambda i, j: (i, j)),
        compiler_params=_params("parallel", "arbitrary"),
        name="proj_residual",
    )(a, w, x, bias.reshape(1, d))


def _sb_kernel(q_ref, k_ref, v_ref, tri_ref, o_ref, acc_ref, car_ref, z_ref, *, tq, tk):
    row0 = pl.program_id(2) * tq
    acc_ref[...] = jnp.zeros_like(acc_ref)
    tri = tri_ref[...]
    nsub = tq // tk
    nheads = q_ref.shape[1] // HEAD_DIM
    assert nheads % 2 == 0
    cols = [slice(h * HEAD_DIM, (h + 1) * HEAD_DIM) for h in range(nheads)]

    def logits(h, ks):
        return lax.dot_general(q_ref[:, cols[h]], k_ref[pl.ds(ks, tq), cols[h]],
                               (((1,), (1,)), ((), ())), preferred_element_type=F32)

    def neg_log2_keep(z2):
        neg_abs = lax.bitcast_convert_type(
            lax.bitcast_convert_type(z2, jnp.uint32) | jnp.uint32(0x80000000), F32)
        return jnp.maximum(z2, 0.0) + jnp.log2(1.0 + jnp.exp2(neg_abs))

    def prefetch(h, ks, ks_next):
        if h + 1 < nheads:
            z_ref[(h + 1) % 2] = logits(h + 1, ks)
        else:
            z_ref[0] = logits(0, ks_next)

    def full_tile(ks, ks_next):
        for h, hs in enumerate(cols):
            prefetch(h, ks, ks_next)
            zr = z_ref.at[h % 2]
            nlb = neg_log2_keep(zr[...]).astype(BF16)
            car = car_ref[:, hs]
            parts = [None] * nsub
            for j in reversed(range(nsub)):
                sl = slice(j * tk, (j + 1) * tk)
                ssum = jnp.dot(nlb[:, sl], tri, preferred_element_type=F32)
                parts[j] = jnp.exp2(zr[:, sl] - ssum
                                    - jnp.concatenate([car] * (tk // LANES), axis=1))
                car = car + jnp.broadcast_to(ssum[:, 0:1], (tq, LANES))
            a = jnp.concatenate(parts, axis=1)
            acc_ref[:, hs] += jnp.dot(a.astype(BF16), v_ref[pl.ds(ks, tq), hs],
                                      preferred_element_type=F32)
            car_ref[:, hs] = car

    def diagonal_tile(ks, ks_next):
        lower = (lax.broadcasted_iota(jnp.int32, (tk, tk), 1)
                 < lax.broadcasted_iota(jnp.int32, (tk, tk), 0))
        for h, hs in enumerate(cols):
            prefetch(h, ks, ks_next)
            zr = z_ref.at[h % 2]
            car = jnp.zeros((tq, LANES), F32)
            for j in reversed(range(nsub)):
                r0 = j * tk
                nr = tq - r0
                sl = slice(j * tk, (j + 1) * tk)
                mask = lower if nr == tk else jnp.concatenate(
                    [lower, jnp.ones((nr - tk, tk), jnp.bool_)], axis=0)
                z2 = zr[r0:, sl]
                nl = jnp.where(mask, neg_log2_keep(z2), 0.0)
                ssum = jnp.dot(nl.astype(BF16), tri, preferred_element_type=F32)
                a = jnp.exp2(z2 - ssum - jnp.concatenate([car[r0:]] * (tk // LANES), axis=1))
                a = jnp.where(mask, a, 0.0)
                acc_ref[r0:, hs] += jnp.dot(a.astype(BF16), v_ref[pl.ds(ks + r0, tk), hs],
                                            preferred_element_type=F32)
                upd = car[r0:] + jnp.broadcast_to(ssum[:, 0:1], (nr, LANES))
                car = upd if r0 == 0 else jnp.concatenate([car[:r0], upd], axis=0)
            car_ref[:, hs] = car

    def start(t):
        return pl.multiple_of(jnp.maximum(row0 - (t + 1) * tq, 0), tq)

    z_ref[0] = logits(0, pl.multiple_of(row0, tq))
    diagonal_tile(pl.multiple_of(row0, tq), start(0))

    def body(t, c):
        full_tile(start(t), start(t + 1))
        return c

    lax.fori_loop(0, pl.program_id(2), body, 0)
    o_ref[...] = acc_ref[...].astype(o_ref.dtype)


def _sb_q_scale(heads):
    width = heads * HEAD_DIM
    return jnp.concatenate([jnp.full((width,), HEAD_DIM ** -0.5 * LOG2E, F32),
                            jnp.ones((2 * width,), F32)])


def _sb_attention(qkv, heads, tq_cap=512, tk_cap=256, heads_per_step=4):
    b, s, _ = qkv.shape
    tq = _tile(s, tq_cap)
    tk = _tile(tq, tk_cap)
    hp = heads_per_step if heads % heads_per_step == 0 else 1
    hw = hp * HEAD_DIM
    hb = heads // hp
    tri = jnp.tril(jnp.ones((tk, tk), F32)).astype(BF16)
    kern = functools.partial(_sb_kernel, tq=tq, tk=tk)
    return pl.pallas_call(
        kern,
        out_shape=jax.ShapeDtypeStruct((b, s, heads * HEAD_DIM), BF16),
        grid=(b, hb, s // tq),
        in_specs=[
            pl.BlockSpec((None, tq, hw), lambda bi, h, i: (bi, i, h)),
            pl.BlockSpec((None, s, hw), lambda bi, h, i: (bi, 0, hb + h)),
            pl.BlockSpec((None, s, hw), lambda bi, h, i: (bi, 0, 2 * hb + h)),
            pl.BlockSpec((tk, tk), lambda bi, h, i: (0, 0)),
        ],
        out_specs=pl.BlockSpec((None, tq, hw), lambda bi, h, i: (bi, i, h)),
        scratch_shapes=[pltpu.VMEM((tq, hw), F32), pltpu.VMEM((tq, hw), F32),
                        pltpu.VMEM((2, tq, tq), F32)],
        compiler_params=_params("parallel", "parallel", "arbitrary"),
        name="sb_attention",
    )(qkv, qkv, qkv, tri)


def _rope_table_kernel(pos_ref, freq_ref, sign_ref, cos_ref, sin_ref):
    ang = pos_ref[...].astype(F32) * freq_ref[...]
    cos_ref[...] = jnp.cos(ang)
    sin_ref[...] = jnp.sin(ang) * sign_ref[...]


def _rope_tables(positions, head_dim):
    m = positions.shape[0]
    rot = head_dim // ROPE_FRACTION
    half = rot // 2
    inv_freq = jnp.power(F32(ROPE_THETA), -jnp.arange(half, dtype=F32) * 2.0 / rot)
    lane = jnp.arange(LANES) % head_dim
    freq = jnp.where(lane < rot, inv_freq[lane % half], 0.0).astype(F32)
    sign = jnp.where(lane < half, -1.0, jnp.where(lane < rot, 1.0, 0.0)).astype(F32)
    tm = _tile(m, 2048, 8)
    out = jax.ShapeDtypeStruct((m, LANES), F32)
    return pl.pallas_call(
        _rope_table_kernel,
        out_shape=(out, out),
        grid=(m // tm,),
        in_specs=[
            pl.BlockSpec((tm, 1), lambda i: (i, 0)),
            pl.BlockSpec((1, LANES), lambda i: (0, 0)),
            pl.BlockSpec((1, LANES), lambda i: (0, 0)),
        ],
        out_specs=(pl.BlockSpec((tm, LANES), lambda i: (i, 0)),
                   pl.BlockSpec((tm, LANES), lambda i: (i, 0))),
        compiler_params=_params("parallel"),
        name="rope_tables",
    )(positions, freq.reshape(1, LANES), sign.reshape(1, LANES))


def _rope(x, cos, sin, head_dim):
    half = head_dim // ROPE_FRACTION // 2
    lane = lax.broadcasted_iota(jnp.int32, x.shape, 1) & (head_dim - 1)
    ahead = pltpu.roll(x, LANES - half, 1)
    behind = pltpu.roll(x, half, 1)
    return x * cos + jnp.where(lane < half, ahead, behind) * sin


def _band_kernel(q_ref, k_ref, v_ref, cos_ref, sin_ref, o_ref, lse_ref, qr_ref, kr_ref, vr_ref,
                 *, dil, length, max_dist, scale):
    kj = lax.broadcasted_iota(jnp.int32, (BLOCK, 2 * BLOCK), 1)
    dist = BLOCK + lax.broadcasted_iota(jnp.int32, (BLOCK, 2 * BLOCK), 0) - kj
    band = (dist >= 0) & (dist <= max_dist)
    band0 = band & (kj >= BLOCK)
    nblocks = length // BLOCK
    kr_ref[:, 0:BLOCK, :] = jnp.zeros((kr_ref.shape[0], BLOCK, LANES), BF16)
    vr_ref[:, 0:BLOCK, :] = jnp.zeros((vr_ref.shape[0], BLOCK, LANES), BF16)

    def block(p, slot, a, mask):
        q0 = pl.multiple_of(a * BLOCK, BLOCK)
        q = qr_ref[slot, pl.ds(q0, BLOCK), :]
        k = kr_ref[slot, pl.ds(q0, 2 * BLOCK), :]
        v = vr_ref[slot, pl.ds(q0, 2 * BLOCK), :]
        s = lax.dot_general(q, k, (((1,), (1,)), ((), ())),
                            preferred_element_type=F32) * scale
        s = jnp.where(mask, s, -jnp.inf)
        m = jnp.max(s, axis=-1, keepdims=True)
        e = jnp.exp(s - m)
        l = jnp.sum(e, axis=-1, keepdims=True)
        o = jnp.dot(e.astype(BF16), v, preferred_element_type=F32) / l
        dest = pl.ds(p + q0 * dil, BLOCK, stride=dil)
        o_ref[dest, :] = o
        lse_ref[dest, :] = jnp.broadcast_to(m + jnp.log(l), (BLOCK, LANES))

    ncls = qr_ref.shape[0]
    group = min(4, nblocks)
    assert nblocks % group == 0 and dil % ncls == 0

    def classes(t, c):
        ps = [t * ncls + slot for slot in range(ncls)]
        for slot, p in enumerate(ps):
            rows = pl.ds(p, length, stride=dil)
            cos = cos_ref[rows, :]
            sin = sin_ref[rows, :]
            qr_ref[slot] = _rope(q_ref[rows, :], cos, sin, HEAD_DIM).astype(BF16)
            kr_ref[slot, BLOCK:, :] = _rope(k_ref[rows, :], cos, sin, HEAD_DIM).astype(BF16)
            vr_ref[slot, BLOCK:, :] = v_ref[rows, :].astype(BF16)
        for slot, p in enumerate(ps):
            block(p, slot, 0, band0)
            for a in range(1, group):
                block(p, slot, a, band)

        def trip(u, c2):
            for slot, p in enumerate(ps):
                for a in range(group):
                    block(p, slot, group * u + a, band)
            return c2

        lax.fori_loop(1, nblocks // group, trip, 0)
        return c

    lax.fori_loop(0, dil // ncls, classes, 0)


def _band_attention(qkv, cos, sin, group, hpg, n_heads, window, dil):
    b, s, n = qkv.shape
    length = s // dil
    assert length % BLOCK == 0 and window % dil == 0 and window // dil <= BLOCK
    h0 = group * hpg

    def col(which):
        return pl.BlockSpec((None, s, LANES), lambda bi, h: (bi, 0, which * n_heads + h0 + h))

    tab = pl.BlockSpec((None, s, LANES), lambda bi, h: (bi, 0, 0))
    out = pl.BlockSpec((None, s, LANES), lambda bi, h: (bi, 0, h))
    shape = jax.ShapeDtypeStruct((b, s, hpg * LANES), F32)
    ncls = min(dil, max(1, (4 * BLOCK) // length))
    kern = functools.partial(_band_kernel, dil=dil, length=length, max_dist=window // dil,
                             scale=HEAD_DIM ** -0.5)
    o, lse = pl.pallas_call(
        kern,
        out_shape=(shape, shape),
        grid=(b, hpg),
        in_specs=[col(0), col(1), col(2), tab, tab],
        out_specs=(out, out),
        scratch_shapes=[pltpu.VMEM((ncls, length, LANES), BF16),
                        pltpu.VMEM((ncls, length + BLOCK, LANES), BF16),
                        pltpu.VMEM((ncls, length + BLOCK, LANES), BF16)],
        compiler_params=_params("parallel", "parallel"),
        name=f"band_attention_d{dil}",
    )(qkv, qkv, qkv, cos, sin)
    return (o.reshape(b * s, hpg * LANES), lse.reshape(b * s, hpg * LANES))


def _dil_out_kernel(*refs, groups):
    o_refs = refs[:groups]
    l_refs = refs[groups:2 * groups]
    w_ref, x_ref, out_ref, a_ref = refs[2 * groups:]

    @pl.when(pl.program_id(1) == 0)
    def _():
        lses = [r[...] for r in l_refs]
        mx = functools.reduce(jnp.maximum, lses)
        es = [jnp.exp(l - mx) for l in lses]
        inv = 1.0 / functools.reduce(lambda u, v: u + v, es)
        w = o_refs[0].shape[1]
        for g in range(groups):
            a_ref[:, g * w:(g + 1) * w] = (o_refs[g][...] * (es[g] * inv)).astype(BF16)

    out_ref[...] = x_ref[...] + jnp.dot(a_ref[...], w_ref[...], preferred_element_type=F32)


def _dil_out(outs, lses, w_o, layer, x, tm_cap=512, tn_cap=2048):
    m, d = x.shape
    groups = len(outs)
    gw = outs[0].shape[1]
    k = groups * gw
    tm = _tile(m, tm_cap, 8)
    tn = _tile(d, tn_cap)
    part = pl.BlockSpec((tm, gw), lambda i, j: (i, 0))
    return pl.pallas_call(
        functools.partial(_dil_out_kernel, groups=groups),
        out_shape=jax.ShapeDtypeStruct((m, d), F32),
        grid=(m // tm, d // tn),
        in_specs=[part] * (2 * groups) + [
            pl.BlockSpec((None, k, tn), lambda i, j: (layer, 0, j)),
            pl.BlockSpec((tm, tn), lambda i, j: (i, j)),
        ],
        out_specs=pl.BlockSpec((tm, tn), lambda i, j: (i, j)),
        scratch_shapes=[pltpu.VMEM((tm, k), BF16)],
        compiler_params=_params("parallel", "arbitrary"),
        name="dil_out",
    )(*outs, *lses, w_o, x)


def _swa_kernel(sink_ref, q_ref, kc_ref, kp_ref, vc_ref, vp_ref, cc_ref, cp_ref, sc_ref, sp_ref,
                o_ref, qr_ref, kr_ref, vr_ref, *, tq, scale):
    pair = pl.program_id(1)
    has_prev = pl.program_id(2) > 0
    d = SWA_HEAD_DIM
    cos = cc_ref[...]
    sin = sc_ref[...]
    kx = jnp.concatenate([_rope(kp_ref[...].astype(F32), cp_ref[...], sp_ref[...], d),
                          _rope(kc_ref[...].astype(F32), cos, sin, d)], axis=0)
    vx = jnp.concatenate([vp_ref[...], vc_ref[...]], axis=0).astype(F32)
    low = lax.broadcasted_iota(jnp.int32, kx.shape, 1) < d
    for src, dst in ((kx, kr_ref), (vx, vr_ref)):
        swapped = pltpu.roll(src, d, 1)
        dst[0] = jnp.where(low, src, 0.0).astype(BF16)
        dst[1] = jnp.where(low, 0.0, swapped).astype(BF16)
        dst[2] = jnp.where(low, swapped, 0.0).astype(BF16)
        dst[3] = jnp.where(low, 0.0, src).astype(BF16)
    nblk = q_ref.shape[1] // LANES
    for c in range(nblk):
        sl = slice(c * LANES, (c + 1) * LANES)
        qr_ref[:, sl] = _rope(q_ref[:, sl].astype(F32), cos, sin, d).astype(BF16)

    qi = lax.broadcasted_iota(jnp.int32, (BLOCK, 2 * BLOCK), 0)
    kj = lax.broadcasted_iota(jnp.int32, (BLOCK, 2 * BLOCK), 1)
    dist = BLOCK + qi - kj
    band = (dist >= 0) & (dist <= SWA_WINDOW - 1)
    band0 = band & ((kj >= BLOCK) | has_prev)
    per_kv = nblk // (LANES // d)
    for r in range(tq // BLOCK):
        rows = slice(r * BLOCK, (r + 1) * BLOCK)
        mask = jnp.concatenate([band0 if r == 0 else band] * per_kv, axis=0)
        for kv in range(LANES // d):
            blocks = range(kv * per_kv, (kv + 1) * per_kv)
            q = jnp.concatenate([qr_ref[rows, c * LANES:(c + 1) * LANES] for c in blocks], axis=0)
            out = None
            for half in range(2):
                k = kr_ref[2 * kv + half, r * BLOCK:(r + 2) * BLOCK, :]
                v = vr_ref[2 * kv + half, r * BLOCK:(r + 2) * BLOCK, :]
                sink = jnp.concatenate(
                    [jnp.full((BLOCK, 1), sink_ref[(pair * nblk + c) * 2 + half], F32)
                     for c in blocks], axis=0)
                s = lax.dot_general(q, k, (((1,), (1,)), ((), ())),
                                    preferred_element_type=F32) * scale
                s = jnp.where(mask, s, -jnp.inf)
                m = jnp.maximum(jnp.max(s, axis=-1, keepdims=True), sink)
                p = jnp.exp(s - m)
                l = jnp.sum(p, axis=-1, keepdims=True) + jnp.exp(sink - m)
                o = jnp.dot(p.astype(BF16), v, preferred_element_type=F32) / l
                out = o if out is None else out + o
            for n, c in enumerate(blocks):
                o_ref[rows, c * LANES:(c + 1) * LANES] = (
                    out[n * BLOCK:(n + 1) * BLOCK].astype(o_ref.dtype))


def _swa_attention(qkv, cos, sin, sinks, q_heads, tq_cap=512):
    b, s, _ = qkv.shape
    d = SWA_HEAD_DIM
    kv_heads = q_heads // SWA_GROUP
    per_blk = LANES // d
    assert kv_heads % per_blk == 0
    nq = q_heads * d
    qw = per_blk * SWA_GROUP * d
    kcol = nq // LANES
    vcol = (nq + kv_heads * d) // LANES
    tq = _tile(s, tq_cap)
    rb = tq // BLOCK

    def cur(col0):
        return pl.BlockSpec((None, tq, LANES), lambda bi, p, i: (bi, i, col0 + p))

    def prev(col0):
        return pl.BlockSpec((None, BLOCK, LANES),
                            lambda bi, p, i: (bi, jnp.maximum(i * rb - 1, 0), col0 + p))

    tab_c = pl.BlockSpec((None, tq, LANES), lambda bi, p, i: (bi, i, 0))
    tab_p = pl.BlockSpec((None, BLOCK, LANES), lambda bi, p, i: (bi, jnp.maximum(i * rb - 1, 0), 0))
    kern = functools.partial(_swa_kernel, tq=tq, scale=d ** -0.5)
    return pl.pallas_call(
        kern,
        out_shape=jax.ShapeDtypeStruct((b, s, nq), BF16),
        grid=(b, kv_heads // per_blk, s // tq),
        in_specs=[
            pl.BlockSpec(memory_space=pltpu.SMEM),
            pl.BlockSpec((None, tq, qw), lambda bi, p, i: (bi, i, p)),
            cur(kcol), prev(kcol), cur(vcol), prev(vcol),
            tab_c, tab_p, tab_c, tab_p,
        ],
        out_specs=pl.BlockSpec((None, tq, qw), lambda bi, p, i: (bi, i, p)),
        scratch_shapes=[pltpu.VMEM((tq, qw), BF16),
                        pltpu.VMEM((2 * per_blk, tq + BLOCK, LANES), BF16),
                        pltpu.VMEM((2 * per_blk, tq + BLOCK, LANES), BF16)],
        compiler_params=_params("parallel", "parallel", "parallel"),
        name="swa_attention",
    )(sinks, qkv, qkv, qkv, qkv, qkv, cos, cos, sin, sin)


def _xattn_kernel(x_ref, g_ref, wq_ref, kv_ref, wo_ref, o_ref, a_ref, *, scale):
    x = x_ref[...]
    q = jnp.dot(_rms(x, g_ref[...]).astype(BF16), wq_ref[...],
                preferred_element_type=F32).astype(BF16)
    width = XA_HEADS * XA_HEAD_DIM
    for h in range(XA_HEADS):
        sl = slice(h * XA_HEAD_DIM, (h + 1) * XA_HEAD_DIM)
        k = kv_ref[:, sl]
        v = kv_ref[:, width + h * XA_HEAD_DIM:width + (h + 1) * XA_HEAD_DIM]
        s = lax.dot_general(q[:, sl], k, (((1,), (1,)), ((), ())),
                            preferred_element_type=F32) * scale
        m = jnp.max(s, axis=-1, keepdims=True)
        p = jnp.exp(s - m)
        l = jnp.sum(p, axis=-1, keepdims=True)
        a_ref[:, sl] = (jnp.dot(p.astype(BF16), v, preferred_element_type=F32) / l).astype(BF16)
    o_ref[...] = x + jnp.dot(a_ref[...], wo_ref[...], preferred_element_type=F32)


def _xattn(x, g, w_q, kv, w_o, layer, seq, tm_cap=512):
    m, d = x.shape
    n_mem = kv.shape[1]
    width = XA_HEADS * XA_HEAD_DIM
    tm = _tile(seq, tm_cap, 8)
    per_batch = seq // tm
    return pl.pallas_call(
        functools.partial(_xattn_kernel, scale=XA_HEAD_DIM ** -0.5),
        out_shape=jax.ShapeDtypeStruct((m, d), F32),
        grid=(m // tm,),
        in_specs=[
            pl.BlockSpec((tm, d), lambda i: (i, 0)),
            pl.BlockSpec((1, d), lambda i: (0, 0)),
            pl.BlockSpec((None, d, width), lambda i: (layer, 0, 0)),
            pl.BlockSpec((None, n_mem, 2 * width), lambda i: (i // per_batch, 0, 0)),
            pl.BlockSpec((None, width, d), lambda i: (layer, 0, 0)),
        ],
        out_specs=pl.BlockSpec((tm, d), lambda i: (i, 0)),
        scratch_shapes=[pltpu.VMEM((tm, width), BF16)],
        compiler_params=_params("parallel"),
        name="xattn",
    )(x, g.reshape(1, d), w_q, kv, w_o)


def _mix_xattn_kernel(*refs, groups, scale):
    if groups:
        o_refs, l_refs = refs[:groups], refs[groups:2 * groups]
        (wm_ref, bm_ref, x_ref, g_ref, wq_ref, kv_ref, wo_ref, out_ref,
         att_ref, a_ref) = refs[2 * groups:]
        lses = [r[...] for r in l_refs]
        mx = functools.reduce(jnp.maximum, lses)
        es = [jnp.exp(l - mx) for l in lses]
        inv = 1.0 / functools.reduce(lambda u, v: u + v, es)
        gw = o_refs[0].shape[1]
        for gi in range(groups):
            a_ref[:, gi * gw:(gi + 1) * gw] = (o_refs[gi][...] * (es[gi] * inv)).astype(BF16)
        a = a_ref[...]
    else:
        a_in, wm_ref, bm_ref, x_ref, g_ref, wq_ref, kv_ref, wo_ref, out_ref, att_ref = refs
        a = a_in[...]
    x1 = x_ref[...] + jnp.dot(a, wm_ref[...], preferred_element_type=F32) + bm_ref[...]
    q = jnp.dot(_rms(x1, g_ref[...]).astype(BF16), wq_ref[...],
                preferred_element_type=F32).astype(BF16)
    width = XA_HEADS * XA_HEAD_DIM
    for h in range(XA_HEADS):
        sl = slice(h * XA_HEAD_DIM, (h + 1) * XA_HEAD_DIM)
        k = kv_ref[:, sl]
        v = kv_ref[:, width + h * XA_HEAD_DIM:width + (h + 1) * XA_HEAD_DIM]
        s = lax.dot_general(q[:, sl], k, (((1,), (1,)), ((), ())),
                            preferred_element_type=F32) * scale
        m = jnp.max(s, axis=-1, keepdims=True)
        p = jnp.exp(s - m)
        l = jnp.sum(p, axis=-1, keepdims=True)
        att_ref[:, sl] = (jnp.dot(p.astype(BF16), v, preferred_element_type=F32) / l).astype(BF16)
    out_ref[...] = x1 + jnp.dot(att_ref[...], wo_ref[...], preferred_element_type=F32)


def _mix_xattn(parts, w_mix, mix_layer, bias, x, g, w_q, kv, w_o, layer, seq, tm_cap=512):
    m, d = x.shape
    n_mem = kv.shape[1]
    k = w_mix.shape[1]
    width = XA_HEADS * XA_HEAD_DIM
    groups = len(parts) // 2 if len(parts) > 1 else 0
    tm = _tile(seq, tm_cap, 8)
    per_batch = seq // tm
    if bias is None:
        bias = jnp.zeros((d,), F32)
    once = pl.Buffered(1)
    in_specs = [pl.BlockSpec((tm, p.shape[1]), lambda i: (i, 0)) for p in parts] + [
        pl.BlockSpec((None, k, d), lambda i: (mix_layer, 0, 0), pipeline_mode=once),
        pl.BlockSpec((1, d), lambda i: (0, 0)),
        pl.BlockSpec((tm, d), lambda i: (i, 0)),
        pl.BlockSpec((1, d), lambda i: (0, 0)),
        pl.BlockSpec((None, d, width), lambda i: (layer, 0, 0), pipeline_mode=once),
        pl.BlockSpec((None, n_mem, 2 * width), lambda i: (i // per_batch, 0, 0)),
        pl.BlockSpec((None, width, d), lambda i: (layer, 0, 0), pipeline_mode=once),
    ]
    scratch = [pltpu.VMEM((tm, width), BF16)]
    if groups:
        scratch.append(pltpu.VMEM((tm, k), BF16))
    return pl.pallas_call(
        functools.partial(_mix_xattn_kernel, groups=groups, scale=XA_HEAD_DIM ** -0.5),
        out_shape=jax.ShapeDtypeStruct((m, d), F32),
        grid=(m // tm,),
        in_specs=in_specs,
        out_specs=pl.BlockSpec((tm, d), lambda i: (i, 0)),
        scratch_shapes=scratch,
        compiler_params=_params("parallel"),
        name="mix_xattn",
    )(*parts, w_mix, bias.reshape(1, d), x, g.reshape(1, d), w_q, kv, w_o)


def kernel(x, mem, positions, ffn1_norm, ffn1_w_gate_up, ffn1_w_down, mix_norm, sb_w_qkv, sb_w_o, dil_w_qkv, dil_w_o, swa_w_qkv, swa_b_qkv, swa_sinks, swa_w_o, swa_b_o, xattn_norm, mem_norm, xattn_w_q, xattn_w_kv, xattn_w_o, ffn2_norm, ffn2_w_gate_up, ffn2_w_down, final_norm):
    b, s, d = x.shape
    m = b * s
    depth = ffn1_norm.shape[0]
    n_mem = mem.shape[1]
    sb_heads = d // HEAD_DIM
    dil_hpg = d // (4 * HEAD_DIM)
    dil_heads = dil_hpg * len(DIL_PATTERNS)
    swa_q_heads = d // SWA_HEAD_DIM

    xs = x.reshape(m, d)
    mem2 = mem.reshape(b * n_mem, d)
    pos = positions.reshape(m, 1)
    if depth > 1:
        dil_cos, dil_sin = (t.reshape(b, s, LANES) for t in _rope_tables(pos, HEAD_DIM))
    if depth > 2:
        swa_cos, swa_sin = (t.reshape(b, s, LANES) for t in _rope_tables(pos, SWA_HEAD_DIM))

    (sb_w_qkv, sb_w_o, dil_w_qkv, dil_w_o, swa_w_qkv, swa_w_o, xattn_w_q, xattn_w_kv,
     xattn_w_o) = (w.astype(BF16) for w in (sb_w_qkv, sb_w_o, dil_w_qkv, dil_w_o, swa_w_qkv,
                                            swa_w_o, xattn_w_q, xattn_w_kv, xattn_w_o))
    chain = _ffn_can_convert(m, d, ffn1_w_down.shape[1])

    def cast_ffn(w_gu, w_d, layer):
        return w_gu[layer].astype(BF16), w_d[layer].astype(BF16)

    w1 = cast_ffn(ffn1_w_gate_up, ffn1_w_down, 0)
    for i in range(depth):
        if chain:
            xs, *w2 = _ffn(xs, ffn1_norm[i], *w1, convert=(ffn2_w_gate_up, ffn2_w_down, i))
        else:
            xs = _ffn(xs, ffn1_norm[i], *w1)
            w2 = cast_ffn(ffn2_w_gate_up, ffn2_w_down, i)

        kv = _norm_matmul(mem2, mem_norm[i], xattn_w_kv, i, None, BF16).reshape(b, n_mem, -1)
        kind, j = i % 3, i // 3
        if kind == 0:
            qkv = _norm_matmul(xs, mix_norm[i], sb_w_qkv, j, None, BF16,
                               col_scale=_sb_q_scale(sb_heads))
            parts = [_sb_attention(qkv.reshape(b, s, -1), sb_heads).reshape(m, -1)]
            w_mix, b_mix = sb_w_o, None
        elif kind == 1:
            qkv = _norm_matmul(xs, mix_norm[i], dil_w_qkv, j, None, F32)
            qkv = qkv.reshape(b, s, -1)
            outs, lses = [], []
            for g, (window, dil) in enumerate(DIL_PATTERNS):
                o, lse = _band_attention(qkv, dil_cos, dil_sin, g, dil_hpg, dil_heads, window, dil)
                outs.append(o)
                lses.append(lse)
            parts = outs + lses
            w_mix, b_mix = dil_w_o, None
        else:
            qkv = _norm_matmul(xs, mix_norm[i], swa_w_qkv, j, swa_b_qkv[j], BF16)
            parts = [_swa_attention(qkv.reshape(b, s, -1), swa_cos, swa_sin, swa_sinks[j],
                                    swa_q_heads).reshape(m, -1)]
            w_mix, b_mix = swa_w_o, swa_b_o[j]
        xs = _mix_xattn(parts, w_mix, j, b_mix, xs, xattn_norm[i], xattn_w_q, kv, xattn_w_o, i, s)

        last = i == depth - 1
        if chain and not last:
            xs, *w1 = _ffn(xs, ffn2_norm[i], *w2, convert=(ffn1_w_gate_up, ffn1_w_down, i + 1))
        else:
            xs = _ffn(xs, ffn2_norm[i], *w2, final_g=final_norm if last else None)
            if not last:
                w1 = cast_ffn(ffn1_w_gate_up, ffn1_w_down, i + 1)

    return xs.reshape(b, s, d)
```

```python
import functools

import jax
import jax.numpy as jnp
from jax import lax
from jax.experimental import pallas as pl
from jax.experimental.pallas import tpu as pltpu

F32 = jnp.float32
BF16 = jnp.bfloat16

NORM_EPS = 1e-6
ROPE_THETA = 500000.0
ROPE_FRACTION = 4
LOG2E = 1.4426950408889634
HEAD_DIM = 128
BLOCK = 128
BAND_BLOCKS_IN_FLIGHT = 8
DIL_PATTERNS = ((128, 1), (512, 4), (2048, 16))
SWA_HEAD_DIM = 64
SWA_GROUP = 8
SWA_WINDOW = 128
XA_HEADS = 4
XA_HEAD_DIM = 128

LANES = 128
V7X_VMEM_BYTES = 64 * 2**20
VMEM_LIMIT_BYTES = (V7X_VMEM_BYTES * 3) // 4
FFN_VMEM_LIMIT_BYTES = (V7X_VMEM_BYTES * 15) // 16
NORM_MATMUL_TILE_BYTES = 18 * 2**20


def _params(*semantics, vmem_limit=VMEM_LIMIT_BYTES):
    return pltpu.CompilerParams(dimension_semantics=semantics, vmem_limit_bytes=vmem_limit)


def _tile(n, cap, mult=LANES):
    if n <= cap:
        return n
    t = (cap // mult) * mult
    while t > mult and n % t:
        t -= mult
    assert n % t == 0, (n, cap, mult)
    return t


def _rms(x, g):
    ms = jnp.mean(x * x, axis=-1, keepdims=True)
    return x * lax.rsqrt(ms + NORM_EPS) * g


def _norm_matmul_kernel(x_ref, g_ref, w_ref, b_ref, s_ref, o_ref, xn_ref):
    def tile(first):
        if first:
            xn = _rms(x_ref[...], g_ref[...]).astype(BF16)
            xn_ref[...] = xn
        else:
            xn = xn_ref[...]
        acc = jnp.dot(xn, w_ref[...], preferred_element_type=F32)
        o_ref[...] = ((acc + b_ref[...]) * s_ref[...]).astype(o_ref.dtype)

    pl.when(pl.program_id(1) == 0)(functools.partial(tile, True))
    pl.when(pl.program_id(1) > 0)(functools.partial(tile, False))


def _norm_matmul(x, g, w, layer, bias, out_dtype, col_scale=None, tm_cap=1024):
    m, d = x.shape
    n = w.shape[2]
    tm = _tile(m, tm_cap, 8)
    col_bytes = 2 * (jnp.dtype(BF16).itemsize * d + jnp.dtype(out_dtype).itemsize * tm)
    tn = _tile(n, NORM_MATMUL_TILE_BYTES // col_bytes // LANES * LANES)
    if bias is None:
        bias = jnp.zeros((n,), F32)
    if col_scale is None:
        col_scale = jnp.ones((n,), F32)
    return pl.pallas_call(
        _norm_matmul_kernel,
        out_shape=jax.ShapeDtypeStruct((m, n), out_dtype),
        grid=(m // tm, n // tn),
        in_specs=[
            pl.BlockSpec((tm, d), lambda i, j: (i, 0)),
            pl.BlockSpec((1, d), lambda i, j: (0, 0)),
            pl.BlockSpec((None, d, tn), lambda i, j: (layer, 0, j)),
            pl.BlockSpec((1, tn), lambda i, j: (0, j)),
            pl.BlockSpec((1, tn), lambda i, j: (0, j)),
        ],
        out_specs=pl.BlockSpec((tm, tn), lambda i, j: (i, j)),
        scratch_shapes=[pltpu.VMEM((tm, d), BF16)],
        compiler_params=_params("parallel", "arbitrary"),
        name="norm_matmul",
    )(x, g.reshape(1, d), w, bias.reshape(1, n), col_scale.reshape(1, n))


def _ffn_kernel(x_ref, g_ref, wg_ref, wu_ref, wd_ref, gf_ref, *rest, nf, final_norm, convert):
    if convert:
        ngu_ref, nd_ref, o_ref, ngu_out, nd_out, xn_ref = rest
        ngu_out[...] = ngu_ref[...].astype(BF16)
        nd_out[...] = nd_ref[...].astype(BF16)
    else:
        o_ref, xn_ref = rest
    j = pl.program_id(1)

    def chunk(first, last):
        if first:
            xn = _rms(x_ref[...], g_ref[...]).astype(BF16)
            xn_ref[...] = xn
        else:
            xn = xn_ref[...]
        gate = jnp.dot(xn, wg_ref[...], preferred_element_type=F32)
        up = jnp.dot(xn, wu_ref[...], preferred_element_type=F32)
        h = gate * jax.nn.sigmoid(gate) * up
        down = jnp.dot(h.astype(BF16), wd_ref[...], preferred_element_type=F32)
        if not first:
            down = o_ref[...] + down
        o_ref[...] = x_ref[...] + 0.5 * down if last else down

    if final_norm:
        pl.when(j == 0)(functools.partial(chunk, True, False))
        pl.when(j > 0)(functools.partial(chunk, False, False))

        @pl.when(j == nf - 1)
        def _():
            rows = min(256, o_ref.shape[0])

            def finish(c, carry):
                sl = pl.ds(pl.multiple_of(c * rows, rows), rows)
                o_ref[sl, :] = _rms(x_ref[sl, :] + 0.5 * o_ref[sl, :], gf_ref[...])
                return carry

            lax.fori_loop(0, o_ref.shape[0] // rows, finish, 0)
    elif nf == 1:
        chunk(True, True)
    else:
        pl.when(j == 0)(functools.partial(chunk, True, False))
        pl.when((j > 0) & (j < nf - 1))(functools.partial(chunk, False, False))
        pl.when(j == nf - 1)(functools.partial(chunk, False, True))


def _ffn_tiles(m, f, tm_cap=1024, tf_cap=512):
    return _tile(m, tm_cap, 8), _tile(f, tf_cap)


def _ffn_can_convert(m, d, f):
    tm, tf = _ffn_tiles(m, f)
    steps_i = m // tm
    return d % (steps_i * 16) == 0 and tf % (steps_i * 16) == 0


def _ffn(x, g, w_gu, w_d, final_g=None, convert=None):
    m, d = x.shape
    f = w_d.shape[0]
    tm, tf = _ffn_tiles(m, f)
    nf = f // tf
    ni = m // tm
    gf = jnp.ones((d,), F32) if final_g is None else final_g
    in_specs = [
        pl.BlockSpec((tm, d), lambda i, j: (i, 0)),
        pl.BlockSpec((1, d), lambda i, j: (0, 0)),
        pl.BlockSpec((d, tf), lambda i, j: (0, j)),
        pl.BlockSpec((d, tf), lambda i, j: (0, j + nf)),
        pl.BlockSpec((tf, d), lambda i, j: (j, 0)),
        pl.BlockSpec((1, d), lambda i, j: (0, 0)),
    ]
    args = [x, g.reshape(1, d), w_gu, w_gu, w_d, gf.reshape(1, d)]
    out_specs = [pl.BlockSpec((tm, d), lambda i, j: (i, 0))]
    out_shape = [jax.ShapeDtypeStruct((m, d), F32)]
    if convert is not None:
        src_gu, src_d, layer = convert
        gu_blk = (d // ni, 2 * tf)
        d_blk = (tf // ni, d)
        in_specs += [pl.BlockSpec((None,) + gu_blk, lambda i, j: (layer, i, j)),
                     pl.BlockSpec((None,) + d_blk, lambda i, j: (layer, i * nf + j, 0))]
        args += [src_gu, src_d]
        out_specs += [pl.BlockSpec(gu_blk, lambda i, j: (i, j)),
                      pl.BlockSpec(d_blk, lambda i, j: (i * nf + j, 0))]
        out_shape += [jax.ShapeDtypeStruct((d, 2 * f), BF16), jax.ShapeDtypeStruct((f, d), BF16)]
    outs = pl.pallas_call(
        functools.partial(_ffn_kernel, nf=nf, final_norm=final_g is not None,
                          convert=convert is not None),
        out_shape=out_shape,
        grid=(ni, nf),
        in_specs=in_specs,
        out_specs=out_specs,
        scratch_shapes=[pltpu.VMEM((tm, d), BF16)],
        compiler_params=_params("parallel", "arbitrary", vmem_limit=FFN_VMEM_LIMIT_BYTES),
        name="ffn",
    )(*args)
    return outs[0] if convert is None else outs


def _sb_kernel(q_ref, k_ref, v_ref, tri_ref, o_ref, acc_ref, car_ref, z_ref, *, tq, tk):
    row0 = pl.program_id(2) * tq
    acc_ref[...] = jnp.zeros_like(acc_ref)
    tri = tri_ref[...]
    nsub = tq // tk
    nheads = q_ref.shape[1] // HEAD_DIM
    assert nheads % 2 == 0
    cols = [slice(h * HEAD_DIM, (h + 1) * HEAD_DIM) for h in range(nheads)]

    def logits(h, ks):
        return lax.dot_general(q_ref[:, cols[h]], k_ref[pl.ds(ks, tq), cols[h]],
                               (((1,), (1,)), ((), ())), preferred_element_type=F32)

    def neg_log2_keep(z2):
        neg_abs = lax.bitcast_convert_type(
            lax.bitcast_convert_type(z2, jnp.uint32) | jnp.uint32(0x80000000), F32)
        return jnp.maximum(z2, 0.0) + jnp.log2(1.0 + jnp.exp2(neg_abs))

    def prefetch(h, ks, ks_next):
        if h + 1 < nheads:
            z_ref[(h + 1) % 2] = logits(h + 1, ks)
        else:
            z_ref[0] = logits(0, ks_next)

    def full_tile(ks, ks_next):
        for h, hs in enumerate(cols):
            prefetch(h, ks, ks_next)
            zr = z_ref.at[h % 2]
            nlb = neg_log2_keep(zr[...]).astype(BF16)
            car = car_ref[:, hs]
            parts = [None] * nsub
            for j in reversed(range(nsub)):
                sl = slice(j * tk, (j + 1) * tk)
                ssum = jnp.dot(nlb[:, sl], tri, preferred_element_type=F32)
                parts[j] = jnp.exp2(zr[:, sl] - ssum
                                    - jnp.concatenate([car] * (tk // LANES), axis=1))
                car = car + jnp.broadcast_to(ssum[:, 0:1], (tq, LANES))
            a = jnp.concatenate(parts, axis=1)
            acc_ref[:, hs] += jnp.dot(a.astype(BF16), v_ref[pl.ds(ks, tq), hs],
                                      preferred_element_type=F32)
            car_ref[:, hs] = car

    def diagonal_tile(ks, ks_next):
        lower = (lax.broadcasted_iota(jnp.int32, (tk, tk), 1)
                 < lax.broadcasted_iota(jnp.int32, (tk, tk), 0))
        for h, hs in enumerate(cols):
            prefetch(h, ks, ks_next)
            zr = z_ref.at[h % 2]
            car = jnp.zeros((tq, LANES), F32)
            for j in reversed(range(nsub)):
                r0 = j * tk
                nr = tq - r0
                sl = slice(j * tk, (j + 1) * tk)
                mask = lower if nr == tk else jnp.concatenate(
                    [lower, jnp.ones((nr - tk, tk), jnp.bool_)], axis=0)
                z2 = zr[r0:, sl]
                nl = jnp.where(mask, neg_log2_keep(z2), 0.0)
                ssum = jnp.dot(nl.astype(BF16), tri, preferred_element_type=F32)
                a = jnp.exp2(z2 - ssum - jnp.concatenate([car[r0:]] * (tk // LANES), axis=1))
                a = jnp.where(mask, a, 0.0)
                acc_ref[r0:, hs] += jnp.dot(a.astype(BF16), v_ref[pl.ds(ks + r0, tk), hs],
                                            preferred_element_type=F32)
                upd = car[r0:] + jnp.broadcast_to(ssum[:, 0:1], (nr, LANES))
                car = upd if r0 == 0 else jnp.concatenate([car[:r0], upd], axis=0)
            car_ref[:, hs] = car

    def start(t):
        return pl.multiple_of(jnp.maximum(row0 - (t + 1) * tq, 0), tq)

    z_ref[0] = logits(0, pl.multiple_of(row0, tq))
    diagonal_tile(pl.multiple_of(row0, tq), start(0))

    def body(t, c):
        full_tile(start(t), start(t + 1))
        return c

    lax.fori_loop(0, pl.program_id(2), body, 0)
    o_ref[...] = acc_ref[...].astype(o_ref.dtype)


def _sb_q_scale(heads):
    width = heads * HEAD_DIM
    return jnp.concatenate([jnp.full((width,), HEAD_DIM ** -0.5 * LOG2E, F32),
                            jnp.ones((2 * width,), F32)])


def _sb_attention(qkv, heads, tq_cap=512, tk_cap=256, heads_per_step=4):
    b, s, _ = qkv.shape
    tq = _tile(s, tq_cap)
    tk = _tile(tq, tk_cap)
    hp = heads_per_step if heads % heads_per_step == 0 else 1
    hw = hp * HEAD_DIM
    hb = heads // hp
    tri = jnp.tril(jnp.ones((tk, tk), F32)).astype(BF16)
    kern = functools.partial(_sb_kernel, tq=tq, tk=tk)
    return pl.pallas_call(
        kern,
        out_shape=jax.ShapeDtypeStruct((b, s, heads * HEAD_DIM), BF16),
        grid=(b, hb, s // tq),
        in_specs=[
            pl.BlockSpec((None, tq, hw), lambda bi, h, i: (bi, i, h)),
            pl.BlockSpec((None, s, hw), lambda bi, h, i: (bi, 0, hb + h)),
            pl.BlockSpec((None, s, hw), lambda bi, h, i: (bi, 0, 2 * hb + h)),
            pl.BlockSpec((tk, tk), lambda bi, h, i: (0, 0)),
        ],
        out_specs=pl.BlockSpec((None, tq, hw), lambda bi, h, i: (bi, i, h)),
        scratch_shapes=[pltpu.VMEM((tq, hw), F32), pltpu.VMEM((tq, hw), F32),
                        pltpu.VMEM((2, tq, tq), F32)],
        compiler_params=_params("parallel", "parallel", "arbitrary"),
        name="sb_attention",
    )(qkv, qkv, qkv, tri)


def _rope_table_kernel(pos_ref, freq_ref, sign_ref, cos_ref, sin_ref):
    ang = pos_ref[...].astype(F32) * freq_ref[...]
    cos_ref[...] = jnp.cos(ang)
    sin_ref[...] = jnp.sin(ang) * sign_ref[...]


def _rope_tables(positions, head_dim):
    m = positions.shape[0]
    rot = head_dim // ROPE_FRACTION
    half = rot // 2
    inv_freq = jnp.power(F32(ROPE_THETA), -jnp.arange(half, dtype=F32) * 2.0 / rot)
    lane = jnp.arange(LANES) % head_dim
    freq = jnp.where(lane < rot, inv_freq[lane % half], 0.0).astype(F32)
    sign = jnp.where(lane < half, -1.0, jnp.where(lane < rot, 1.0, 0.0)).astype(F32)
    tm = _tile(m, 2048, 8)
    out = jax.ShapeDtypeStruct((m, LANES), F32)
    return pl.pallas_call(
        _rope_table_kernel,
        out_shape=(out, out),
        grid=(m // tm,),
        in_specs=[
            pl.BlockSpec((tm, 1), lambda i: (i, 0)),
            pl.BlockSpec((1, LANES), lambda i: (0, 0)),
            pl.BlockSpec((1, LANES), lambda i: (0, 0)),
        ],
        out_specs=(pl.BlockSpec((tm, LANES), lambda i: (i, 0)),
                   pl.BlockSpec((tm, LANES), lambda i: (i, 0))),
        compiler_params=_params("parallel"),
        name="rope_tables",
    )(positions, freq.reshape(1, LANES), sign.reshape(1, LANES))


def _rope(x, cos, sin, head_dim):
    half = head_dim // ROPE_FRACTION // 2
    lane = lax.broadcasted_iota(jnp.int32, x.shape, 1) & (head_dim - 1)
    ahead = pltpu.roll(x, LANES - half, 1)
    behind = pltpu.roll(x, half, 1)
    return x * cos + jnp.where(lane < half, ahead, behind) * sin


def _band_kernel(q_ref, k_ref, v_ref, cos_ref, sin_ref, o_ref, lse_ref, qr_ref, kr_ref, vr_ref,
                 *, dil, length, max_dist, scale):
    kj = lax.broadcasted_iota(jnp.int32, (BLOCK, 2 * BLOCK), 1)
    dist = BLOCK + lax.broadcasted_iota(jnp.int32, (BLOCK, 2 * BLOCK), 0) - kj
    band = (dist >= 0) & (dist <= max_dist)
    band0 = band & (kj >= BLOCK)
    nblocks = length // BLOCK
    kr_ref[:, 0:BLOCK, :] = jnp.zeros((kr_ref.shape[0], BLOCK, LANES), BF16)
    vr_ref[:, 0:BLOCK, :] = jnp.zeros((vr_ref.shape[0], BLOCK, LANES), BF16)

    def block(p, slot, a, mask):
        q0 = pl.multiple_of(a * BLOCK, BLOCK)
        q = qr_ref[slot, pl.ds(q0, BLOCK), :]
        k = kr_ref[slot, pl.ds(q0, 2 * BLOCK), :]
        v = vr_ref[slot, pl.ds(q0, 2 * BLOCK), :]
        s = lax.dot_general(q, k, (((1,), (1,)), ((), ())),
                            preferred_element_type=F32) * scale
        s = jnp.where(mask, s, -jnp.inf)
        m = jnp.max(s, axis=-1, keepdims=True)
        e = jnp.exp(s - m)
        l = jnp.sum(e, axis=-1, keepdims=True)
        o = jnp.dot(e.astype(BF16), v, preferred_element_type=F32) / l
        dest = pl.ds(p + q0 * dil, BLOCK, stride=dil)
        o_ref[dest, :] = o
        lse_ref[dest, :] = jnp.broadcast_to(m + jnp.log(l), (BLOCK, LANES))

    ncls = qr_ref.shape[0]
    group = min(BAND_BLOCKS_IN_FLIGHT, nblocks)
    assert nblocks % group == 0 and dil % ncls == 0

    def classes(t, c):
        ps = [t * ncls + slot for slot in range(ncls)]
        for slot, p in enumerate(ps):
            rows = pl.ds(p, length, stride=dil)
            cos = cos_ref[rows, :]
            sin = sin_ref[rows, :]
            qr_ref[slot] = _rope(q_ref[rows, :], cos, sin, HEAD_DIM).astype(BF16)
            kr_ref[slot, BLOCK:, :] = _rope(k_ref[rows, :], cos, sin, HEAD_DIM).astype(BF16)
            vr_ref[slot, BLOCK:, :] = v_ref[rows, :].astype(BF16)
        for slot, p in enumerate(ps):
            block(p, slot, 0, band0)
            for a in range(1, group):
                block(p, slot, a, band)

        def trip(u, c2):
            for slot, p in enumerate(ps):
                for a in range(group):
                    block(p, slot, group * u + a, band)
            return c2

        lax.fori_loop(1, nblocks // group, trip, 0)
        return c

    lax.fori_loop(0, dil // ncls, classes, 0)


def _band_attention(qkv, cos, sin, group, hpg, n_heads, window, dil):
    b, s, n = qkv.shape
    length = s // dil
    assert length % BLOCK == 0 and window % dil == 0 and window // dil <= BLOCK
    h0 = group * hpg

    def col(which):
        return pl.BlockSpec((None, s, LANES), lambda bi, h: (bi, 0, which * n_heads + h0 + h))

    tab = pl.BlockSpec((None, s, LANES), lambda bi, h: (bi, 0, 0))
    out = pl.BlockSpec((None, s, LANES), lambda bi, h: (bi, 0, h))
    shape = jax.ShapeDtypeStruct((b, s, hpg * LANES), F32)
    ncls = min(dil, max(1, (BAND_BLOCKS_IN_FLIGHT * BLOCK) // length))
    kern = functools.partial(_band_kernel, dil=dil, length=length, max_dist=window // dil,
                             scale=HEAD_DIM ** -0.5)
    o, lse = pl.pallas_call(
        kern,
        out_shape=(shape, shape),
        grid=(b, hpg),
        in_specs=[col(0), col(1), col(2), tab, tab],
        out_specs=(out, out),
        scratch_shapes=[pltpu.VMEM((ncls, length, LANES), BF16),
                        pltpu.VMEM((ncls, length + BLOCK, LANES), BF16),
                        pltpu.VMEM((ncls, length + BLOCK, LANES), BF16)],
        compiler_params=_params("parallel", "parallel"),
        name=f"band_attention_d{dil}",
    )(qkv, qkv, qkv, cos, sin)
    return (o.reshape(b * s, hpg * LANES), lse.reshape(b * s, hpg * LANES))


def _swa_kernel(sink_ref, q_ref, kc_ref, kp_ref, vc_ref, vp_ref, cc_ref, cp_ref, sc_ref, sp_ref,
                o_ref, qr_ref, kr_ref, vr_ref, *, tq, scale):
    pair = pl.program_id(1)
    has_prev = pl.program_id(2) > 0
    d = SWA_HEAD_DIM
    cos = cc_ref[...]
    sin = sc_ref[...]
    kx = jnp.concatenate([_rope(kp_ref[...].astype(F32), cp_ref[...], sp_ref[...], d),
                          _rope(kc_ref[...].astype(F32), cos, sin, d)], axis=0)
    vx = jnp.concatenate([vp_ref[...], vc_ref[...]], axis=0).astype(F32)
    low = lax.broadcasted_iota(jnp.int32, kx.shape, 1) < d
    for src, dst in ((kx, kr_ref), (vx, vr_ref)):
        swapped = pltpu.roll(src, d, 1)
        dst[0] = jnp.where(low, src, 0.0).astype(BF16)
        dst[1] = jnp.where(low, 0.0, swapped).astype(BF16)
        dst[2] = jnp.where(low, swapped, 0.0).astype(BF16)
        dst[3] = jnp.where(low, 0.0, src).astype(BF16)
    nblk = q_ref.shape[1] // LANES
    for c in range(nblk):
        sl = slice(c * LANES, (c + 1) * LANES)
        qr_ref[:, sl] = _rope(q_ref[:, sl].astype(F32), cos, sin, d).astype(BF16)

    qi = lax.broadcasted_iota(jnp.int32, (BLOCK, 2 * BLOCK), 0)
    kj = lax.broadcasted_iota(jnp.int32, (BLOCK, 2 * BLOCK), 1)
    dist = BLOCK + qi - kj
    band = (dist >= 0) & (dist <= SWA_WINDOW - 1)
    band0 = band & ((kj >= BLOCK) | has_prev)
    per_kv = nblk // (LANES // d)
    for r in range(tq // BLOCK):
        rows = slice(r * BLOCK, (r + 1) * BLOCK)
        mask = jnp.concatenate([band0 if r == 0 else band] * per_kv, axis=0)
        for kv in range(LANES // d):
            blocks = range(kv * per_kv, (kv + 1) * per_kv)
            q = jnp.concatenate([qr_ref[rows, c * LANES:(c + 1) * LANES] for c in blocks], axis=0)
            out = None
            for half in range(2):
                k = kr_ref[2 * kv + half, r * BLOCK:(r + 2) * BLOCK, :]
                v = vr_ref[2 * kv + half, r * BLOCK:(r + 2) * BLOCK, :]
                sink = jnp.concatenate(
                    [jnp.full((BLOCK, 1), sink_ref[(pair * nblk + c) * 2 + half], F32)
                     for c in blocks], axis=0)
                s = lax.dot_general(q, k, (((1,), (1,)), ((), ())),
                                    preferred_element_type=F32) * scale
                s = jnp.where(mask, s, -jnp.inf)
                m = jnp.maximum(jnp.max(s, axis=-1, keepdims=True), sink)
                p = jnp.exp(s - m)
                l = jnp.sum(p, axis=-1, keepdims=True) + jnp.exp(sink - m)
                o = jnp.dot(p.astype(BF16), v, preferred_element_type=F32) / l
                out = o if out is None else out + o
            for n, c in enumerate(blocks):
                o_ref[rows, c * LANES:(c + 1) * LANES] = (
                    out[n * BLOCK:(n + 1) * BLOCK].astype(o_ref.dtype))


def _swa_attention(qkv, cos, sin, sinks, q_heads, tq_cap=512):
    b, s, _ = qkv.shape
    d = SWA_HEAD_DIM
    kv_heads = q_heads // SWA_GROUP
    per_blk = LANES // d
    assert kv_heads % per_blk == 0
    nq = q_heads * d
    qw = per_blk * SWA_GROUP * d
    kcol = nq // LANES
    vcol = (nq + kv_heads * d) // LANES
    tq = _tile(s, tq_cap)
    rb = tq // BLOCK

    def cur(col0):
        return pl.BlockSpec((None, tq, LANES), lambda bi, p, i: (bi, i, col0 + p))

    def prev(col0):
        return pl.BlockSpec((None, BLOCK, LANES),
                            lambda bi, p, i: (bi, jnp.maximum(i * rb - 1, 0), col0 + p))

    tab_c = pl.BlockSpec((None, tq, LANES), lambda bi, p, i: (bi, i, 0))
    tab_p = pl.BlockSpec((None, BLOCK, LANES), lambda bi, p, i: (bi, jnp.maximum(i * rb - 1, 0), 0))
    kern = functools.partial(_swa_kernel, tq=tq, scale=d ** -0.5)
    return pl.pallas_call(
        kern,
        out_shape=jax.ShapeDtypeStruct((b, s, nq), BF16),
        grid=(b, kv_heads // per_blk, s // tq),
        in_specs=[
            pl.BlockSpec(memory_space=pltpu.SMEM),
            pl.BlockSpec((None, tq, qw), lambda bi, p, i: (bi, i, p)),
            cur(kcol), prev(kcol), cur(vcol), prev(vcol),
            tab_c, tab_p, tab_c, tab_p,
        ],
        out_specs=pl.BlockSpec((None, tq, qw), lambda bi, p, i: (bi, i, p)),
        scratch_shapes=[pltpu.VMEM((tq, qw), BF16),
                        pltpu.VMEM((2 * per_blk, tq + BLOCK, LANES), BF16),
                        pltpu.VMEM((2 * per_blk, tq + BLOCK, LANES), BF16)],
        compiler_params=_params("parallel", "parallel", "parallel"),
        name="swa_attention",
    )(sinks, qkv, qkv, qkv, qkv, qkv, cos, cos, sin, sin)


def _mix_xattn_kernel(*refs, groups, scale):
    if groups:
        o_refs, l_refs = refs[:groups], refs[groups:2 * groups]
        (wm_ref, bm_ref, x_ref, g_ref, wq_ref, kv_ref, wo_ref, out_ref,
         att_ref, a_ref) = refs[2 * groups:]
        lses = [r[...] for r in l_refs]
        mx = functools.reduce(jnp.maximum, lses)
        es = [jnp.exp(l - mx) for l in lses]
        inv = 1.0 / functools.reduce(lambda u, v: u + v, es)
        gw = o_refs[0].shape[1]
        for gi in range(groups):
            a_ref[:, gi * gw:(gi + 1) * gw] = (o_refs[gi][...] * (es[gi] * inv)).astype(BF16)
        a = a_ref[...]
    else:
        a_in, wm_ref, bm_ref, x_ref, g_ref, wq_ref, kv_ref, wo_ref, out_ref, att_ref = refs
        a = a_in[...]
    x1 = x_ref[...] + jnp.dot(a, wm_ref[...], preferred_element_type=F32) + bm_ref[...]
    q = jnp.dot(_rms(x1, g_ref[...]).astype(BF16), wq_ref[...],
                preferred_element_type=F32).astype(BF16)
    width = XA_HEADS * XA_HEAD_DIM
    for h in range(XA_HEADS):
        sl = slice(h * XA_HEAD_DIM, (h + 1) * XA_HEAD_DIM)
        k = kv_ref[:, sl]
        v = kv_ref[:, width + h * XA_HEAD_DIM:width + (h + 1) * XA_HEAD_DIM]
        s = lax.dot_general(q[:, sl], k, (((1,), (1,)), ((), ())),
                            preferred_element_type=F32) * scale
        m = jnp.max(s, axis=-1, keepdims=True)
        p = jnp.exp(s - m)
        l = jnp.sum(p, axis=-1, keepdims=True)
        att_ref[:, sl] = (jnp.dot(p.astype(BF16), v, preferred_element_type=F32) / l).astype(BF16)
    out_ref[...] = x1 + jnp.dot(att_ref[...], wo_ref[...], preferred_element_type=F32)


def _mix_xattn(parts, w_mix, mix_layer, bias, x, g, w_q, kv, w_o, layer, seq, tm_cap=512):
    m, d = x.shape
    n_mem = kv.shape[1]
    k = w_mix.shape[1]
    width = XA_HEADS * XA_HEAD_DIM
    groups = len(parts) // 2 if len(parts) > 1 else 0
    tm = _tile(seq, tm_cap, 8)
    per_batch = seq // tm
    if bias is None:
        bias = jnp.zeros((d,), F32)
    once = pl.Buffered(1)
    in_specs = [pl.BlockSpec((tm, p.shape[1]), lambda i: (i, 0)) for p in parts] + [
        pl.BlockSpec((None, k, d), lambda i: (mix_layer, 0, 0), pipeline_mode=once),
        pl.BlockSpec((1, d), lambda i: (0, 0)),
        pl.BlockSpec((tm, d), lambda i: (i, 0)),
        pl.BlockSpec((1, d), lambda i: (0, 0)),
        pl.BlockSpec((None, d, width), lambda i: (layer, 0, 0), pipeline_mode=once),
        pl.BlockSpec((None, n_mem, 2 * width), lambda i: (i // per_batch, 0, 0)),
        pl.BlockSpec((None, width, d), lambda i: (layer, 0, 0), pipeline_mode=once),
    ]
    scratch = [pltpu.VMEM((tm, width), BF16)]
    if groups:
        scratch.append(pltpu.VMEM((tm, k), BF16))
    return pl.pallas_call(
        functools.partial(_mix_xattn_kernel, groups=groups, scale=XA_HEAD_DIM ** -0.5),
        out_shape=jax.ShapeDtypeStruct((m, d), F32),
        grid=(m // tm,),
        in_specs=in_specs,
        out_specs=pl.BlockSpec((tm, d), lambda i: (i, 0)),
        scratch_shapes=scratch,
        compiler_params=_params("parallel"),
        name="mix_xattn",
    )(*parts, w_mix, bias.reshape(1, d), x, g.reshape(1, d), w_q, kv, w_o)


def kernel(x, mem, positions, ffn1_norm, ffn1_w_gate_up, ffn1_w_down, mix_norm, sb_w_qkv, sb_w_o, dil_w_qkv, dil_w_o, swa_w_qkv, swa_b_qkv, swa_sinks, swa_w_o, swa_b_o, xattn_norm, mem_norm, xattn_w_q, xattn_w_kv, xattn_w_o, ffn2_norm, ffn2_w_gate_up, ffn2_w_down, final_norm):
    b, s, d = x.shape
    m = b * s
    depth = ffn1_norm.shape[0]
    n_mem = mem.shape[1]
    sb_heads = d // HEAD_DIM
    dil_hpg = d // (4 * HEAD_DIM)
    dil_heads = dil_hpg * len(DIL_PATTERNS)
    swa_q_heads = d // SWA_HEAD_DIM

    xs = x.reshape(m, d)
    mem2 = mem.reshape(b * n_mem, d)
    pos = positions.reshape(m, 1)
    if depth > 1:
        dil_cos, dil_sin = (t.reshape(b, s, LANES) for t in _rope_tables(pos, HEAD_DIM))
    if depth > 2:
        swa_cos, swa_sin = (t.reshape(b, s, LANES) for t in _rope_tables(pos, SWA_HEAD_DIM))

    (sb_w_qkv, sb_w_o, dil_w_qkv, dil_w_o, swa_w_qkv, swa_w_o, xattn_w_q, xattn_w_kv,
     xattn_w_o) = (w.astype(BF16) for w in (sb_w_qkv, sb_w_o, dil_w_qkv, dil_w_o, swa_w_qkv,
                                            swa_w_o, xattn_w_q, xattn_w_kv, xattn_w_o))
    chain = _ffn_can_convert(m, d, ffn1_w_down.shape[1])

    def cast_ffn(w_gu, w_d, layer):
        return w_gu[layer].astype(BF16), w_d[layer].astype(BF16)

    w1 = cast_ffn(ffn1_w_gate_up, ffn1_w_down, 0)
    for i in range(depth):
        if chain:
            xs, *w2 = _ffn(xs, ffn1_norm[i], *w1, convert=(ffn2_w_gate_up, ffn2_w_down, i))
        else:
            xs = _ffn(xs, ffn1_norm[i], *w1)
            w2 = cast_ffn(ffn2_w_gate_up, ffn2_w_down, i)

        kv = _norm_matmul(mem2, mem_norm[i], xattn_w_kv, i, None, BF16).reshape(b, n_mem, -1)
        kind, j = i % 3, i // 3
        if kind == 0:
            qkv = _norm_matmul(xs, mix_norm[i], sb_w_qkv, j, None, BF16,
                               col_scale=_sb_q_scale(sb_heads))
            parts = [_sb_attention(qkv.reshape(b, s, -1), sb_heads).reshape(m, -1)]
            w_mix, b_mix = sb_w_o, None
        elif kind == 1:
            qkv = _norm_matmul(xs, mix_norm[i], dil_w_qkv, j, None, F32)
            qkv = qkv.reshape(b, s, -1)
            outs, lses = [], []
            for g, (window, dil) in enumerate(DIL_PATTERNS):
                o, lse = _band_attention(qkv, dil_cos, dil_sin, g, dil_hpg, dil_heads, window, dil)
                outs.append(o)
                lses.append(lse)
            parts = outs + lses
            w_mix, b_mix = dil_w_o, None
        else:
            qkv = _norm_matmul(xs, mix_norm[i], swa_w_qkv, j, swa_b_qkv[j], BF16)
            parts = [_swa_attention(qkv.reshape(b, s, -1), swa_cos, swa_sin, swa_sinks[j],
                                    swa_q_heads).reshape(m, -1)]
            w_mix, b_mix = swa_w_o, swa_b_o[j]
        xs = _mix_xattn(parts, w_mix, j, b_mix, xs, xattn_norm[i], xattn_w_q, kv, xattn_w_o, i, s)

        last = i == depth - 1
        if chain and not last:
            xs, *w1 = _ffn(xs, ffn2_norm[i], *w2, convert=(ffn1_w_gate_up, ffn1_w_down, i + 1))
        else:
            xs = _ffn(xs, ffn2_norm[i], *w2, final_g=final_norm if last else None)
            if not last:
                w1 = cast_ffn(ffn1_w_gate_up, ffn1_w_down, i + 1)

    return xs.reshape(b, s, d)
```

```python
import functools

import jax
import jax.numpy as jnp
from jax import lax
from jax.experimental import pallas as pl
from jax.experimental.pallas import tpu as pltpu

F32 = jnp.float32
BF16 = jnp.bfloat16

NORM_EPS = 1e-6
ROPE_THETA = 500000.0
ROPE_FRACTION = 4
LOG2E = 1.4426950408889634
HEAD_DIM = 128
BLOCK = 128
BAND_BLOCKS_IN_FLIGHT = 8
DIL_PATTERNS = ((128, 1), (512, 4), (2048, 16))
SWA_HEAD_DIM = 64
SWA_GROUP = 8
SWA_WINDOW = 128
XA_HEADS = 4
XA_HEAD_DIM = 128

LANES = 128
V7X_VMEM_BYTES = 64 * 2**20
VMEM_LIMIT_BYTES = (V7X_VMEM_BYTES * 3) // 4
FFN_VMEM_LIMIT_BYTES = (V7X_VMEM_BYTES * 15) // 16
NORM_MATMUL_TILE_BYTES = 18 * 2**20


def _params(*semantics, vmem_limit=VMEM_LIMIT_BYTES):
    return pltpu.CompilerParams(dimension_semantics=semantics, vmem_limit_bytes=vmem_limit)


def _tile(n, cap, mult=LANES):
    if n <= cap:
        return n
    t = (cap // mult) * mult
    while t > mult and n % t:
        t -= mult
    assert n % t == 0, (n, cap, mult)
    return t


def _rms(x, g):
    ms = jnp.mean(x * x, axis=-1, keepdims=True)
    return x * lax.rsqrt(ms + NORM_EPS) * g


def _norm_matmul_kernel(x_ref, g_ref, w_ref, b_ref, s_ref, o_ref, xn_ref):
    def tile(first):
        if first:
            xn = _rms(x_ref[...], g_ref[...]).astype(BF16)
            xn_ref[...] = xn
        else:
            xn = xn_ref[...]
        acc = jnp.dot(xn, w_ref[...], preferred_element_type=F32)
        o_ref[...] = ((acc + b_ref[...]) * s_ref[...]).astype(o_ref.dtype)

    pl.when(pl.program_id(1) == 0)(functools.partial(tile, True))
    pl.when(pl.program_id(1) > 0)(functools.partial(tile, False))


def _norm_matmul(x, g, w, layer, bias, out_dtype, col_scale=None, tm_cap=1024):
    m, d = x.shape
    n = w.shape[2]
    tm = _tile(m, tm_cap, 8)
    col_bytes = 2 * (jnp.dtype(BF16).itemsize * d + jnp.dtype(out_dtype).itemsize * tm)
    tn = _tile(n, NORM_MATMUL_TILE_BYTES // col_bytes // LANES * LANES)
    if bias is None:
        bias = jnp.zeros((n,), F32)
    if col_scale is None:
        col_scale = jnp.ones((n,), F32)
    return pl.pallas_call(
        _norm_matmul_kernel,
        out_shape=jax.ShapeDtypeStruct((m, n), out_dtype),
        grid=(m // tm, n // tn),
        in_specs=[
            pl.BlockSpec((tm, d), lambda i, j: (i, 0)),
            pl.BlockSpec((1, d), lambda i, j: (0, 0)),
            pl.BlockSpec((None, d, tn), lambda i, j: (layer, 0, j)),
            pl.BlockSpec((1, tn), lambda i, j: (0, j)),
            pl.BlockSpec((1, tn), lambda i, j: (0, j)),
        ],
        out_specs=pl.BlockSpec((tm, tn), lambda i, j: (i, j)),
        scratch_shapes=[pltpu.VMEM((tm, d), BF16)],
        compiler_params=_params("parallel", "arbitrary"),
        name="norm_matmul",
    )(x, g.reshape(1, d), w, bias.reshape(1, n), col_scale.reshape(1, n))


def _ffn_kernel(x_ref, g_ref, wg_ref, wu_ref, wd_ref, gf_ref, *rest, nf, final_norm, convert):
    src_refs, o_ref, dst_refs, xn_ref = (rest[:convert], rest[convert],
                                         rest[convert + 1:2 * convert + 1], rest[-1])
    for src, dst in zip(src_refs, dst_refs):
        dst[...] = src[...].astype(BF16)
    j = pl.program_id(1)

    def chunk(first, last):
        if first:
            xn = _rms(x_ref[...], g_ref[...]).astype(BF16)
            xn_ref[...] = xn
        else:
            xn = xn_ref[...]
        gate = jnp.dot(xn, wg_ref[...], preferred_element_type=F32)
        up = jnp.dot(xn, wu_ref[...], preferred_element_type=F32)
        h = gate * jax.nn.sigmoid(gate) * up
        down = jnp.dot(h.astype(BF16), wd_ref[...], preferred_element_type=F32)
        if not first:
            down = o_ref[...] + down
        o_ref[...] = x_ref[...] + 0.5 * down if last else down

    if final_norm:
        pl.when(j == 0)(functools.partial(chunk, True, False))
        pl.when(j > 0)(functools.partial(chunk, False, False))

        @pl.when(j == nf - 1)
        def _():
            rows = min(256, o_ref.shape[0])

            def finish(c, carry):
                sl = pl.ds(pl.multiple_of(c * rows, rows), rows)
                o_ref[sl, :] = _rms(x_ref[sl, :] + 0.5 * o_ref[sl, :], gf_ref[...])
                return carry

            lax.fori_loop(0, o_ref.shape[0] // rows, finish, 0)
    elif nf == 1:
        chunk(True, True)
    else:
        pl.when(j == 0)(functools.partial(chunk, True, False))
        pl.when((j > 0) & (j < nf - 1))(functools.partial(chunk, False, False))
        pl.when(j == nf - 1)(functools.partial(chunk, False, True))


def _ffn_tiles(m, f, tm_cap=1024, tf_cap=512):
    return _tile(m, tm_cap, 8), _tile(f, tf_cap)


def _cast_plan(rows, cols, ni, nf):
    if rows % (ni * 16):
        return None
    for cw in range(LANES, cols + 1, LANES):
        if cols % cw == 0 and cols // cw <= nf:
            return rows // ni, cw, cols // cw
    return None


def _ffn_can_cast(m, f, shapes):
    tm, tf = _ffn_tiles(m, f)
    return all(_cast_plan(r, c, m // tm, f // tf) is not None for r, c in shapes)


def _ffn(x, g, w_gu, w_d, final_g=None, casts=()):
    m, d = x.shape
    f = w_d.shape[0]
    tm, tf = _ffn_tiles(m, f)
    nf = f // tf
    ni = m // tm
    gf = jnp.ones((d,), F32) if final_g is None else final_g
    in_specs = [
        pl.BlockSpec((tm, d), lambda i, j: (i, 0)),
        pl.BlockSpec((1, d), lambda i, j: (0, 0)),
        pl.BlockSpec((d, tf), lambda i, j: (0, j)),
        pl.BlockSpec((d, tf), lambda i, j: (0, j + nf)),
        pl.BlockSpec((tf, d), lambda i, j: (j, 0)),
        pl.BlockSpec((1, d), lambda i, j: (0, 0)),
    ]
    args = [x, g.reshape(1, d), w_gu, w_gu, w_d, gf.reshape(1, d)]
    out_specs = [pl.BlockSpec((tm, d), lambda i, j: (i, 0))]
    out_shape = [jax.ShapeDtypeStruct((m, d), F32)]
    for src, layer in casts:
        rows, cols = src.shape[1:]
        rb, cw, chunks = _cast_plan(rows, cols, ni, nf)
        in_specs.append(pl.BlockSpec(
            (None, rb, cw), lambda i, j, layer=layer, chunks=chunks:
            (layer, i, jnp.minimum(j, chunks - 1))))
        args.append(src)
    for src, layer in casts:
        rows, cols = src.shape[1:]
        rb, cw, chunks = _cast_plan(rows, cols, ni, nf)
        out_specs.append(pl.BlockSpec(
            (rb, cw), lambda i, j, chunks=chunks: (i, jnp.minimum(j, chunks - 1))))
        out_shape.append(jax.ShapeDtypeStruct((rows, cols), BF16))
    outs = pl.pallas_call(
        functools.partial(_ffn_kernel, nf=nf, final_norm=final_g is not None,
                          convert=len(casts)),
        out_shape=out_shape,
        grid=(ni, nf),
        in_specs=in_specs,
        out_specs=out_specs,
        scratch_shapes=[pltpu.VMEM((tm, d), BF16)],
        compiler_params=_params("parallel", "arbitrary", vmem_limit=FFN_VMEM_LIMIT_BYTES),
        name="ffn",
    )(*args)
    return outs[0], list(outs[1:])


def _sb_kernel(q_ref, k_ref, v_ref, tri_ref, o_ref, acc_ref, car_ref, z_ref, *, tq, tk):
    row0 = pl.program_id(2) * tq
    acc_ref[...] = jnp.zeros_like(acc_ref)
    tri = tri_ref[...]
    nsub = tq // tk
    nheads = q_ref.shape[1] // HEAD_DIM
    assert nheads % 2 == 0
    cols = [slice(h * HEAD_DIM, (h + 1) * HEAD_DIM) for h in range(nheads)]

    def logits(h, ks):
        return lax.dot_general(q_ref[:, cols[h]], k_ref[pl.ds(ks, tq), cols[h]],
                               (((1,), (1,)), ((), ())), preferred_element_type=F32)

    def neg_log2_keep(z2):
        neg_abs = lax.bitcast_convert_type(
            lax.bitcast_convert_type(z2, jnp.uint32) | jnp.uint32(0x80000000), F32)
        return jnp.maximum(z2, 0.0) + jnp.log2(1.0 + jnp.exp2(neg_abs))

    def prefetch(h, ks, ks_next):
        if h + 1 < nheads:
            z_ref[(h + 1) % 2] = logits(h + 1, ks)
        else:
            z_ref[0] = logits(0, ks_next)

    def full_tile(ks, ks_next):
        for h, hs in enumerate(cols):
            prefetch(h, ks, ks_next)
            zr = z_ref.at[h % 2]
            nlb = neg_log2_keep(zr[...]).astype(BF16)
            car = car_ref[:, hs]
            parts = [None] * nsub
            for j in reversed(range(nsub)):
                sl = slice(j * tk, (j + 1) * tk)
                ssum = jnp.dot(nlb[:, sl], tri, preferred_element_type=F32)
                parts[j] = jnp.exp2(zr[:, sl] - ssum
                                    - jnp.concatenate([car] * (tk // LANES), axis=1))
                car = car + jnp.broadcast_to(ssum[:, 0:1], (tq, LANES))
            a = jnp.concatenate(parts, axis=1)
            acc_ref[:, hs] += jnp.dot(a.astype(BF16), v_ref[pl.ds(ks, tq), hs],
                                      preferred_element_type=F32)
            car_ref[:, hs] = car

    def diagonal_tile(ks, ks_next):
        lower = (lax.broadcasted_iota(jnp.int32, (tk, tk), 1)
                 < lax.broadcasted_iota(jnp.int32, (tk, tk), 0))
        for h, hs in enumerate(cols):
            prefetch(h, ks, ks_next)
            zr = z_ref.at[h % 2]
            car = jnp.zeros((tq, LANES), F32)
            for j in reversed(range(nsub)):
                r0 = j * tk
                nr = tq - r0
                sl = slice(j * tk, (j + 1) * tk)
                mask = lower if nr == tk else jnp.concatenate(
                    [lower, jnp.ones((nr - tk, tk), jnp.bool_)], axis=0)
                z2 = zr[r0:, sl]
                nl = jnp.where(mask, neg_log2_keep(z2), 0.0)
                ssum = jnp.dot(nl.astype(BF16), tri, preferred_element_type=F32)
                a = jnp.exp2(z2 - ssum - jnp.concatenate([car[r0:]] * (tk // LANES), axis=1))
                a = jnp.where(mask, a, 0.0)
                acc_ref[r0:, hs] += jnp.dot(a.astype(BF16), v_ref[pl.ds(ks + r0, tk), hs],
                                            preferred_element_type=F32)
                upd = car[r0:] + jnp.broadcast_to(ssum[:, 0:1], (nr, LANES))
                car = upd if r0 == 0 else jnp.concatenate([car[:r0], upd], axis=0)
            car_ref[:, hs] = car

    def start(t):
        return pl.multiple_of(jnp.maximum(row0 - (t + 1) * tq, 0), tq)

    z_ref[0] = logits(0, pl.multiple_of(row0, tq))
    diagonal_tile(pl.multiple_of(row0, tq), start(0))

    def body(t, c):
        full_tile(start(t), start(t + 1))
        return c

    lax.fori_loop(0, pl.program_id(2), body, 0)
    o_ref[...] = acc_ref[...].astype(o_ref.dtype)


def _sb_q_scale(heads):
    width = heads * HEAD_DIM
    return jnp.concatenate([jnp.full((width,), HEAD_DIM ** -0.5 * LOG2E, F32),
                            jnp.ones((2 * width,), F32)])


def _sb_attention(qkv, heads, tq_cap=512, tk_cap=256, heads_per_step=4):
    b, s, _ = qkv.shape
    tq = _tile(s, tq_cap)
    tk = _tile(tq, tk_cap)
    hp = heads_per_step if heads % heads_per_step == 0 else 1
    hw = hp * HEAD_DIM
    hb = heads // hp
    tri = jnp.tril(jnp.ones((tk, tk), F32)).astype(BF16)
    kern = functools.partial(_sb_kernel, tq=tq, tk=tk)
    return pl.pallas_call(
        kern,
        out_shape=jax.ShapeDtypeStruct((b, s, heads * HEAD_DIM), BF16),
        grid=(b, hb, s // tq),
        in_specs=[
            pl.BlockSpec((None, tq, hw), lambda bi, h, i: (bi, i, h)),
            pl.BlockSpec((None, s, hw), lambda bi, h, i: (bi, 0, hb + h)),
            pl.BlockSpec((None, s, hw), lambda bi, h, i: (bi, 0, 2 * hb + h)),
            pl.BlockSpec((tk, tk), lambda bi, h, i: (0, 0)),
        ],
        out_specs=pl.BlockSpec((None, tq, hw), lambda bi, h, i: (bi, i, h)),
        scratch_shapes=[pltpu.VMEM((tq, hw), F32), pltpu.VMEM((tq, hw), F32),
                        pltpu.VMEM((2, tq, tq), F32)],
        compiler_params=_params("parallel", "parallel", "arbitrary"),
        name="sb_attention",
    )(qkv, qkv, qkv, tri)


def _rope_table_kernel(pos_ref, freq_ref, sign_ref, cos_ref, sin_ref):
    ang = pos_ref[...].astype(F32) * freq_ref[...]
    cos_ref[...] = jnp.cos(ang)
    sin_ref[...] = jnp.sin(ang) * sign_ref[...]


def _rope_tables(positions, head_dim):
    m = positions.shape[0]
    rot = head_dim // ROPE_FRACTION
    half = rot // 2
    inv_freq = jnp.power(F32(ROPE_THETA), -jnp.arange(half, dtype=F32) * 2.0 / rot)
    lane = jnp.arange(LANES) % head_dim
    freq = jnp.where(lane < rot, inv_freq[lane % half], 0.0).astype(F32)
    sign = jnp.where(lane < half, -1.0, jnp.where(lane < rot, 1.0, 0.0)).astype(F32)
    tm = _tile(m, 2048, 8)
    out = jax.ShapeDtypeStruct((m, LANES), F32)
    return pl.pallas_call(
        _rope_table_kernel,
        out_shape=(out, out),
        grid=(m // tm,),
        in_specs=[
            pl.BlockSpec((tm, 1), lambda i: (i, 0)),
            pl.BlockSpec((1, LANES), lambda i: (0, 0)),
            pl.BlockSpec((1, LANES), lambda i: (0, 0)),
        ],
        out_specs=(pl.BlockSpec((tm, LANES), lambda i: (i, 0)),
                   pl.BlockSpec((tm, LANES), lambda i: (i, 0))),
        compiler_params=_params("parallel"),
        name="rope_tables",
    )(positions, freq.reshape(1, LANES), sign.reshape(1, LANES))


def _rope(x, cos, sin, head_dim):
    half = head_dim // ROPE_FRACTION // 2
    lane = lax.broadcasted_iota(jnp.int32, x.shape, 1) & (head_dim - 1)
    ahead = pltpu.roll(x, LANES - half, 1)
    behind = pltpu.roll(x, half, 1)
    return x * cos + jnp.where(lane < half, ahead, behind) * sin


def _band_kernel(q_ref, k_ref, v_ref, cos_ref, sin_ref, o_ref, lse_ref, qr_ref, kr_ref, vr_ref,
                 *, dil, length, max_dist, scale):
    kj = lax.broadcasted_iota(jnp.int32, (BLOCK, 2 * BLOCK), 1)
    dist = BLOCK + lax.broadcasted_iota(jnp.int32, (BLOCK, 2 * BLOCK), 0) - kj
    band = (dist >= 0) & (dist <= max_dist)
    band0 = band & (kj >= BLOCK)
    nblocks = length // BLOCK
    kr_ref[:, 0:BLOCK, :] = jnp.zeros((kr_ref.shape[0], BLOCK, LANES), BF16)
    vr_ref[:, 0:BLOCK, :] = jnp.zeros((vr_ref.shape[0], BLOCK, LANES), BF16)

    def block(p, slot, a, mask):
        q0 = pl.multiple_of(a * BLOCK, BLOCK)
        q = qr_ref[slot, pl.ds(q0, BLOCK), :]
        k = kr_ref[slot, pl.ds(q0, 2 * BLOCK), :]
        v = vr_ref[slot, pl.ds(q0, 2 * BLOCK), :]
        s = lax.dot_general(q, k, (((1,), (1,)), ((), ())),
                            preferred_element_type=F32) * scale
        s = jnp.where(mask, s, -jnp.inf)
        m = jnp.max(s, axis=-1, keepdims=True)
        e = jnp.exp(s - m)
        l = jnp.sum(e, axis=-1, keepdims=True)
        o = jnp.dot(e.astype(BF16), v, preferred_element_type=F32) / l
        dest = pl.ds(p + q0 * dil, BLOCK, stride=dil)
        o_ref[dest, :] = o
        lse_ref[dest, :] = jnp.broadcast_to(m + jnp.log(l), (BLOCK, LANES))

    ncls = qr_ref.shape[0]
    group = min(BAND_BLOCKS_IN_FLIGHT, nblocks)
    assert nblocks % group == 0 and dil % ncls == 0

    def classes(t, c):
        ps = [t * ncls + slot for slot in range(ncls)]
        for slot, p in enumerate(ps):
            rows = pl.ds(p, length, stride=dil)
            cos = cos_ref[rows, :]
            sin = sin_ref[rows, :]
            qr_ref[slot] = _rope(q_ref[rows, :], cos, sin, HEAD_DIM).astype(BF16)
            kr_ref[slot, BLOCK:, :] = _rope(k_ref[rows, :], cos, sin, HEAD_DIM).astype(BF16)
            vr_ref[slot, BLOCK:, :] = v_ref[rows, :].astype(BF16)
        for slot, p in enumerate(ps):
            block(p, slot, 0, band0)
            for a in range(1, group):
                block(p, slot, a, band)

        def trip(u, c2):
            for slot, p in enumerate(ps):
                for a in range(group):
                    block(p, slot, group * u + a, band)
            return c2

        lax.fori_loop(1, nblocks // group, trip, 0)
        return c

    lax.fori_loop(0, dil // ncls, classes, 0)


def _band_attention(qkv, cos, sin, group, hpg, n_heads, window, dil):
    b, s, n = qkv.shape
    length = s // dil
    assert length % BLOCK == 0 and window % dil == 0 and window // dil <= BLOCK
    h0 = group * hpg

    def col(which):
        return pl.BlockSpec((None, s, LANES), lambda bi, h: (bi, 0, which * n_heads + h0 + h))

    tab = pl.BlockSpec((None, s, LANES), lambda bi, h: (bi, 0, 0))
    out = pl.BlockSpec((None, s, LANES), lambda bi, h: (bi, 0, h))
    shape = jax.ShapeDtypeStruct((b, s, hpg * LANES), F32)
    ncls = min(dil, max(1, (BAND_BLOCKS_IN_FLIGHT * BLOCK) // length))
    kern = functools.partial(_band_kernel, dil=dil, length=length, max_dist=window // dil,
                             scale=HEAD_DIM ** -0.5)
    o, lse = pl.pallas_call(
        kern,
        out_shape=(shape, shape),
        grid=(b, hpg),
        in_specs=[col(0), col(1), col(2), tab, tab],
        out_specs=(out, out),
        scratch_shapes=[pltpu.VMEM((ncls, length, LANES), BF16),
                        pltpu.VMEM((ncls, length + BLOCK, LANES), BF16),
                        pltpu.VMEM((ncls, length + BLOCK, LANES), BF16)],
        compiler_params=_params("parallel", "parallel"),
        name=f"band_attention_d{dil}",
    )(qkv, qkv, qkv, cos, sin)
    return (o.reshape(b * s, hpg * LANES), lse.reshape(b * s, hpg * LANES))


def _swa_kernel(sink_ref, q_ref, kc_ref, kp_ref, vc_ref, vp_ref, cc_ref, cp_ref, sc_ref, sp_ref,
                o_ref, qr_ref, kr_ref, vr_ref, *, tq, scale):
    pair = pl.program_id(1)
    has_prev = pl.program_id(2) > 0
    d = SWA_HEAD_DIM
    cos = cc_ref[...]
    sin = sc_ref[...]
    kx = jnp.concatenate([_rope(kp_ref[...].astype(F32), cp_ref[...], sp_ref[...], d),
                          _rope(kc_ref[...].astype(F32), cos, sin, d)], axis=0)
    vx = jnp.concatenate([vp_ref[...], vc_ref[...]], axis=0).astype(F32)
    low = lax.broadcasted_iota(jnp.int32, kx.shape, 1) < d
    for src, dst in ((kx, kr_ref), (vx, vr_ref)):
        swapped = pltpu.roll(src, d, 1)
        dst[0] = jnp.where(low, src, 0.0).astype(BF16)
        dst[1] = jnp.where(low, 0.0, swapped).astype(BF16)
        dst[2] = jnp.where(low, swapped, 0.0).astype(BF16)
        dst[3] = jnp.where(low, 0.0, src).astype(BF16)
    nblk = q_ref.shape[1] // LANES
    for c in range(nblk):
        sl = slice(c * LANES, (c + 1) * LANES)
        qr_ref[:, sl] = _rope(q_ref[:, sl].astype(F32), cos, sin, d).astype(BF16)

    qi = lax.broadcasted_iota(jnp.int32, (BLOCK, 2 * BLOCK), 0)
    kj = lax.broadcasted_iota(jnp.int32, (BLOCK, 2 * BLOCK), 1)
    dist = BLOCK + qi - kj
    band = (dist >= 0) & (dist <= SWA_WINDOW - 1)
    band0 = band & ((kj >= BLOCK) | has_prev)
    per_kv = nblk // (LANES // d)
    for r in range(tq // BLOCK):
        rows = slice(r * BLOCK, (r + 1) * BLOCK)
        mask = jnp.concatenate([band0 if r == 0 else band] * per_kv, axis=0)
        for kv in range(LANES // d):
            blocks = range(kv * per_kv, (kv + 1) * per_kv)
            q = jnp.concatenate([qr_ref[rows, c * LANES:(c + 1) * LANES] for c in blocks], axis=0)
            out = None
            for half in range(2):
                k = kr_ref[2 * kv + half, r * BLOCK:(r + 2) * BLOCK, :]
                v = vr_ref[2 * kv + half, r * BLOCK:(r + 2) * BLOCK, :]
                sink = jnp.concatenate(
                    [jnp.full((BLOCK, 1), sink_ref[(pair * nblk + c) * 2 + half], F32)
                     for c in blocks], axis=0)
                s = lax.dot_general(q, k, (((1,), (1,)), ((), ())),
                                    preferred_element_type=F32) * scale
                s = jnp.where(mask, s, -jnp.inf)
                m = jnp.maximum(jnp.max(s, axis=-1, keepdims=True), sink)
                p = jnp.exp(s - m)
                l = jnp.sum(p, axis=-1, keepdims=True) + jnp.exp(sink - m)
                o = jnp.dot(p.astype(BF16), v, preferred_element_type=F32) / l
                out = o if out is None else out + o
            for n, c in enumerate(blocks):
                o_ref[rows, c * LANES:(c + 1) * LANES] = (
                    out[n * BLOCK:(n + 1) * BLOCK].astype(o_ref.dtype))


def _swa_attention(qkv, cos, sin, sinks, q_heads, tq_cap=512):
    b, s, _ = qkv.shape
    d = SWA_HEAD_DIM
    kv_heads = q_heads // SWA_GROUP
    per_blk = LANES // d
    assert kv_heads % per_blk == 0
    nq = q_heads * d
    qw = per_blk * SWA_GROUP * d
    kcol = nq // LANES
    vcol = (nq + kv_heads * d) // LANES
    tq = _tile(s, tq_cap)
    rb = tq // BLOCK

    def cur(col0):
        return pl.BlockSpec((None, tq, LANES), lambda bi, p, i: (bi, i, col0 + p))

    def prev(col0):
        return pl.BlockSpec((None, BLOCK, LANES),
                            lambda bi, p, i: (bi, jnp.maximum(i * rb - 1, 0), col0 + p))

    tab_c = pl.BlockSpec((None, tq, LANES), lambda bi, p, i: (bi, i, 0))
    tab_p = pl.BlockSpec((None, BLOCK, LANES), lambda bi, p, i: (bi, jnp.maximum(i * rb - 1, 0), 0))
    kern = functools.partial(_swa_kernel, tq=tq, scale=d ** -0.5)
    return pl.pallas_call(
        kern,
        out_shape=jax.ShapeDtypeStruct((b, s, nq), BF16),
        grid=(b, kv_heads // per_blk, s // tq),
        in_specs=[
            pl.BlockSpec(memory_space=pltpu.SMEM),
            pl.BlockSpec((None, tq, qw), lambda bi, p, i: (bi, i, p)),
            cur(kcol), prev(kcol), cur(vcol), prev(vcol),
            tab_c, tab_p, tab_c, tab_p,
        ],
        out_specs=pl.BlockSpec((None, tq, qw), lambda bi, p, i: (bi, i, p)),
        scratch_shapes=[pltpu.VMEM((tq, qw), BF16),
                        pltpu.VMEM((2 * per_blk, tq + BLOCK, LANES), BF16),
                        pltpu.VMEM((2 * per_blk, tq + BLOCK, LANES), BF16)],
        compiler_params=_params("parallel", "parallel", "parallel"),
        name="swa_attention",
    )(sinks, qkv, qkv, qkv, qkv, qkv, cos, cos, sin, sin)


def _mix_xattn_kernel(*refs, groups, scale):
    if groups:
        o_refs, l_refs = refs[:groups], refs[groups:2 * groups]
        (wm_ref, bm_ref, x_ref, g_ref, wq_ref, kv_ref, wo_ref, out_ref,
         att_ref, a_ref) = refs[2 * groups:]
        lses = [r[...] for r in l_refs]
        mx = functools.reduce(jnp.maximum, lses)
        es = [jnp.exp(l - mx) for l in lses]
        inv = 1.0 / functools.reduce(lambda u, v: u + v, es)
        gw = o_refs[0].shape[1]
        for gi in range(groups):
            a_ref[:, gi * gw:(gi + 1) * gw] = (o_refs[gi][...] * (es[gi] * inv)).astype(BF16)
        a = a_ref[...]
    else:
        a_in, wm_ref, bm_ref, x_ref, g_ref, wq_ref, kv_ref, wo_ref, out_ref, att_ref = refs
        a = a_in[...]
    x1 = x_ref[...] + jnp.dot(a, wm_ref[...], preferred_element_type=F32) + bm_ref[...]
    q = jnp.dot(_rms(x1, g_ref[...]).astype(BF16), wq_ref[...],
                preferred_element_type=F32).astype(BF16)
    width = XA_HEADS * XA_HEAD_DIM
    for h in range(XA_HEADS):
        sl = slice(h * XA_HEAD_DIM, (h + 1) * XA_HEAD_DIM)
        k = kv_ref[:, sl]
        v = kv_ref[:, width + h * XA_HEAD_DIM:width + (h + 1) * XA_HEAD_DIM]
        s = lax.dot_general(q[:, sl], k, (((1,), (1,)), ((), ())),
                            preferred_element_type=F32) * scale
        m = jnp.max(s, axis=-1, keepdims=True)
        p = jnp.exp(s - m)
        l = jnp.sum(p, axis=-1, keepdims=True)
        att_ref[:, sl] = (jnp.dot(p.astype(BF16), v, preferred_element_type=F32) / l).astype(BF16)
    out_ref[...] = x1 + jnp.dot(att_ref[...], wo_ref[...], preferred_element_type=F32)


def _mix_xattn(parts, w_mix, mix_layer, bias, x, g, w_q, kv, w_o, layer, seq, tm_cap=512):
    m, d = x.shape
    n_mem = kv.shape[1]
    k = w_mix.shape[1]
    width = XA_HEADS * XA_HEAD_DIM
    groups = len(parts) // 2 if len(parts) > 1 else 0
    tm = _tile(seq, tm_cap, 8)
    per_batch = seq // tm
    if bias is None:
        bias = jnp.zeros((d,), F32)
    once = pl.Buffered(1)
    in_specs = [pl.BlockSpec((tm, p.shape[1]), lambda i: (i, 0)) for p in parts] + [
        pl.BlockSpec((None, k, d), lambda i: (mix_layer, 0, 0), pipeline_mode=once),
        pl.BlockSpec((1, d), lambda i: (0, 0)),
        pl.BlockSpec((tm, d), lambda i: (i, 0)),
        pl.BlockSpec((1, d), lambda i: (0, 0)),
        pl.BlockSpec((None, d, width), lambda i: (layer, 0, 0), pipeline_mode=once),
        pl.BlockSpec((None, n_mem, 2 * width), lambda i: (i // per_batch, 0, 0)),
        pl.BlockSpec((None, width, d), lambda i: (layer, 0, 0), pipeline_mode=once),
    ]
    scratch = [pltpu.VMEM((tm, width), BF16)]
    if groups:
        scratch.append(pltpu.VMEM((tm, k), BF16))
    return pl.pallas_call(
        functools.partial(_mix_xattn_kernel, groups=groups, scale=XA_HEAD_DIM ** -0.5),
        out_shape=jax.ShapeDtypeStruct((m, d), F32),
        grid=(m // tm,),
        in_specs=in_specs,
        out_specs=pl.BlockSpec((tm, d), lambda i: (i, 0)),
        scratch_shapes=scratch,
        compiler_params=_params("parallel"),
        name="mix_xattn",
    )(*parts, w_mix, bias.reshape(1, d), x, g.reshape(1, d), w_q, kv, w_o)


def kernel(x, mem, positions, ffn1_norm, ffn1_w_gate_up, ffn1_w_down, mix_norm, sb_w_qkv, sb_w_o, dil_w_qkv, dil_w_o, swa_w_qkv, swa_b_qkv, swa_sinks, swa_w_o, swa_b_o, xattn_norm, mem_norm, xattn_w_q, xattn_w_kv, xattn_w_o, ffn2_norm, ffn2_w_gate_up, ffn2_w_down, final_norm):
    b, s, d = x.shape
    m = b * s
    depth = ffn1_norm.shape[0]
    n_mem = mem.shape[1]
    sb_heads = d // HEAD_DIM
    dil_hpg = d // (4 * HEAD_DIM)
    dil_heads = dil_hpg * len(DIL_PATTERNS)
    swa_q_heads = d // SWA_HEAD_DIM

    xs = x.reshape(m, d)
    mem2 = mem.reshape(b * n_mem, d)
    pos = positions.reshape(m, 1)
    if depth > 1:
        dil_cos, dil_sin = (t.reshape(b, s, LANES) for t in _rope_tables(pos, HEAD_DIM))
    if depth > 2:
        swa_cos, swa_sin = (t.reshape(b, s, LANES) for t in _rope_tables(pos, SWA_HEAD_DIM))

    mixer_weights = ((sb_w_qkv, sb_w_o), (dil_w_qkv, dil_w_o), (swa_w_qkv, swa_w_o))

    def after_ffn1(i):
        return ([(ffn2_w_gate_up, i), (ffn2_w_down, i)]
                + [(w, i // 3) for w in mixer_weights[i % 3]]
                + [(xattn_w_q, i), (xattn_w_kv, i), (xattn_w_o, i)])

    def cast_up_front(pairs):
        return [w[layer].astype(BF16) for w, layer in pairs]

    shapes = {w.shape[1:] for i in range(depth) for w, _ in after_ffn1(i)}
    shapes |= {ffn1_w_gate_up.shape[1:], ffn1_w_down.shape[1:]}
    chain = _ffn_can_cast(m, ffn1_w_down.shape[1], shapes)

    w1 = cast_up_front([(ffn1_w_gate_up, 0), (ffn1_w_down, 0)])
    for i in range(depth):
        if chain:
            xs, cast = _ffn(xs, ffn1_norm[i], *w1, casts=after_ffn1(i))
        else:
            xs, _ = _ffn(xs, ffn1_norm[i], *w1)
            cast = cast_up_front(after_ffn1(i))
        w2 = cast[:2]
        w_qkv, w_mix, xw_q, xw_kv, xw_o = (w[None] for w in cast[2:])

        kv = _norm_matmul(mem2, mem_norm[i], xw_kv, 0, None, BF16).reshape(b, n_mem, -1)
        kind, j = i % 3, i // 3
        b_mix = None
        if kind == 0:
            qkv = _norm_matmul(xs, mix_norm[i], w_qkv, 0, None, BF16,
                               col_scale=_sb_q_scale(sb_heads))
            parts = [_sb_attention(qkv.reshape(b, s, -1), sb_heads).reshape(m, -1)]
        elif kind == 1:
            qkv = _norm_matmul(xs, mix_norm[i], w_qkv, 0, None, F32).reshape(b, s, -1)
            outs, lses = [], []
            for g, (window, dil) in enumerate(DIL_PATTERNS):
                o, lse = _band_attention(qkv, dil_cos, dil_sin, g, dil_hpg, dil_heads, window, dil)
                outs.append(o)
                lses.append(lse)
            parts = outs + lses
        else:
            qkv = _norm_matmul(xs, mix_norm[i], w_qkv, 0, swa_b_qkv[j], BF16)
            parts = [_swa_attention(qkv.reshape(b, s, -1), swa_cos, swa_sin, swa_sinks[j],
                                    swa_q_heads).reshape(m, -1)]
            b_mix = swa_b_o[j]
        xs = _mix_xattn(parts, w_mix, 0, b_mix, xs, xattn_norm[i], xw_q, kv, xw_o, 0, s)

        if i == depth - 1:
            xs, _ = _ffn(xs, ffn2_norm[i], *w2, final_g=final_norm)
        elif chain:
            xs, w1 = _ffn(xs, ffn2_norm[i], *w2,
                          casts=[(ffn1_w_gate_up, i + 1), (ffn1_w_down, i + 1)])
        else:
            xs, _ = _ffn(xs, ffn2_norm[i], *w2)
            w1 = cast_up_front([(ffn1_w_gate_up, i + 1), (ffn1_w_down, i + 1)])

    return xs.reshape(b, s, d)
```

```python
import functools

import jax
import jax.numpy as jnp
from jax import lax
from jax.experimental import pallas as pl
from jax.experimental.pallas import tpu as pltpu

F32 = jnp.float32
BF16 = jnp.bfloat16

NORM_EPS = 1e-6
ROPE_THETA = 500000.0
ROPE_FRACTION = 4
LOG2E = 1.4426950408889634
HEAD_DIM = 128
BLOCK = 128
BAND_BLOCKS_IN_FLIGHT = 16
DIL_PATTERNS = ((128, 1), (512, 4), (2048, 16))
SWA_HEAD_DIM = 64
SWA_GROUP = 8
SWA_WINDOW = 128
XA_HEADS = 4
XA_HEAD_DIM = 128

LANES = 128
V7X_VMEM_BYTES = 64 * 2**20
VMEM_LIMIT_BYTES = (V7X_VMEM_BYTES * 3) // 4
FFN_VMEM_LIMIT_BYTES = (V7X_VMEM_BYTES * 15) // 16
NORM_MATMUL_TILE_BYTES = 18 * 2**20


def _params(*semantics, vmem_limit=VMEM_LIMIT_BYTES):
    return pltpu.CompilerParams(dimension_semantics=semantics, vmem_limit_bytes=vmem_limit)


def _tile(n, cap, mult=LANES):
    if n <= cap:
        return n
    t = (cap // mult) * mult
    while t > mult and n % t:
        t -= mult
    assert n % t == 0, (n, cap, mult)
    return t


def _rms(x, g):
    ms = jnp.mean(x * x, axis=-1, keepdims=True)
    return x * lax.rsqrt(ms + NORM_EPS) * g


def _norm_matmul_kernel(x_ref, g_ref, w_ref, b_ref, s_ref, o_ref, xn_ref):
    def tile(first):
        if first:
            xn = _rms(x_ref[...], g_ref[...]).astype(BF16)
            xn_ref[...] = xn
        else:
            xn = xn_ref[...]
        acc = jnp.dot(xn, w_ref[...], preferred_element_type=F32)
        o_ref[...] = ((acc + b_ref[...]) * s_ref[...]).astype(o_ref.dtype)

    pl.when(pl.program_id(1) == 0)(functools.partial(tile, True))
    pl.when(pl.program_id(1) > 0)(functools.partial(tile, False))


def _norm_matmul(x, g, w, layer, bias, out_dtype, col_scale=None, tm_cap=1024):
    m, d = x.shape
    n = w.shape[2]
    tm = _tile(m, tm_cap, 8)
    col_bytes = 2 * (jnp.dtype(BF16).itemsize * d + jnp.dtype(out_dtype).itemsize * tm)
    tn = _tile(n, NORM_MATMUL_TILE_BYTES // col_bytes // LANES * LANES)
    if bias is None:
        bias = jnp.zeros((n,), F32)
    if col_scale is None:
        col_scale = jnp.ones((n,), F32)
    return pl.pallas_call(
        _norm_matmul_kernel,
        out_shape=jax.ShapeDtypeStruct((m, n), out_dtype),
        grid=(m // tm, n // tn),
        in_specs=[
            pl.BlockSpec((tm, d), lambda i, j: (i, 0)),
            pl.BlockSpec((1, d), lambda i, j: (0, 0)),
            pl.BlockSpec((None, d, tn), lambda i, j: (layer, 0, j)),
            pl.BlockSpec((1, tn), lambda i, j: (0, j)),
            pl.BlockSpec((1, tn), lambda i, j: (0, j)),
        ],
        out_specs=pl.BlockSpec((tm, tn), lambda i, j: (i, j)),
        scratch_shapes=[pltpu.VMEM((tm, d), BF16)],
        compiler_params=_params("parallel", "arbitrary"),
        name="norm_matmul",
    )(x, g.reshape(1, d), w, bias.reshape(1, n), col_scale.reshape(1, n))


def _ffn_kernel(x_ref, g_ref, wg_ref, wu_ref, wd_ref, gf_ref, *rest, nf, final_norm, convert):
    src_refs, o_ref, dst_refs, xn_ref = (rest[:convert], rest[convert],
                                         rest[convert + 1:2 * convert + 1], rest[-1])
    for src, dst in zip(src_refs, dst_refs):
        dst[...] = src[...].astype(BF16)
    j = pl.program_id(1)

    def chunk(first, last):
        if first:
            xn = _rms(x_ref[...], g_ref[...]).astype(BF16)
            xn_ref[...] = xn
        else:
            xn = xn_ref[...]
        gate = jnp.dot(xn, wg_ref[...], preferred_element_type=F32)
        up = jnp.dot(xn, wu_ref[...], preferred_element_type=F32)
        h = gate * jax.nn.sigmoid(gate) * up
        down = jnp.dot(h.astype(BF16), wd_ref[...], preferred_element_type=F32)
        if not first:
            down = o_ref[...] + down
        o_ref[...] = x_ref[...] + 0.5 * down if last else down

    if final_norm:
        pl.when(j == 0)(functools.partial(chunk, True, False))
        pl.when(j > 0)(functools.partial(chunk, False, False))

        @pl.when(j == nf - 1)
        def _():
            rows = min(256, o_ref.shape[0])

            def finish(c, carry):
                sl = pl.ds(pl.multiple_of(c * rows, rows), rows)
                o_ref[sl, :] = _rms(x_ref[sl, :] + 0.5 * o_ref[sl, :], gf_ref[...])
                return carry

            lax.fori_loop(0, o_ref.shape[0] // rows, finish, 0)
    elif nf == 1:
        chunk(True, True)
    else:
        pl.when(j == 0)(functools.partial(chunk, True, False))
        pl.when((j > 0) & (j < nf - 1))(functools.partial(chunk, False, False))
        pl.when(j == nf - 1)(functools.partial(chunk, False, True))


def _ffn_tiles(m, f, tm_cap=1024, tf_cap=512):
    return _tile(m, tm_cap, 8), _tile(f, tf_cap)


def _cast_plan(rows, cols, ni, nf):
    if rows % (ni * 16):
        return None
    for cw in range(LANES, cols + 1, LANES):
        if cols % cw == 0 and cols // cw <= nf:
            return rows // ni, cw, cols // cw
    return None


def _ffn_can_cast(m, f, shapes):
    tm, tf = _ffn_tiles(m, f)
    return all(_cast_plan(r, c, m // tm, f // tf) is not None for r, c in shapes)


def _ffn(x, g, w_gu, w_d, final_g=None, casts=()):
    m, d = x.shape
    f = w_d.shape[0]
    tm, tf = _ffn_tiles(m, f)
    nf = f // tf
    ni = m // tm
    gf = jnp.ones((d,), F32) if final_g is None else final_g
    in_specs = [
        pl.BlockSpec((tm, d), lambda i, j: (i, 0)),
        pl.BlockSpec((1, d), lambda i, j: (0, 0)),
        pl.BlockSpec((d, tf), lambda i, j: (0, j)),
        pl.BlockSpec((d, tf), lambda i, j: (0, j + nf)),
        pl.BlockSpec((tf, d), lambda i, j: (j, 0)),
        pl.BlockSpec((1, d), lambda i, j: (0, 0)),
    ]
    args = [x, g.reshape(1, d), w_gu, w_gu, w_d, gf.reshape(1, d)]
    out_specs = [pl.BlockSpec((tm, d), lambda i, j: (i, 0))]
    out_shape = [jax.ShapeDtypeStruct((m, d), F32)]
    for src, layer in casts:
        rows, cols = src.shape[1:]
        rb, cw, chunks = _cast_plan(rows, cols, ni, nf)
        in_specs.append(pl.BlockSpec(
            (None, rb, cw), lambda i, j, layer=layer, chunks=chunks:
            (layer, i, jnp.minimum(j, chunks - 1))))
        args.append(src)
    for src, layer in casts:
        rows, cols = src.shape[1:]
        rb, cw, chunks = _cast_plan(rows, cols, ni, nf)
        out_specs.append(pl.BlockSpec(
            (rb, cw), lambda i, j, chunks=chunks: (i, jnp.minimum(j, chunks - 1))))
        out_shape.append(jax.ShapeDtypeStruct((rows, cols), BF16))
    outs = pl.pallas_call(
        functools.partial(_ffn_kernel, nf=nf, final_norm=final_g is not None,
                          convert=len(casts)),
        out_shape=out_shape,
        grid=(ni, nf),
        in_specs=in_specs,
        out_specs=out_specs,
        scratch_shapes=[pltpu.VMEM((tm, d), BF16)],
        compiler_params=_params("parallel", "arbitrary", vmem_limit=FFN_VMEM_LIMIT_BYTES),
        name="ffn",
    )(*args)
    return outs[0], list(outs[1:])


def _sb_kernel(q_ref, k_ref, v_ref, tri_ref, o_ref, acc_ref, car_ref, z_ref, *, tq, tk):
    row0 = pl.program_id(2) * tq
    acc_ref[...] = jnp.zeros_like(acc_ref)
    tri = tri_ref[...]
    nsub = tq // tk
    nheads = q_ref.shape[1] // HEAD_DIM
    assert nheads % 2 == 0
    cols = [slice(h * HEAD_DIM, (h + 1) * HEAD_DIM) for h in range(nheads)]

    def logits(h, ks):
        return lax.dot_general(q_ref[:, cols[h]], k_ref[pl.ds(ks, tq), cols[h]],
                               (((1,), (1,)), ((), ())), preferred_element_type=F32)

    def neg_log2_keep(z2):
        neg_abs = lax.bitcast_convert_type(
            lax.bitcast_convert_type(z2, jnp.uint32) | jnp.uint32(0x80000000), F32)
        return jnp.maximum(z2, 0.0) + jnp.log2(1.0 + jnp.exp2(neg_abs))

    def prefetch(h, ks, ks_next):
        if h + 1 < nheads:
            z_ref[(h + 1) % 2] = logits(h + 1, ks)
        else:
            z_ref[0] = logits(0, ks_next)

    def full_tile(ks, ks_next):
        for h, hs in enumerate(cols):
            prefetch(h, ks, ks_next)
            zr = z_ref.at[h % 2]
            nlb = neg_log2_keep(zr[...]).astype(BF16)
            car = car_ref[:, hs]
            parts = [None] * nsub
            for j in reversed(range(nsub)):
                sl = slice(j * tk, (j + 1) * tk)
                ssum = jnp.dot(nlb[:, sl], tri, preferred_element_type=F32)
                parts[j] = jnp.exp2(zr[:, sl] - ssum
                                    - jnp.concatenate([car] * (tk // LANES), axis=1))
                car = car + jnp.broadcast_to(ssum[:, 0:1], (tq, LANES))
            a = jnp.concatenate(parts, axis=1)
            acc_ref[:, hs] += jnp.dot(a.astype(BF16), v_ref[pl.ds(ks, tq), hs],
                                      preferred_element_type=F32)
            car_ref[:, hs] = car

    def diagonal_tile(ks, ks_next):
        lower = (lax.broadcasted_iota(jnp.int32, (tk, tk), 1)
                 < lax.broadcasted_iota(jnp.int32, (tk, tk), 0))
        for h, hs in enumerate(cols):
            prefetch(h, ks, ks_next)
            zr = z_ref.at[h % 2]
            car = jnp.zeros((tq, LANES), F32)
            for j in reversed(range(nsub)):
                r0 = j * tk
                nr = tq - r0
                sl = slice(j * tk, (j + 1) * tk)
                mask = lower if nr == tk else jnp.concatenate(
                    [lower, jnp.ones((nr - tk, tk), jnp.bool_)], axis=0)
                z2 = zr[r0:, sl]
                nl = jnp.where(mask, neg_log2_keep(z2), 0.0)
                ssum = jnp.dot(nl.astype(BF16), tri, preferred_element_type=F32)
                a = jnp.exp2(z2 - ssum - jnp.concatenate([car[r0:]] * (tk // LANES), axis=1))
                a = jnp.where(mask, a, 0.0)
                acc_ref[r0:, hs] += jnp.dot(a.astype(BF16), v_ref[pl.ds(ks + r0, tk), hs],
                                            preferred_element_type=F32)
                upd = car[r0:] + jnp.broadcast_to(ssum[:, 0:1], (nr, LANES))
                car = upd if r0 == 0 else jnp.concatenate([car[:r0], upd], axis=0)
            car_ref[:, hs] = car

    def start(t):
        return pl.multiple_of(jnp.maximum(row0 - (t + 1) * tq, 0), tq)

    z_ref[0] = logits(0, pl.multiple_of(row0, tq))
    diagonal_tile(pl.multiple_of(row0, tq), start(0))

    def body(t, c):
        full_tile(start(t), start(t + 1))
        return c

    lax.fori_loop(0, pl.program_id(2), body, 0)
    o_ref[...] = acc_ref[...].astype(o_ref.dtype)


def _sb_q_scale(heads):
    width = heads * HEAD_DIM
    return jnp.concatenate([jnp.full((width,), HEAD_DIM ** -0.5 * LOG2E, F32),
                            jnp.ones((2 * width,), F32)])


def _sb_attention(qkv, heads, tq_cap=512, tk_cap=256, heads_per_step=4):
    b, s, _ = qkv.shape
    tq = _tile(s, tq_cap)
    tk = _tile(tq, tk_cap)
    hp = heads_per_step if heads % heads_per_step == 0 else 1
    hw = hp * HEAD_DIM
    hb = heads // hp
    tri = jnp.tril(jnp.ones((tk, tk), F32)).astype(BF16)
    kern = functools.partial(_sb_kernel, tq=tq, tk=tk)
    return pl.pallas_call(
        kern,
        out_shape=jax.ShapeDtypeStruct((b, s, heads * HEAD_DIM), BF16),
        grid=(b, hb, s // tq),
        in_specs=[
            pl.BlockSpec((None, tq, hw), lambda bi, h, i: (bi, i, h)),
            pl.BlockSpec((None, s, hw), lambda bi, h, i: (bi, 0, hb + h)),
            pl.BlockSpec((None, s, hw), lambda bi, h, i: (bi, 0, 2 * hb + h)),
            pl.BlockSpec((tk, tk), lambda bi, h, i: (0, 0)),
        ],
        out_specs=pl.BlockSpec((None, tq, hw), lambda bi, h, i: (bi, i, h)),
        scratch_shapes=[pltpu.VMEM((tq, hw), F32), pltpu.VMEM((tq, hw), F32),
                        pltpu.VMEM((2, tq, tq), F32)],
        compiler_params=_params("parallel", "parallel", "arbitrary"),
        name="sb_attention",
    )(qkv, qkv, qkv, tri)


def _rope_table_kernel(pos_ref, freq_ref, sign_ref, cos_ref, sin_ref):
    ang = pos_ref[...].astype(F32) * freq_ref[...]
    cos_ref[...] = jnp.cos(ang)
    sin_ref[...] = jnp.sin(ang) * sign_ref[...]


def _rope_tables(positions, head_dim):
    m = positions.shape[0]
    rot = head_dim // ROPE_FRACTION
    half = rot // 2
    inv_freq = jnp.power(F32(ROPE_THETA), -jnp.arange(half, dtype=F32) * 2.0 / rot)
    lane = jnp.arange(LANES) % head_dim
    freq = jnp.where(lane < rot, inv_freq[lane % half], 0.0).astype(F32)
    sign = jnp.where(lane < half, -1.0, jnp.where(lane < rot, 1.0, 0.0)).astype(F32)
    tm = _tile(m, 2048, 8)
    out = jax.ShapeDtypeStruct((m, LANES), F32)
    return pl.pallas_call(
        _rope_table_kernel,
        out_shape=(out, out),
        grid=(m // tm,),
        in_specs=[
            pl.BlockSpec((tm, 1), lambda i: (i, 0)),
            pl.BlockSpec((1, LANES), lambda i: (0, 0)),
            pl.BlockSpec((1, LANES), lambda i: (0, 0)),
        ],
        out_specs=(pl.BlockSpec((tm, LANES), lambda i: (i, 0)),
                   pl.BlockSpec((tm, LANES), lambda i: (i, 0))),
        compiler_params=_params("parallel"),
        name="rope_tables",
    )(positions, freq.reshape(1, LANES), sign.reshape(1, LANES))


def _rope_partner_matrix(head_dim):
    rot = head_dim // ROPE_FRACTION
    half = rot // 2
    lane = jnp.arange(LANES)
    inner = lane % head_dim
    partner = jnp.where(inner < half, lane + half, lane - half)
    return ((lane[:, None] == partner[None, :]) & (inner < rot)[None, :]).astype(BF16)


def _rope_mxu(x, cos, sin, perm):
    hi = x.astype(BF16)
    lo = (x - hi.astype(F32)).astype(BF16)
    partner = (jnp.dot(hi, perm, preferred_element_type=F32)
               + jnp.dot(lo, perm, preferred_element_type=F32))
    return x * cos + partner * sin


def _rope_roll(x, cos, sin, head_dim):
    half = head_dim // ROPE_FRACTION // 2
    lane = lax.broadcasted_iota(jnp.int32, x.shape, 1) & (head_dim - 1)
    ahead = pltpu.roll(x, LANES - half, 1)
    behind = pltpu.roll(x, half, 1)
    return x * cos + jnp.where(lane < half, ahead, behind) * sin


def _band_kernel(q_ref, k_ref, v_ref, cos_ref, sin_ref, perm_ref, o_ref, lse_ref,
                 qr_ref, kr_ref, vr_ref, *, dil, length, max_dist, scale):
    perm = perm_ref[...]
    kj = lax.broadcasted_iota(jnp.int32, (BLOCK, 2 * BLOCK), 1)
    dist = BLOCK + lax.broadcasted_iota(jnp.int32, (BLOCK, 2 * BLOCK), 0) - kj
    band = (dist >= 0) & (dist <= max_dist)
    band0 = band & (kj >= BLOCK)
    nblocks = length // BLOCK
    kr_ref[:, 0:BLOCK, :] = jnp.zeros((kr_ref.shape[0], BLOCK, LANES), BF16)
    vr_ref[:, 0:BLOCK, :] = jnp.zeros((vr_ref.shape[0], BLOCK, LANES), BF16)

    def block(p, slot, a, mask):
        q0 = pl.multiple_of(a * BLOCK, BLOCK)
        q = qr_ref[slot, pl.ds(q0, BLOCK), :]
        k = kr_ref[slot, pl.ds(q0, 2 * BLOCK), :]
        v = vr_ref[slot, pl.ds(q0, 2 * BLOCK), :]
        s = lax.dot_general(q, k, (((1,), (1,)), ((), ())),
                            preferred_element_type=F32) * scale
        s = jnp.where(mask, s, -jnp.inf)
        m = jnp.max(s, axis=-1, keepdims=True)
        e = jnp.exp(s - m)
        l = jnp.sum(e, axis=-1, keepdims=True)
        o = jnp.dot(e.astype(BF16), v, preferred_element_type=F32) / l
        dest = pl.ds(p + q0 * dil, BLOCK, stride=dil)
        o_ref[dest, :] = o
        lse_ref[dest, :] = jnp.broadcast_to(m + jnp.log(l), (BLOCK, LANES))

    ncls = qr_ref.shape[0]
    group = min(BAND_BLOCKS_IN_FLIGHT, nblocks)
    assert nblocks % group == 0 and dil % ncls == 0

    def classes(t, c):
        ps = [t * ncls + slot for slot in range(ncls)]
        for slot, p in enumerate(ps):
            rows = pl.ds(p, length, stride=dil)
            cos = cos_ref[rows, :]
            sin = sin_ref[rows, :]
            qr_ref[slot] = _rope_mxu(q_ref[rows, :], cos, sin, perm).astype(BF16)
            kr_ref[slot, BLOCK:, :] = _rope_mxu(k_ref[rows, :], cos, sin, perm).astype(BF16)
            vr_ref[slot, BLOCK:, :] = v_ref[rows, :].astype(BF16)
        for slot, p in enumerate(ps):
            block(p, slot, 0, band0)
            for a in range(1, group):
                block(p, slot, a, band)

        def trip(u, c2):
            for slot, p in enumerate(ps):
                for a in range(group):
                    block(p, slot, group * u + a, band)
            return c2

        lax.fori_loop(1, nblocks // group, trip, 0)
        return c

    lax.fori_loop(0, dil // ncls, classes, 0)


def _band_attention(qkv, cos, sin, group, hpg, n_heads, window, dil):
    b, s, n = qkv.shape
    length = s // dil
    assert length % BLOCK == 0 and window % dil == 0 and window // dil <= BLOCK
    h0 = group * hpg

    def col(which):
        return pl.BlockSpec((None, s, LANES), lambda bi, h: (bi, 0, which * n_heads + h0 + h))

    tab = pl.BlockSpec((None, s, LANES), lambda bi, h: (bi, 0, 0))
    out = pl.BlockSpec((None, s, LANES), lambda bi, h: (bi, 0, h))
    shape = jax.ShapeDtypeStruct((b, s, hpg * LANES), F32)
    ncls = min(dil, max(1, (BAND_BLOCKS_IN_FLIGHT * BLOCK) // length))
    kern = functools.partial(_band_kernel, dil=dil, length=length, max_dist=window // dil,
                             scale=HEAD_DIM ** -0.5)
    o, lse = pl.pallas_call(
        kern,
        out_shape=(shape, shape),
        grid=(b, hpg),
        in_specs=[col(0), col(1), col(2), tab, tab,
                  pl.BlockSpec((LANES, LANES), lambda bi, h: (0, 0))],
        out_specs=(out, out),
        scratch_shapes=[pltpu.VMEM((ncls, length, LANES), BF16),
                        pltpu.VMEM((ncls, length + BLOCK, LANES), BF16),
                        pltpu.VMEM((ncls, length + BLOCK, LANES), BF16)],
        compiler_params=_params("parallel", "parallel"),
        name=f"band_attention_d{dil}",
    )(qkv, qkv, qkv, cos, sin, _rope_partner_matrix(HEAD_DIM))
    return (o.reshape(b * s, hpg * LANES), lse.reshape(b * s, hpg * LANES))


def _swa_kernel(sink_ref, q_ref, kc_ref, kp_ref, vc_ref, vp_ref, cc_ref, cp_ref, sc_ref, sp_ref,
                o_ref, qr_ref, kr_ref, vr_ref, *, tq, scale):
    pair = pl.program_id(1)
    has_prev = pl.program_id(2) > 0
    d = SWA_HEAD_DIM
    cos = cc_ref[...]
    sin = sc_ref[...]
    kx = jnp.concatenate([_rope_roll(kp_ref[...].astype(F32), cp_ref[...], sp_ref[...], d),
                          _rope_roll(kc_ref[...].astype(F32), cos, sin, d)], axis=0)
    vx = jnp.concatenate([vp_ref[...], vc_ref[...]], axis=0).astype(F32)
    low = lax.broadcasted_iota(jnp.int32, kx.shape, 1) < d
    for src, dst in ((kx, kr_ref), (vx, vr_ref)):
        swapped = pltpu.roll(src, d, 1)
        dst[0] = jnp.where(low, src, 0.0).astype(BF16)
        dst[1] = jnp.where(low, 0.0, swapped).astype(BF16)
        dst[2] = jnp.where(low, swapped, 0.0).astype(BF16)
        dst[3] = jnp.where(low, 0.0, src).astype(BF16)
    nblk = q_ref.shape[1] // LANES
    for c in range(nblk):
        sl = slice(c * LANES, (c + 1) * LANES)
        qr_ref[:, sl] = _rope_roll(q_ref[:, sl].astype(F32), cos, sin, d).astype(BF16)

    qi = lax.broadcasted_iota(jnp.int32, (BLOCK, 2 * BLOCK), 0)
    kj = lax.broadcasted_iota(jnp.int32, (BLOCK, 2 * BLOCK), 1)
    dist = BLOCK + qi - kj
    band = (dist >= 0) & (dist <= SWA_WINDOW - 1)
    band0 = band & ((kj >= BLOCK) | has_prev)
    per_kv = nblk // (LANES // d)
    for r in range(tq // BLOCK):
        rows = slice(r * BLOCK, (r + 1) * BLOCK)
        mask = jnp.concatenate([band0 if r == 0 else band] * per_kv, axis=0)
        for kv in range(LANES // d):
            blocks = range(kv * per_kv, (kv + 1) * per_kv)
            q = jnp.concatenate([qr_ref[rows, c * LANES:(c + 1) * LANES] for c in blocks], axis=0)
            out = None
            for half in range(2):
                k = kr_ref[2 * kv + half, r * BLOCK:(r + 2) * BLOCK, :]
                v = vr_ref[2 * kv + half, r * BLOCK:(r + 2) * BLOCK, :]
                sink = jnp.concatenate(
                    [jnp.full((BLOCK, 1), sink_ref[(pair * nblk + c) * 2 + half], F32)
                     for c in blocks], axis=0)
                s = lax.dot_general(q, k, (((1,), (1,)), ((), ())),
                                    preferred_element_type=F32) * scale
                s = jnp.where(mask, s, -jnp.inf)
                m = jnp.maximum(jnp.max(s, axis=-1, keepdims=True), sink)
                p = jnp.exp(s - m)
                l = jnp.sum(p, axis=-1, keepdims=True) + jnp.exp(sink - m)
                o = jnp.dot(p.astype(BF16), v, preferred_element_type=F32) / l
                out = o if out is None else out + o
            for n, c in enumerate(blocks):
                o_ref[rows, c * LANES:(c + 1) * LANES] = (
                    out[n * BLOCK:(n + 1) * BLOCK].astype(o_ref.dtype))


def _swa_attention(qkv, cos, sin, sinks, q_heads, tq_cap=1024):
    b, s, _ = qkv.shape
    d = SWA_HEAD_DIM
    kv_heads = q_heads // SWA_GROUP
    per_blk = LANES // d
    assert kv_heads % per_blk == 0
    nq = q_heads * d
    qw = per_blk * SWA_GROUP * d
    kcol = nq // LANES
    vcol = (nq + kv_heads * d) // LANES
    tq = _tile(s, tq_cap)
    rb = tq // BLOCK

    def cur(col0):
        return pl.BlockSpec((None, tq, LANES), lambda bi, p, i: (bi, i, col0 + p))

    def prev(col0):
        return pl.BlockSpec((None, BLOCK, LANES),
                            lambda bi, p, i: (bi, jnp.maximum(i * rb - 1, 0), col0 + p))

    tab_c = pl.BlockSpec((None, tq, LANES), lambda bi, p, i: (bi, i, 0))
    tab_p = pl.BlockSpec((None, BLOCK, LANES), lambda bi, p, i: (bi, jnp.maximum(i * rb - 1, 0), 0))
    kern = functools.partial(_swa_kernel, tq=tq, scale=d ** -0.5)
    return pl.pallas_call(
        kern,
        out_shape=jax.ShapeDtypeStruct((b, s, nq), BF16),
        grid=(b, kv_heads // per_blk, s // tq),
        in_specs=[
            pl.BlockSpec(memory_space=pltpu.SMEM),
            pl.BlockSpec((None, tq, qw), lambda bi, p, i: (bi, i, p)),
            cur(kcol), prev(kcol), cur(vcol), prev(vcol),
            tab_c, tab_p, tab_c, tab_p,
        ],
        out_specs=pl.BlockSpec((None, tq, qw), lambda bi, p, i: (bi, i, p)),
        scratch_shapes=[pltpu.VMEM((tq, qw), BF16),
                        pltpu.VMEM((2 * per_blk, tq + BLOCK, LANES), BF16),
                        pltpu.VMEM((2 * per_blk, tq + BLOCK, LANES), BF16)],
        compiler_params=_params("parallel", "parallel", "parallel"),
        name="swa_attention",
    )(sinks, qkv, qkv, qkv, qkv, qkv, cos, cos, sin, sin)


def _mix_xattn_kernel(*refs, groups, scale):
    if groups:
        o_refs, l_refs = refs[:groups], refs[groups:2 * groups]
        (wm_ref, bm_ref, x_ref, g_ref, wq_ref, kv_ref, wo_ref, out_ref,
         att_ref, a_ref) = refs[2 * groups:]
        lses = [r[...] for r in l_refs]
        mx = functools.reduce(jnp.maximum, lses)
        es = [jnp.exp(l - mx) for l in lses]
        inv = 1.0 / functools.reduce(lambda u, v: u + v, es)
        gw = o_refs[0].shape[1]
        for gi in range(groups):
            a_ref[:, gi * gw:(gi + 1) * gw] = (o_refs[gi][...] * (es[gi] * inv)).astype(BF16)
        a = a_ref[...]
    else:
        a_in, wm_ref, bm_ref, x_ref, g_ref, wq_ref, kv_ref, wo_ref, out_ref, att_ref = refs
        a = a_in[...]
    x1 = x_ref[...] + jnp.dot(a, wm_ref[...], preferred_element_type=F32) + bm_ref[...]
    q = jnp.dot(_rms(x1, g_ref[...]).astype(BF16), wq_ref[...],
                preferred_element_type=F32).astype(BF16)
    width = XA_HEADS * XA_HEAD_DIM
    for h in range(XA_HEADS):
        sl = slice(h * XA_HEAD_DIM, (h + 1) * XA_HEAD_DIM)
        k = kv_ref[:, sl]
        v = kv_ref[:, width + h * XA_HEAD_DIM:width + (h + 1) * XA_HEAD_DIM]
        s = lax.dot_general(q[:, sl], k, (((1,), (1,)), ((), ())),
                            preferred_element_type=F32) * scale
        m = jnp.max(s, axis=-1, keepdims=True)
        p = jnp.exp(s - m)
        l = jnp.sum(p, axis=-1, keepdims=True)
        att_ref[:, sl] = (jnp.dot(p.astype(BF16), v, preferred_element_type=F32) / l).astype(BF16)
    out_ref[...] = x1 + jnp.dot(att_ref[...], wo_ref[...], preferred_element_type=F32)


def _mix_xattn(parts, w_mix, mix_layer, bias, x, g, w_q, kv, w_o, layer, seq, tm_cap=512):
    m, d = x.shape
    n_mem = kv.shape[1]
    k = w_mix.shape[1]
    width = XA_HEADS * XA_HEAD_DIM
    groups = len(parts) // 2 if len(parts) > 1 else 0
    tm = _tile(seq, tm_cap, 8)
    per_batch = seq // tm
    if bias is None:
        bias = jnp.zeros((d,), F32)
    once = pl.Buffered(1)
    in_specs = [pl.BlockSpec((tm, p.shape[1]), lambda i: (i, 0)) for p in parts] + [
        pl.BlockSpec((None, k, d), lambda i: (mix_layer, 0, 0), pipeline_mode=once),
        pl.BlockSpec((1, d), lambda i: (0, 0)),
        pl.BlockSpec((tm, d), lambda i: (i, 0)),
        pl.BlockSpec((1, d), lambda i: (0, 0)),
        pl.BlockSpec((None, d, width), lambda i: (layer, 0, 0), pipeline_mode=once),
        pl.BlockSpec((None, n_mem, 2 * width), lambda i: (i // per_batch, 0, 0)),
        pl.BlockSpec((None, width, d), lambda i: (layer, 0, 0), pipeline_mode=once),
    ]
    scratch = [pltpu.VMEM((tm, width), BF16)]
    if groups:
        scratch.append(pltpu.VMEM((tm, k), BF16))
    return pl.pallas_call(
        functools.partial(_mix_xattn_kernel, groups=groups, scale=XA_HEAD_DIM ** -0.5),
        out_shape=jax.ShapeDtypeStruct((m, d), F32),
        grid=(m // tm,),
        in_specs=in_specs,
        out_specs=pl.BlockSpec((tm, d), lambda i: (i, 0)),
        scratch_shapes=scratch,
        compiler_params=_params("parallel"),
        name="mix_xattn",
    )(*parts, w_mix, bias.reshape(1, d), x, g.reshape(1, d), w_q, kv, w_o)


def kernel(x, mem, positions, ffn1_norm, ffn1_w_gate_up, ffn1_w_down, mix_norm, sb_w_qkv, sb_w_o, dil_w_qkv, dil_w_o, swa_w_qkv, swa_b_qkv, swa_sinks, swa_w_o, swa_b_o, xattn_norm, mem_norm, xattn_w_q, xattn_w_kv, xattn_w_o, ffn2_norm, ffn2_w_gate_up, ffn2_w_down, final_norm):
    b, s, d = x.shape
    m = b * s
    depth = ffn1_norm.shape[0]
    n_mem = mem.shape[1]
    sb_heads = d // HEAD_DIM
    dil_hpg = d // (4 * HEAD_DIM)
    dil_heads = dil_hpg * len(DIL_PATTERNS)
    swa_q_heads = d // SWA_HEAD_DIM

    xs = x.reshape(m, d)
    mem2 = mem.reshape(b * n_mem, d)
    pos = positions.reshape(m, 1)
    if depth > 1:
        dil_cos, dil_sin = (t.reshape(b, s, LANES) for t in _rope_tables(pos, HEAD_DIM))
    if depth > 2:
        swa_cos, swa_sin = (t.reshape(b, s, LANES) for t in _rope_tables(pos, SWA_HEAD_DIM))

    mixer_weights = ((sb_w_qkv, sb_w_o), (dil_w_qkv, dil_w_o), (swa_w_qkv, swa_w_o))

    def after_ffn1(i):
        return ([(ffn2_w_gate_up, i), (ffn2_w_down, i)]
                + [(w, i // 3) for w in mixer_weights[i % 3]]
                + [(xattn_w_q, i), (xattn_w_kv, i), (xattn_w_o, i)])

    def cast_up_front(pairs):
        return [w[layer].astype(BF16) for w, layer in pairs]

    shapes = {w.shape[1:] for i in range(depth) for w, _ in after_ffn1(i)}
    shapes |= {ffn1_w_gate_up.shape[1:], ffn1_w_down.shape[1:]}
    chain = _ffn_can_cast(m, ffn1_w_down.shape[1], shapes)

    w1 = cast_up_front([(ffn1_w_gate_up, 0), (ffn1_w_down, 0)])
    for i in range(depth):
        if chain:
            xs, cast = _ffn(xs, ffn1_norm[i], *w1, casts=after_ffn1(i))
        else:
            xs, _ = _ffn(xs, ffn1_norm[i], *w1)
            cast = cast_up_front(after_ffn1(i))
        w2 = cast[:2]
        w_qkv, w_mix, xw_q, xw_kv, xw_o = (w[None] for w in cast[2:])

        kv = _norm_matmul(mem2, mem_norm[i], xw_kv, 0, None, BF16).reshape(b, n_mem, -1)
        kind, j = i % 3, i // 3
        b_mix = None
        if kind == 0:
            qkv = _norm_matmul(xs, mix_norm[i], w_qkv, 0, None, BF16,
                               col_scale=_sb_q_scale(sb_heads))
            parts = [_sb_attention(qkv.reshape(b, s, -1), sb_heads).reshape(m, -1)]
        elif kind == 1:
            qkv = _norm_matmul(xs, mix_norm[i], w_qkv, 0, None, F32).reshape(b, s, -1)
            outs, lses = [], []
            for g, (window, dil) in enumerate(DIL_PATTERNS):
                o, lse = _band_attention(qkv, dil_cos, dil_sin, g, dil_hpg, dil_heads, window, dil)
                outs.append(o)
                lses.append(lse)
            parts = outs + lses
        else:
            qkv = _norm_matmul(xs, mix_norm[i], w_qkv, 0, swa_b_qkv[j], BF16)
            parts = [_swa_attention(qkv.reshape(b, s, -1), swa_cos, swa_sin, swa_sinks[j],
                                    swa_q_heads).reshape(m, -1)]
            b_mix = swa_b_o[j]
        xs = _mix_xattn(parts, w_mix, 0, b_mix, xs, xattn_norm[i], xw_q, kv, xw_o, 0, s)

        if i == depth - 1:
            xs, _ = _ffn(xs, ffn2_norm[i], *w2, final_g=final_norm)
        elif chain:
            xs, w1 = _ffn(xs, ffn2_norm[i], *w2,
                          casts=[(ffn1_w_gate_up, i + 1), (ffn1_w_down, i + 1)])
        else:
            xs, _ = _ffn(xs, ffn2_norm[i], *w2)
            w1 = cast_up_front([(ffn1_w_gate_up, i + 1), (ffn1_w_down, i + 1)])

    return xs.reshape(b, s, d)
```

```python
import functools

import jax
import jax.numpy as jnp
from jax import lax
from jax.experimental import pallas as pl
from jax.experimental.pallas import tpu as pltpu

F32 = jnp.float32
BF16 = jnp.bfloat16

NORM_EPS = 1e-6
ROPE_THETA = 500000.0
ROPE_FRACTION = 4
LOG2E = 1.4426950408889634
HEAD_DIM = 128
BLOCK = 128
BAND_BLOCKS_IN_FLIGHT = 8
DIL_PATTERNS = ((128, 1), (512, 4), (2048, 16))
SWA_HEAD_DIM = 64
SWA_GROUP = 8
SWA_WINDOW = 128
XA_HEADS = 4
XA_HEAD_DIM = 128

LANES = 128
V7X_VMEM_BYTES = 64 * 2**20
VMEM_LIMIT_BYTES = (V7X_VMEM_BYTES * 3) // 4
FFN_VMEM_LIMIT_BYTES = (V7X_VMEM_BYTES * 15) // 16
NORM_MATMUL_TILE_BYTES = 18 * 2**20


def _params(*semantics, vmem_limit=VMEM_LIMIT_BYTES):
    return pltpu.CompilerParams(dimension_semantics=semantics, vmem_limit_bytes=vmem_limit)


def _tile(n, cap, mult=LANES):
    if n <= cap:
        return n
    t = (cap // mult) * mult
    while t > mult and n % t:
        t -= mult
    assert n % t == 0, (n, cap, mult)
    return t


def _rms(x, g):
    ms = jnp.mean(x * x, axis=-1, keepdims=True)
    return x * lax.rsqrt(ms + NORM_EPS) * g


def _norm_matmul_kernel(x_ref, g_ref, w_ref, b_ref, s_ref, o_ref, xn_ref):
    def tile(first):
        if first:
            xn = _rms(x_ref[...], g_ref[...]).astype(BF16)
            xn_ref[...] = xn
        else:
            xn = xn_ref[...]
        acc = jnp.dot(xn, w_ref[...], preferred_element_type=F32)
        o_ref[...] = ((acc + b_ref[...]) * s_ref[...]).astype(o_ref.dtype)

    pl.when(pl.program_id(1) == 0)(functools.partial(tile, True))
    pl.when(pl.program_id(1) > 0)(functools.partial(tile, False))


def _norm_matmul(x, g, w, layer, bias, out_dtype, col_scale=None, tm_cap=1024):
    m, d = x.shape
    n = w.shape[2]
    tm = _tile(m, tm_cap, 8)
    col_bytes = 2 * (jnp.dtype(BF16).itemsize * d + jnp.dtype(out_dtype).itemsize * tm)
    tn = _tile(n, NORM_MATMUL_TILE_BYTES // col_bytes // LANES * LANES)
    if bias is None:
        bias = jnp.zeros((n,), F32)
    if col_scale is None:
        col_scale = jnp.ones((n,), F32)
    return pl.pallas_call(
        _norm_matmul_kernel,
        out_shape=jax.ShapeDtypeStruct((m, n), out_dtype),
        grid=(m // tm, n // tn),
        in_specs=[
            pl.BlockSpec((tm, d), lambda i, j: (i, 0)),
            pl.BlockSpec((1, d), lambda i, j: (0, 0)),
            pl.BlockSpec((None, d, tn), lambda i, j: (layer, 0, j)),
            pl.BlockSpec((1, tn), lambda i, j: (0, j)),
            pl.BlockSpec((1, tn), lambda i, j: (0, j)),
        ],
        out_specs=pl.BlockSpec((tm, tn), lambda i, j: (i, j)),
        scratch_shapes=[pltpu.VMEM((tm, d), BF16)],
        compiler_params=_params("parallel", "arbitrary"),
        name="norm_matmul",
    )(x, g.reshape(1, d), w, bias.reshape(1, n), col_scale.reshape(1, n))


def _ffn_kernel(x_ref, g_ref, wg_ref, wu_ref, wd_ref, gf_ref, *rest, nf, final_norm, convert):
    src_refs, o_ref, dst_refs, xn_ref = (rest[:convert], rest[convert],
                                         rest[convert + 1:2 * convert + 1], rest[-1])
    for src, dst in zip(src_refs, dst_refs):
        dst[...] = src[...].astype(BF16)
    j = pl.program_id(1)

    def chunk(first, last):
        if first:
            xn = _rms(x_ref[...], g_ref[...]).astype(BF16)
            xn_ref[...] = xn
        else:
            xn = xn_ref[...]
        halves = []
        width = wg_ref.shape[1] // 2
        for c in range(2):
            sl = slice(c * width, (c + 1) * width)
            gate = jnp.dot(xn, wg_ref[:, sl], preferred_element_type=F32)
            up = jnp.dot(xn, wu_ref[:, sl], preferred_element_type=F32)
            halves.append((gate * jax.nn.sigmoid(gate) * up).astype(BF16))
        down = jnp.dot(jnp.concatenate(halves, axis=1), wd_ref[...],
                       preferred_element_type=F32)
        if not first:
            down = o_ref[...] + down
        o_ref[...] = x_ref[...] + 0.5 * down if last else down

    if final_norm:
        pl.when(j == 0)(functools.partial(chunk, True, False))
        pl.when(j > 0)(functools.partial(chunk, False, False))

        @pl.when(j == nf - 1)
        def _():
            rows = min(256, o_ref.shape[0])

            def finish(c, carry):
                sl = pl.ds(pl.multiple_of(c * rows, rows), rows)
                o_ref[sl, :] = _rms(x_ref[sl, :] + 0.5 * o_ref[sl, :], gf_ref[...])
                return carry

            lax.fori_loop(0, o_ref.shape[0] // rows, finish, 0)
    elif nf == 1:
        chunk(True, True)
    else:
        pl.when(j == 0)(functools.partial(chunk, True, False))
        pl.when((j > 0) & (j < nf - 1))(functools.partial(chunk, False, False))
        pl.when(j == nf - 1)(functools.partial(chunk, False, True))


def _ffn_tiles(m, f, tm_cap=1024, tf_cap=512):
    return _tile(m, tm_cap, 8), _tile(f, tf_cap)


def _cast_plan(rows, cols, ni, nf):
    if rows % (ni * 16):
        return None
    for cw in range(LANES, cols + 1, LANES):
        if cols % cw == 0 and cols // cw <= nf:
            return rows // ni, cw, cols // cw
    return None


def _ffn_can_cast(m, f, shapes):
    tm, tf = _ffn_tiles(m, f)
    return all(_cast_plan(r, c, m // tm, f // tf) is not None for r, c in shapes)


def _ffn(x, g, w_gu, w_d, final_g=None, casts=()):
    m, d = x.shape
    f = w_d.shape[0]
    tm, tf = _ffn_tiles(m, f)
    nf = f // tf
    ni = m // tm
    gf = jnp.ones((d,), F32) if final_g is None else final_g
    in_specs = [
        pl.BlockSpec((tm, d), lambda i, j: (i, 0)),
        pl.BlockSpec((1, d), lambda i, j: (0, 0)),
        pl.BlockSpec((d, tf), lambda i, j: (0, j)),
        pl.BlockSpec((d, tf), lambda i, j: (0, j + nf)),
        pl.BlockSpec((tf, d), lambda i, j: (j, 0)),
        pl.BlockSpec((1, d), lambda i, j: (0, 0)),
    ]
    args = [x, g.reshape(1, d), w_gu, w_gu, w_d, gf.reshape(1, d)]
    out_specs = [pl.BlockSpec((tm, d), lambda i, j: (i, 0))]
    out_shape = [jax.ShapeDtypeStruct((m, d), F32)]
    for src, layer in casts:
        rows, cols = src.shape[1:]
        rb, cw, chunks = _cast_plan(rows, cols, ni, nf)
        in_specs.append(pl.BlockSpec(
            (None, rb, cw), lambda i, j, layer=layer, chunks=chunks:
            (layer, i, jnp.minimum(j, chunks - 1))))
        args.append(src)
    for src, layer in casts:
        rows, cols = src.shape[1:]
        rb, cw, chunks = _cast_plan(rows, cols, ni, nf)
        out_specs.append(pl.BlockSpec(
            (rb, cw), lambda i, j, chunks=chunks: (i, jnp.minimum(j, chunks - 1))))
        out_shape.append(jax.ShapeDtypeStruct((rows, cols), BF16))
    outs = pl.pallas_call(
        functools.partial(_ffn_kernel, nf=nf, final_norm=final_g is not None,
                          convert=len(casts)),
        out_shape=out_shape,
        grid=(ni, nf),
        in_specs=in_specs,
        out_specs=out_specs,
        scratch_shapes=[pltpu.VMEM((tm, d), BF16)],
        compiler_params=_params("parallel", "arbitrary", vmem_limit=FFN_VMEM_LIMIT_BYTES),
        name="ffn",
    )(*args)
    return outs[0], list(outs[1:])


def _sb_kernel(q_ref, k_ref, v_ref, tri_ref, o_ref, acc_ref, car_ref, z_ref, *, tq, tk):
    row0 = pl.program_id(2) * tq
    acc_ref[...] = jnp.zeros_like(acc_ref)
    tri = tri_ref[...]
    nsub = tq // tk
    nheads = q_ref.shape[1] // HEAD_DIM
    assert nheads % 2 == 0
    cols = [slice(h * HEAD_DIM, (h + 1) * HEAD_DIM) for h in range(nheads)]

    def logits(h, ks):
        return lax.dot_general(q_ref[:, cols[h]], k_ref[pl.ds(ks, tq), cols[h]],
                               (((1,), (1,)), ((), ())), preferred_element_type=F32)

    def neg_log2_keep(z2):
        neg_abs = lax.bitcast_convert_type(
            lax.bitcast_convert_type(z2, jnp.uint32) | jnp.uint32(0x80000000), F32)
        return jnp.maximum(z2, 0.0) + jnp.log2(1.0 + jnp.exp2(neg_abs))

    def prefetch(h, ks, ks_next):
        if h + 1 < nheads:
            z_ref[(h + 1) % 2] = logits(h + 1, ks)
        else:
            z_ref[0] = logits(0, ks_next)

    def full_tile(ks, ks_next):
        for h, hs in enumerate(cols):
            prefetch(h, ks, ks_next)
            zr = z_ref.at[h % 2]
            nlb = neg_log2_keep(zr[...]).astype(BF16)
            car = car_ref[:, hs]
            parts = [None] * nsub
            for j in reversed(range(nsub)):
                sl = slice(j * tk, (j + 1) * tk)
                ssum = jnp.dot(nlb[:, sl], tri, preferred_element_type=F32)
                parts[j] = jnp.exp2(zr[:, sl] - ssum
                                    - jnp.concatenate([car] * (tk // LANES), axis=1))
                car = car + jnp.broadcast_to(ssum[:, 0:1], (tq, LANES))
            a = jnp.concatenate(parts, axis=1)
            acc_ref[:, hs] += jnp.dot(a.astype(BF16), v_ref[pl.ds(ks, tq), hs],
                                      preferred_element_type=F32)
            car_ref[:, hs] = car

    def diagonal_tile(ks, ks_next):
        lower = (lax.broadcasted_iota(jnp.int32, (tk, tk), 1)
                 < lax.broadcasted_iota(jnp.int32, (tk, tk), 0))
        for h, hs in enumerate(cols):
            prefetch(h, ks, ks_next)
            zr = z_ref.at[h % 2]
            car = jnp.zeros((tq, LANES), F32)
            for j in reversed(range(nsub)):
                r0 = j * tk
                nr = tq - r0
                sl = slice(j * tk, (j + 1) * tk)
                mask = lower if nr == tk else jnp.concatenate(
                    [lower, jnp.ones((nr - tk, tk), jnp.bool_)], axis=0)
                z2 = zr[r0:, sl]
                nl = jnp.where(mask, neg_log2_keep(z2), 0.0)
                ssum = jnp.dot(nl.astype(BF16), tri, preferred_element_type=F32)
                a = jnp.exp2(z2 - ssum - jnp.concatenate([car[r0:]] * (tk // LANES), axis=1))
                a = jnp.where(mask, a, 0.0)
                acc_ref[r0:, hs] += jnp.dot(a.astype(BF16), v_ref[pl.ds(ks + r0, tk), hs],
                                            preferred_element_type=F32)
                upd = car[r0:] + jnp.broadcast_to(ssum[:, 0:1], (nr, LANES))
                car = upd if r0 == 0 else jnp.concatenate([car[:r0], upd], axis=0)
            car_ref[:, hs] = car

    def start(t):
        return pl.multiple_of(jnp.maximum(row0 - (t + 1) * tq, 0), tq)

    z_ref[0] = logits(0, pl.multiple_of(row0, tq))
    diagonal_tile(pl.multiple_of(row0, tq), start(0))

    def body(t, c):
        full_tile(start(t), start(t + 1))
        return c

    lax.fori_loop(0, pl.program_id(2), body, 0)
    o_ref[...] = acc_ref[...].astype(o_ref.dtype)


def _sb_q_scale(heads):
    width = heads * HEAD_DIM
    return jnp.concatenate([jnp.full((width,), HEAD_DIM ** -0.5 * LOG2E, F32),
                            jnp.ones((2 * width,), F32)])


def _sb_attention(qkv, heads, tq_cap=512, tk_cap=256, heads_per_step=4):
    b, s, _ = qkv.shape
    tq = _tile(s, tq_cap)
    tk = _tile(tq, tk_cap)
    hp = heads_per_step if heads % heads_per_step == 0 else 1
    hw = hp * HEAD_DIM
    hb = heads // hp
    tri = jnp.tril(jnp.ones((tk, tk), F32)).astype(BF16)
    kern = functools.partial(_sb_kernel, tq=tq, tk=tk)
    return pl.pallas_call(
        kern,
        out_shape=jax.ShapeDtypeStruct((b, s, heads * HEAD_DIM), BF16),
        grid=(b, hb, s // tq),
        in_specs=[
            pl.BlockSpec((None, tq, hw), lambda bi, h, i: (bi, i, h)),
            pl.BlockSpec((None, s, hw), lambda bi, h, i: (bi, 0, hb + h)),
            pl.BlockSpec((None, s, hw), lambda bi, h, i: (bi, 0, 2 * hb + h)),
            pl.BlockSpec((tk, tk), lambda bi, h, i: (0, 0)),
        ],
        out_specs=pl.BlockSpec((None, tq, hw), lambda bi, h, i: (bi, i, h)),
        scratch_shapes=[pltpu.VMEM((tq, hw), F32), pltpu.VMEM((tq, hw), F32),
                        pltpu.VMEM((2, tq, tq), F32)],
        compiler_params=_params("parallel", "parallel", "arbitrary"),
        name="sb_attention",
    )(qkv, qkv, qkv, tri)


def _rope_table_kernel(pos_ref, freq_ref, sign_ref, cos_ref, sin_ref):
    ang = pos_ref[...].astype(F32) * freq_ref[...]
    cos_ref[...] = jnp.cos(ang)
    sin_ref[...] = jnp.sin(ang) * sign_ref[...]


def _rope_tables(positions, head_dim):
    m = positions.shape[0]
    rot = head_dim // ROPE_FRACTION
    half = rot // 2
    inv_freq = jnp.power(F32(ROPE_THETA), -jnp.arange(half, dtype=F32) * 2.0 / rot)
    lane = jnp.arange(LANES) % head_dim
    freq = jnp.where(lane < rot, inv_freq[lane % half], 0.0).astype(F32)
    sign = jnp.where(lane < half, -1.0, jnp.where(lane < rot, 1.0, 0.0)).astype(F32)
    tm = _tile(m, 2048, 8)
    out = jax.ShapeDtypeStruct((m, LANES), F32)
    return pl.pallas_call(
        _rope_table_kernel,
        out_shape=(out, out),
        grid=(m // tm,),
        in_specs=[
            pl.BlockSpec((tm, 1), lambda i: (i, 0)),
            pl.BlockSpec((1, LANES), lambda i: (0, 0)),
            pl.BlockSpec((1, LANES), lambda i: (0, 0)),
        ],
        out_specs=(pl.BlockSpec((tm, LANES), lambda i: (i, 0)),
                   pl.BlockSpec((tm, LANES), lambda i: (i, 0))),
        compiler_params=_params("parallel"),
        name="rope_tables",
    )(positions, freq.reshape(1, LANES), sign.reshape(1, LANES))


def _rope_partner_matrix(head_dim):
    rot = head_dim // ROPE_FRACTION
    half = rot // 2
    lane = jnp.arange(LANES)
    inner = lane % head_dim
    partner = jnp.where(inner < half, lane + half, lane - half)
    return ((lane[:, None] == partner[None, :]) & (inner < rot)[None, :]).astype(BF16)


def _rope_mxu(x, cos, sin, perm):
    hi = x.astype(BF16)
    lo = (x - hi.astype(F32)).astype(BF16)
    partner = (jnp.dot(hi, perm, preferred_element_type=F32)
               + jnp.dot(lo, perm, preferred_element_type=F32))
    return x * cos + partner * sin


def _rope_roll(x, cos, sin, head_dim):
    half = head_dim // ROPE_FRACTION // 2
    lane = lax.broadcasted_iota(jnp.int32, x.shape, 1) & (head_dim - 1)
    ahead = pltpu.roll(x, LANES - half, 1)
    behind = pltpu.roll(x, half, 1)
    return x * cos + jnp.where(lane < half, ahead, behind) * sin


def _band_kernel(q_ref, k_ref, v_ref, cos_ref, sin_ref, perm_ref, o_ref, lse_ref,
                 qr_ref, kr_ref, vr_ref, *, dil, length, max_dist, scale):
    perm = perm_ref[...]
    kj = lax.broadcasted_iota(jnp.int32, (BLOCK, 2 * BLOCK), 1)
    dist = BLOCK + lax.broadcasted_iota(jnp.int32, (BLOCK, 2 * BLOCK), 0) - kj
    band = (dist >= 0) & (dist <= max_dist)
    band0 = band & (kj >= BLOCK)
    nblocks = length // BLOCK
    kr_ref[:, 0:BLOCK, :] = jnp.zeros((kr_ref.shape[0], BLOCK, LANES), BF16)
    vr_ref[:, 0:BLOCK, :] = jnp.zeros((vr_ref.shape[0], BLOCK, LANES), BF16)

    def block(p, slot, a, mask):
        q0 = pl.multiple_of(a * BLOCK, BLOCK)
        q = qr_ref[slot, pl.ds(q0, BLOCK), :]
        k = kr_ref[slot, pl.ds(q0, 2 * BLOCK), :]
        v = vr_ref[slot, pl.ds(q0, 2 * BLOCK), :]
        s = lax.dot_general(q, k, (((1,), (1,)), ((), ())),
                            preferred_element_type=F32) * scale
        s = jnp.where(mask, s, -jnp.inf)
        m = jnp.max(s, axis=-1, keepdims=True)
        e = jnp.exp(s - m)
        l = jnp.sum(e, axis=-1, keepdims=True)
        o = jnp.dot(e.astype(BF16), v, preferred_element_type=F32) / l
        dest = pl.ds(p + q0 * dil, BLOCK, stride=dil)
        o_ref[dest, :] = o
        lse_ref[dest, :] = jnp.broadcast_to(m + jnp.log(l), (BLOCK, LANES))

    ncls = qr_ref.shape[0]
    group = min(BAND_BLOCKS_IN_FLIGHT, nblocks)
    assert nblocks % group == 0 and dil % ncls == 0

    def classes(t, c):
        ps = [t * ncls + slot for slot in range(ncls)]
        for slot, p in enumerate(ps):
            rows = pl.ds(p, length, stride=dil)
            cos = cos_ref[rows, :]
            sin = sin_ref[rows, :]
            qr_ref[slot] = _rope_mxu(q_ref[rows, :], cos, sin, perm).astype(BF16)
            kr_ref[slot, BLOCK:, :] = _rope_mxu(k_ref[rows, :], cos, sin, perm).astype(BF16)
            vr_ref[slot, BLOCK:, :] = v_ref[rows, :].astype(BF16)
        for slot, p in enumerate(ps):
            block(p, slot, 0, band0)
            for a in range(1, group):
                block(p, slot, a, band)

        def trip(u, c2):
            for slot, p in enumerate(ps):
                for a in range(group):
                    block(p, slot, group * u + a, band)
            return c2

        lax.fori_loop(1, nblocks // group, trip, 0)
        return c

    lax.fori_loop(0, dil // ncls, classes, 0)


def _band_attention(qkv, cos, sin, group, hpg, n_heads, window, dil):
    b, s, n = qkv.shape
    length = s // dil
    assert length % BLOCK == 0 and window % dil == 0 and window // dil <= BLOCK
    h0 = group * hpg

    def col(which):
        return pl.BlockSpec((None, s, LANES), lambda bi, h: (bi, 0, which * n_heads + h0 + h))

    tab = pl.BlockSpec((None, s, LANES), lambda bi, h: (bi, 0, 0))
    out = pl.BlockSpec((None, s, LANES), lambda bi, h: (bi, 0, h))
    shape = jax.ShapeDtypeStruct((b, s, hpg * LANES), F32)
    ncls = min(dil, max(1, (BAND_BLOCKS_IN_FLIGHT * BLOCK) // length))
    kern = functools.partial(_band_kernel, dil=dil, length=length, max_dist=window // dil,
                             scale=HEAD_DIM ** -0.5)
    o, lse = pl.pallas_call(
        kern,
        out_shape=(shape, shape),
        grid=(b, hpg),
        in_specs=[col(0), col(1), col(2), tab, tab,
                  pl.BlockSpec((LANES, LANES), lambda bi, h: (0, 0))],
        out_specs=(out, out),
        scratch_shapes=[pltpu.VMEM((ncls, length, LANES), BF16),
                        pltpu.VMEM((ncls, length + BLOCK, LANES), BF16),
                        pltpu.VMEM((ncls, length + BLOCK, LANES), BF16)],
        compiler_params=_params("parallel", "parallel"),
        name=f"band_attention_d{dil}",
    )(qkv, qkv, qkv, cos, sin, _rope_partner_matrix(HEAD_DIM))
    return (o.reshape(b * s, hpg * LANES), lse.reshape(b * s, hpg * LANES))


def _swa_kernel(sink_ref, q_ref, kc_ref, kp_ref, vc_ref, vp_ref, cc_ref, cp_ref, sc_ref, sp_ref,
                o_ref, qr_ref, kr_ref, vr_ref, *, tq, scale):
    pair = pl.program_id(1)
    has_prev = pl.program_id(2) > 0
    d = SWA_HEAD_DIM
    cos = cc_ref[...]
    sin = sc_ref[...]
    kx = jnp.concatenate([_rope_roll(kp_ref[...].astype(F32), cp_ref[...], sp_ref[...], d),
                          _rope_roll(kc_ref[...].astype(F32), cos, sin, d)], axis=0)
    vx = jnp.concatenate([vp_ref[...], vc_ref[...]], axis=0).astype(F32)
    low = lax.broadcasted_iota(jnp.int32, kx.shape, 1) < d
    for src, dst in ((kx, kr_ref), (vx, vr_ref)):
        swapped = pltpu.roll(src, d, 1)
        dst[0] = jnp.where(low, src, 0.0).astype(BF16)
        dst[1] = jnp.where(low, 0.0, swapped).astype(BF16)
        dst[2] = jnp.where(low, swapped, 0.0).astype(BF16)
        dst[3] = jnp.where(low, 0.0, src).astype(BF16)
    nblk = q_ref.shape[1] // LANES
    for c in range(nblk):
        sl = slice(c * LANES, (c + 1) * LANES)
        qr_ref[:, sl] = _rope_roll(q_ref[:, sl].astype(F32), cos, sin, d).astype(BF16)

    qi = lax.broadcasted_iota(jnp.int32, (BLOCK, 2 * BLOCK), 0)
    kj = lax.broadcasted_iota(jnp.int32, (BLOCK, 2 * BLOCK), 1)
    dist = BLOCK + qi - kj
    band = (dist >= 0) & (dist <= SWA_WINDOW - 1)
    band0 = band & ((kj >= BLOCK) | has_prev)
    per_kv = nblk // (LANES // d)
    for r in range(tq // BLOCK):
        rows = slice(r * BLOCK, (r + 1) * BLOCK)
        mask = jnp.concatenate([band0 if r == 0 else band] * per_kv, axis=0)
        for kv in range(LANES // d):
            blocks = range(kv * per_kv, (kv + 1) * per_kv)
            q = jnp.concatenate([qr_ref[rows, c * LANES:(c + 1) * LANES] for c in blocks], axis=0)
            out = None
            for half in range(2):
                k = kr_ref[2 * kv + half, r * BLOCK:(r + 2) * BLOCK, :]
                v = vr_ref[2 * kv + half, r * BLOCK:(r + 2) * BLOCK, :]
                sink = jnp.concatenate(
                    [jnp.full((BLOCK, 1), sink_ref[(pair * nblk + c) * 2 + half], F32)
                     for c in blocks], axis=0)
                s = lax.dot_general(q, k, (((1,), (1,)), ((), ())),
                                    preferred_element_type=F32) * scale
                s = jnp.where(mask, s, -jnp.inf)
                m = jnp.maximum(jnp.max(s, axis=-1, keepdims=True), sink)
                p = jnp.exp(s - m)
                l = jnp.sum(p, axis=-1, keepdims=True) + jnp.exp(sink - m)
                o = jnp.dot(p.astype(BF16), v, preferred_element_type=F32) / l
                out = o if out is None else out + o
            for n, c in enumerate(blocks):
                o_ref[rows, c * LANES:(c + 1) * LANES] = (
                    out[n * BLOCK:(n + 1) * BLOCK].astype(o_ref.dtype))


def _swa_attention(qkv, cos, sin, sinks, q_heads, tq_cap=512):
    b, s, _ = qkv.shape
    d = SWA_HEAD_DIM
    kv_heads = q_heads // SWA_GROUP
    per_blk = LANES // d
    assert kv_heads % per_blk == 0
    nq = q_heads * d
    qw = per_blk * SWA_GROUP * d
    kcol = nq // LANES
    vcol = (nq + kv_heads * d) // LANES
    tq = _tile(s, tq_cap)
    rb = tq // BLOCK

    def cur(col0):
        return pl.BlockSpec((None, tq, LANES), lambda bi, p, i: (bi, i, col0 + p))

    def prev(col0):
        return pl.BlockSpec((None, BLOCK, LANES),
                            lambda bi, p, i: (bi, jnp.maximum(i * rb - 1, 0), col0 + p))

    tab_c = pl.BlockSpec((None, tq, LANES), lambda bi, p, i: (bi, i, 0))
    tab_p = pl.BlockSpec((None, BLOCK, LANES), lambda bi, p, i: (bi, jnp.maximum(i * rb - 1, 0), 0))
    kern = functools.partial(_swa_kernel, tq=tq, scale=d ** -0.5)
    return pl.pallas_call(
        kern,
        out_shape=jax.ShapeDtypeStruct((b, s, nq), BF16),
        grid=(b, kv_heads // per_blk, s // tq),
        in_specs=[
            pl.BlockSpec(memory_space=pltpu.SMEM),
            pl.BlockSpec((None, tq, qw), lambda bi, p, i: (bi, i, p)),
            cur(kcol), prev(kcol), cur(vcol), prev(vcol),
            tab_c, tab_p, tab_c, tab_p,
        ],
        out_specs=pl.BlockSpec((None, tq, qw), lambda bi, p, i: (bi, i, p)),
        scratch_shapes=[pltpu.VMEM((tq, qw), BF16),
                        pltpu.VMEM((2 * per_blk, tq + BLOCK, LANES), BF16),
                        pltpu.VMEM((2 * per_blk, tq + BLOCK, LANES), BF16)],
        compiler_params=_params("parallel", "parallel", "parallel"),
        name="swa_attention",
    )(sinks, qkv, qkv, qkv, qkv, qkv, cos, cos, sin, sin)


def _mix_xattn_kernel(*refs, groups, scale):
    if groups:
        o_refs, l_refs = refs[:groups], refs[groups:2 * groups]
        (wm_ref, bm_ref, x_ref, g_ref, wq_ref, kv_ref, wo_ref, out_ref,
         att_ref, a_ref) = refs[2 * groups:]
        lses = [r[...] for r in l_refs]
        mx = functools.reduce(jnp.maximum, lses)
        es = [jnp.exp(l - mx) for l in lses]
        inv = 1.0 / functools.reduce(lambda u, v: u + v, es)
        gw = o_refs[0].shape[1]
        for gi in range(groups):
            a_ref[:, gi * gw:(gi + 1) * gw] = (o_refs[gi][...] * (es[gi] * inv)).astype(BF16)
        a = a_ref[...]
    else:
        a_in, wm_ref, bm_ref, x_ref, g_ref, wq_ref, kv_ref, wo_ref, out_ref, att_ref = refs
        a = a_in[...]
    x1 = x_ref[...] + jnp.dot(a, wm_ref[...], preferred_element_type=F32) + bm_ref[...]
    q = jnp.dot(_rms(x1, g_ref[...]).astype(BF16), wq_ref[...],
                preferred_element_type=F32).astype(BF16)
    width = XA_HEADS * XA_HEAD_DIM
    for h in range(XA_HEADS):
        sl = slice(h * XA_HEAD_DIM, (h + 1) * XA_HEAD_DIM)
        k = kv_ref[:, sl]
        v = kv_ref[:, width + h * XA_HEAD_DIM:width + (h + 1) * XA_HEAD_DIM]
        s = lax.dot_general(q[:, sl], k, (((1,), (1,)), ((), ())),
                            preferred_element_type=F32) * scale
        m = jnp.max(s, axis=-1, keepdims=True)
        p = jnp.exp(s - m)
        l = jnp.sum(p, axis=-1, keepdims=True)
        att_ref[:, sl] = (jnp.dot(p.astype(BF16), v, preferred_element_type=F32) / l).astype(BF16)
    out_ref[...] = x1 + jnp.dot(att_ref[...], wo_ref[...], preferred_element_type=F32)


def _mix_xattn(parts, w_mix, mix_layer, bias, x, g, w_q, kv, w_o, layer, seq, tm_cap=512):
    m, d = x.shape
    n_mem = kv.shape[1]
    k = w_mix.shape[1]
    width = XA_HEADS * XA_HEAD_DIM
    groups = len(parts) // 2 if len(parts) > 1 else 0
    tm = _tile(seq, tm_cap, 8)
    per_batch = seq // tm
    if bias is None:
        bias = jnp.zeros((d,), F32)
    once = pl.Buffered(1)
    in_specs = [pl.BlockSpec((tm, p.shape[1]), lambda i: (i, 0)) for p in parts] + [
        pl.BlockSpec((None, k, d), lambda i: (mix_layer, 0, 0), pipeline_mode=once),
        pl.BlockSpec((1, d), lambda i: (0, 0)),
        pl.BlockSpec((tm, d), lambda i: (i, 0)),
        pl.BlockSpec((1, d), lambda i: (0, 0)),
        pl.BlockSpec((None, d, width), lambda i: (layer, 0, 0), pipeline_mode=once),
        pl.BlockSpec((None, n_mem, 2 * width), lambda i: (i // per_batch, 0, 0)),
        pl.BlockSpec((None, width, d), lambda i: (layer, 0, 0), pipeline_mode=once),
    ]
    scratch = [pltpu.VMEM((tm, width), BF16)]
    if groups:
        scratch.append(pltpu.VMEM((tm, k), BF16))
    return pl.pallas_call(
        functools.partial(_mix_xattn_kernel, groups=groups, scale=XA_HEAD_DIM ** -0.5),
        out_shape=jax.ShapeDtypeStruct((m, d), F32),
        grid=(m // tm,),
        in_specs=in_specs,
        out_specs=pl.BlockSpec((tm, d), lambda i: (i, 0)),
        scratch_shapes=scratch,
        compiler_params=_params("parallel"),
        name="mix_xattn",
    )(*parts, w_mix, bias.reshape(1, d), x, g.reshape(1, d), w_q, kv, w_o)


def kernel(x, mem, positions, ffn1_norm, ffn1_w_gate_up, ffn1_w_down, mix_norm, sb_w_qkv, sb_w_o, dil_w_qkv, dil_w_o, swa_w_qkv, swa_b_qkv, swa_sinks, swa_w_o, swa_b_o, xattn_norm, mem_norm, xattn_w_q, xattn_w_kv, xattn_w_o, ffn2_norm, ffn2_w_gate_up, ffn2_w_down, final_norm):
    b, s, d = x.shape
    m = b * s
    depth = ffn1_norm.shape[0]
    n_mem = mem.shape[1]
    sb_heads = d // HEAD_DIM
    dil_hpg = d // (4 * HEAD_DIM)
    dil_heads = dil_hpg * len(DIL_PATTERNS)
    swa_q_heads = d // SWA_HEAD_DIM

    xs = x.reshape(m, d)
    mem2 = mem.reshape(b * n_mem, d)
    pos = positions.reshape(m, 1)
    if depth > 1:
        dil_cos, dil_sin = (t.reshape(b, s, LANES) for t in _rope_tables(pos, HEAD_DIM))
    if depth > 2:
        swa_cos, swa_sin = (t.reshape(b, s, LANES) for t in _rope_tables(pos, SWA_HEAD_DIM))

    mixer_weights = ((sb_w_qkv, sb_w_o), (dil_w_qkv, dil_w_o), (swa_w_qkv, swa_w_o))

    def after_ffn1(i):
        return ([(ffn2_w_gate_up, i), (ffn2_w_down, i)]
                + [(w, i // 3) for w in mixer_weights[i % 3]]
                + [(xattn_w_q, i), (xattn_w_kv, i), (xattn_w_o, i)])

    def cast_up_front(pairs):
        return [w[layer].astype(BF16) for w, layer in pairs]

    shapes = {w.shape[1:] for i in range(depth) for w, _ in after_ffn1(i)}
    shapes |= {ffn1_w_gate_up.shape[1:], ffn1_w_down.shape[1:]}
    chain = _ffn_can_cast(m, ffn1_w_down.shape[1], shapes)

    w1 = cast_up_front([(ffn1_w_gate_up, 0), (ffn1_w_down, 0)])
    for i in range(depth):
        if chain:
            xs, cast = _ffn(xs, ffn1_norm[i], *w1, casts=after_ffn1(i))
        else:
            xs, _ = _ffn(xs, ffn1_norm[i], *w1)
            cast = cast_up_front(after_ffn1(i))
        w2 = cast[:2]
        w_qkv, w_mix, xw_q, xw_kv, xw_o = (w[None] for w in cast[2:])

        kv = _norm_matmul(mem2, mem_norm[i], xw_kv, 0, None, BF16).reshape(b, n_mem, -1)
        kind, j = i % 3, i // 3
        b_mix = None
        if kind == 0:
            qkv = _norm_matmul(xs, mix_norm[i], w_qkv, 0, None, BF16,
                               col_scale=_sb_q_scale(sb_heads))
            parts = [_sb_attention(qkv.reshape(b, s, -1), sb_heads).reshape(m, -1)]
        elif kind == 1:
            qkv = _norm_matmul(xs, mix_norm[i], w_qkv, 0, None, F32).reshape(b, s, -1)
            outs, lses = [], []
            for g, (window, dil) in enumerate(DIL_PATTERNS):
                o, lse = _band_attention(qkv, dil_cos, dil_sin, g, dil_hpg, dil_heads, window, dil)
                outs.append(o)
                lses.append(lse)
            parts = outs + lses
        else:
            qkv = _norm_matmul(xs, mix_norm[i], w_qkv, 0, swa_b_qkv[j], BF16)
            parts = [_swa_attention(qkv.reshape(b, s, -1), swa_cos, swa_sin, swa_sinks[j],
                                    swa_q_heads).reshape(m, -1)]
            b_mix = swa_b_o[j]
        xs = _mix_xattn(parts, w_mix, 0, b_mix, xs, xattn_norm[i], xw_q, kv, xw_o, 0, s)

        if i == depth - 1:
            xs, _ = _ffn(xs, ffn2_norm[i], *w2, final_g=final_norm)
        elif chain:
            xs, w1 = _ffn(xs, ffn2_norm[i], *w2,
                          casts=[(ffn1_w_gate_up, i + 1), (ffn1_w_down, i + 1)])
        else:
            xs, _ = _ffn(xs, ffn2_norm[i], *w2)
            w1 = cast_up_front([(ffn1_w_gate_up, i + 1), (ffn1_w_down, i + 1)])

    return xs.reshape(b, s, d)
```
